```python
import math
import jax, jax.numpy as jnp
from jax import lax
import numpy as np


D_MODEL = 1024
BATCH = 2
SEQ = 8192
DEPTH = 2

SSM_GROUP = 16
SSM_STATE = 64
SSM_WIDTH = D_MODEL // 2
SSM_GROUPS = SSM_WIDTH // SSM_GROUP
HEAD_DIM = 64
N_HEADS = D_MODEL // HEAD_DIM
N_KV = N_HEADS // 4
HPG = N_HEADS // N_KV
ATTN_WIDTH = N_HEADS * HEAD_DIM
KV_WIDTH = N_KV * HEAD_DIM
CMP_BLOCK = 32
CMP_STRIDE = 16
CMP_HIDDEN = 4 * HEAD_DIM
SLC_BLOCK = 64
SLC_TOPK = 16
WINDOW = 512
Q_CHUNK = 64
FORCED_SCORE = 1e4
IN_WIDTH = ATTN_WIDTH + 6 * KV_WIDTH + 3 * N_HEADS + SSM_WIDTH + 2 * D_MODEL
N_GROUPS = 4
EXP_PER_GROUP = 8
N_EXPERTS = N_GROUPS * EXP_PER_GROUP
TOPK_IN_GROUP = 2
D_EXPERT = D_MODEL // 4
MOE_BLOCK = 128
EPS = 1e-6
NEG = -1e30

kernel_name = "hybrid_s5_nsa_hmoe_adaln"


def rms_norm(x, g):
    xf = x.astype(jnp.float32)
    y = xf * lax.rsqrt(jnp.mean(xf * xf, axis=-1, keepdims=True) + EPS)
    return (y * g.astype(jnp.float32)).astype(x.dtype)


def masked_softmax(s, mask):
    p = jax.nn.softmax(jnp.where(mask, s.astype(jnp.float32), NEG), axis=-1)
    return p * mask


def s5_mixer(u, lam_re, lam_im, log_dt, b_re, b_im, c_re, c_im, d_skip, w_glu):
    f32 = jnp.float32
    Bsz, L, _ = u.shape
    ug = u.astype(f32).reshape(Bsz, L, SSM_GROUPS, SSM_GROUP)
    dt = jnp.exp(log_dt.astype(f32))[:, None]
    lr, li = lam_re.astype(f32), lam_im.astype(f32)
    mag = jnp.exp(lr * dt)
    ab_re, ab_im = mag * jnp.cos(li * dt), mag * jnp.sin(li * dt)
    den = lr * lr + li * li
    nr = ab_re - 1.0
    cr = (nr * lr + ab_im * li) / den
    cim = (ab_im * lr - nr * li) / den
    br, bim = b_re.astype(f32), b_im.astype(f32)
    bb_re = cr[..., None] * br - cim[..., None] * bim
    bb_im = cr[..., None] * bim + cim[..., None] * br
    bu_re = jnp.einsum('blgh,gph->blgp', ug, bb_re)
    bu_im = jnp.einsum('blgh,gph->blgp', ug, bb_im)
    a_re = jnp.broadcast_to(ab_re, bu_re.shape)
    a_im = jnp.broadcast_to(ab_im, bu_im.shape)

    def combine(e1, e2):
        a1r, a1i, b1r, b1i = e1
        a2r, a2i, b2r, b2i = e2
        return (a2r * a1r - a2i * a1i, a2r * a1i + a2i * a1r,
                a2r * b1r - a2i * b1i + b2r, a2r * b1i + a2i * b1r + b2i)

    _, _, s_re, s_im = lax.associative_scan(combine, (a_re, a_im, bu_re, bu_im), axis=1)
    y = (jnp.einsum('blgp,ghp->blgh', s_re, c_re.astype(f32))
         - jnp.einsum('blgp,ghp->blgh', s_im, c_im.astype(f32))
         + d_skip.astype(f32) * ug)
    y = jax.nn.gelu(y.reshape(Bsz, L, SSM_WIDTH))
    y = y * jax.nn.sigmoid(y @ w_glu.astype(f32))
    return y.astype(u.dtype)


def compress_kv(kv, pos, w1, w2):
    Bsz, L, G, hd = kv.shape
    n_cmp = (L - CMP_BLOCK) // CMP_STRIDE + 1
    idx = jnp.arange(n_cmp)[:, None] * CMP_STRIDE + jnp.arange(CMP_BLOCK)[None, :]
    blocks = kv[:, idx] + pos[:, None, :]
    blocks = jnp.swapaxes(blocks, 2, 3).reshape(Bsz, n_cmp, G, CMP_BLOCK * hd)
    return jax.nn.gelu(blocks @ w1) @ w2


def nsa_attention(q, k_cmp, v_cmp, k_slc, v_slc, k_win, v_win, gates, cmp_pos, cmp_w1, cmp_w2):
    Bsz, L = q.shape[:2]
    n_cmp = (L - CMP_BLOCK) // CMP_STRIDE + 1
    n_slc = L // SLC_BLOCK
    n_sel = min(SLC_TOPK, n_slc)
    n_chunks = L // Q_CHUNK
    scale = HEAD_DIM ** -0.5
    kc = compress_kv(k_cmp, cmp_pos[0], cmp_w1[0], cmp_w2[0])
    vc = compress_kv(v_cmp, cmp_pos[1], cmp_w1[1], cmp_w2[1])
    cmp_end = jnp.arange(n_cmp) * CMP_STRIDE + CMP_BLOCK - 1
    c_start = jnp.arange(n_cmp)[:, None] * CMP_STRIDE
    s_start = jnp.arange(n_slc)[None, :] * SLC_BLOCK
    overlap = ((c_start < s_start + SLC_BLOCK) & (c_start + CMP_BLOCK > s_start)).astype(jnp.float32)
    ks_blk = jnp.moveaxis(k_slc.reshape(Bsz, n_slc, SLC_BLOCK, N_KV, HEAD_DIM), 3, 1)
    vs_blk = jnp.moveaxis(v_slc.reshape(Bsz, n_slc, SLC_BLOCK, N_KV, HEAD_DIM), 3, 1)
    kw = jnp.pad(k_win, ((0, 0), (WINDOW, 0), (0, 0), (0, 0)))
    vw = jnp.pad(v_win, ((0, 0), (WINDOW, 0), (0, 0), (0, 0)))
    bi = jnp.arange(Bsz)[:, None, None, None]
    gi = jnp.arange(N_KV)[None, :, None, None]
    blk = jnp.arange(n_slc)

    def chunk(args):
        c_idx, qc, gc = args
        t = c_idx * Q_CHUNK + jnp.arange(Q_CHUNK)
        s = jnp.einsum('bqghd,bngd->bqghn', qc, kc) * scale
        m = cmp_end[None, :] <= t[:, None]
        p_cmp = masked_softmax(s, m[None, :, None, None, :])
        o_cmp = jnp.einsum('bqghn,bngd->bqghd', p_cmp.astype(vc.dtype), vc)
        imp = jnp.einsum('bqghn,ns->bqgs', p_cmp, overlap)
        cur = t[:, None] // SLC_BLOCK
        forced = (blk[None, :] == 0) | (blk[None, :] == cur) | (blk[None, :] == cur - 1)
        future = blk[None, :] * SLC_BLOCK > t[:, None]
        imp = jnp.where(forced[None, :, None, :], FORCED_SCORE, imp)
        imp = jnp.where(future[None, :, None, :], -1.0, imp)
        _, sel = lax.top_k(imp, n_sel)
        sel = jnp.transpose(sel, (0, 2, 1, 3))
        ks = ks_blk[bi, gi, sel].reshape(Bsz, N_KV, Q_CHUNK, n_sel * SLC_BLOCK, HEAD_DIM)
        vs = vs_blk[bi, gi, sel].reshape(Bsz, N_KV, Q_CHUNK, n_sel * SLC_BLOCK, HEAD_DIM)
        kpos = (sel[..., None] * SLC_BLOCK + jnp.arange(SLC_BLOCK)).reshape(Bsz, N_KV, Q_CHUNK, n_sel * SLC_BLOCK)
        m = jnp.transpose(kpos <= t[None, None, :, None], (0, 2, 1, 3))[:, :, :, None, :]
        s = jnp.einsum('bqghd,bgqkd->bqghk', qc, ks) * scale
        p = masked_softmax(s, m)
        o_slc = jnp.einsum('bqghk,bgqkd->bqghd', p.astype(vs.dtype), vs)
        start = c_idx * Q_CHUNK
        kwc = lax.dynamic_slice_in_dim(kw, start, WINDOW + Q_CHUNK, axis=1)
        vwc = lax.dynamic_slice_in_dim(vw, start, WINDOW + Q_CHUNK, axis=1)
        wpos = start - WINDOW + jnp.arange(WINDOW + Q_CHUNK)
        d = t[:, None] - wpos[None, :]
        m = (d >= 0) & (d < WINDOW) & (wpos[None, :] >= 0)
        s = jnp.einsum('bqghd,bkgd->bqghk', qc, kwc) * scale
        p = masked_softmax(s, m[None, :, None, None, :])
        o_win = jnp.einsum('bqghk,bkgd->bqghd', p.astype(vwc.dtype), vwc)
        return (gc[:, :, 0, :, :, None] * o_cmp + gc[:, :, 1, :, :, None] * o_slc
                + gc[:, :, 2, :, :, None] * o_win)

    q_ch = jnp.moveaxis(q.reshape(Bsz, n_chunks, Q_CHUNK, N_KV, HPG, HEAD_DIM), 1, 0)
    g_ch = jnp.moveaxis(gates.reshape(Bsz, n_chunks, Q_CHUNK, 3, N_KV, HPG), 1, 0)
    o = lax.map(chunk, (jnp.arange(n_chunks), q_ch, g_ch))
    return jnp.moveaxis(o, 0, 1).reshape(Bsz, L, ATTN_WIDTH)


def token_mixer(h, w_in, ssm_lam_re, ssm_lam_im, ssm_log_dt, ssm_b_re, ssm_b_im, ssm_c_re, ssm_c_im,
                ssm_d, ssm_w_glu, ssm_w_o, nsa_cmp_pos, nsa_cmp_w1, nsa_cmp_w2, nsa_w_o, w_out):
    Bsz, L, _ = h.shape
    proj = h @ w_in
    o1 = ATTN_WIDTH
    o2 = o1 + 6 * KV_WIDTH
    o3 = o2 + 3 * N_HEADS
    o4 = o3 + SSM_WIDTH
    q, kv, nsa_g, u, merge_g = jnp.split(proj, [o1, o2, o3, o4], axis=-1)
    k_cmp, v_cmp, k_slc, v_slc, k_win, v_win = [t.reshape(Bsz, L, N_KV, HEAD_DIM) for t in jnp.split(kv, 6, axis=-1)]
    attn = nsa_attention(q.reshape(Bsz, L, N_KV, HPG, HEAD_DIM), k_cmp, v_cmp, k_slc, v_slc, k_win, v_win,
                         jax.nn.sigmoid(nsa_g).reshape(Bsz, L, 3, N_KV, HPG), nsa_cmp_pos, nsa_cmp_w1, nsa_cmp_w2)
    ssm = s5_mixer(u, ssm_lam_re, ssm_lam_im, ssm_log_dt, ssm_b_re, ssm_b_im, ssm_c_re, ssm_c_im, ssm_d, ssm_w_glu)
    g_attn, g_ssm = jnp.split(jax.nn.sigmoid(merge_g), 2, axis=-1)
    merged = g_attn * (attn @ nsa_w_o) + g_ssm * (ssm @ ssm_w_o)
    return merged @ w_out


def hier_moe(h, w_group, b_group, w_expert, b_expert, w_gate, w_up, w_down):
    Bsz, L, D = h.shape
    xt = h.reshape(-1, D)
    N = xt.shape[0]
    g_prob = jax.nn.softmax((xt @ w_group + b_group).astype(jnp.float32), axis=-1)
    g_p, g_idx = lax.top_k(g_prob, 1)
    e_logits = (xt @ w_expert + b_expert).astype(jnp.float32).reshape(N, N_GROUPS, EXP_PER_GROUP)
    e_logits = jnp.take_along_axis(e_logits, jnp.broadcast_to(g_idx[:, :, None], (N, 1, EXP_PER_GROUP)), axis=1)[:, 0]
    e_p, e_idx = lax.top_k(jax.nn.softmax(e_logits, axis=-1), TOPK_IN_GROUP)
    w = g_p * e_p / jnp.sum(e_p, axis=-1, keepdims=True)
    eid = g_idx * EXP_PER_GROUP + e_idx
    A = N * TOPK_IN_GROUP
    flat_e = eid.reshape(-1)
    flat_t = jnp.repeat(jnp.arange(N, dtype=jnp.int32), TOPK_IN_GROUP)
    flat_w = w.reshape(-1)
    order = jnp.argsort(flat_e)
    se = flat_e[order]
    counts = jnp.zeros(N_EXPERTS, jnp.int32).at[flat_e].add(1)
    starts = jnp.cumsum(counts) - counts
    pcounts = (counts + MOE_BLOCK - 1) // MOE_BLOCK * MOE_BLOCK
    pends = jnp.cumsum(pcounts)
    pstarts = pends - pcounts
    dest = pstarts[se] + jnp.arange(A) - starts[se]
    P = A + N_EXPERTS * MOE_BLOCK
    n_blk = P // MOE_BLOCK
    buf_t = jnp.zeros(P, jnp.int32).at[dest].set(flat_t[order])
    buf_w = jnp.zeros(P, jnp.float32).at[dest].set(flat_w[order])
    blk_e = jnp.minimum(jnp.searchsorted(pends, jnp.arange(n_blk) * MOE_BLOCK, side='right'), N_EXPERTS - 1)
    xs = xt[buf_t].reshape(n_blk, MOE_BLOCK, D)

    def expert_block(args):
        xb, e = args
        return (jax.nn.silu(xb @ w_gate[e]) * (xb @ w_up[e])) @ w_down[e]

    ys = lax.map(expert_block, (xs, blk_e)).reshape(P, D)
    out = jnp.zeros((N, D), jnp.float32).at[buf_t].add(ys.astype(jnp.float32) * buf_w[:, None])
    return out.astype(h.dtype).reshape(Bsz, L, D)


def setup_inputs(seed: int = 0) -> dict:
    key = jax.random.key(seed)
    ks = jax.random.split(key, 32)
    f32 = jnp.float32

    def nrm(k, shape, s):
        return jax.random.normal(k, shape, f32) * s

    n_idx = jnp.arange(SSM_STATE, dtype=f32)
    return {
        'x': nrm(ks[0], (BATCH, SEQ, D_MODEL), 1.0),
        'c': nrm(ks[1], (BATCH, D_MODEL), 1.0),
        'ada_w': nrm(ks[2], (DEPTH, D_MODEL, 6 * D_MODEL), 0.5 * D_MODEL ** -0.5),
        'ada_b': nrm(ks[3], (DEPTH, 6 * D_MODEL), 0.01),
        'norm1_g': 1.0 + nrm(ks[4], (DEPTH, D_MODEL), 0.02),
        'w_in': nrm(ks[5], (DEPTH, D_MODEL, IN_WIDTH), D_MODEL ** -0.5),
        'ssm_lam_re': -0.5 + nrm(ks[6], (DEPTH, SSM_GROUPS, SSM_STATE), 0.01),
        'ssm_lam_im': math.pi * n_idx + nrm(ks[7], (DEPTH, SSM_GROUPS, SSM_STATE), 0.01),
        'ssm_log_dt': jax.random.uniform(ks[8], (DEPTH, SSM_GROUPS), f32, math.log(1e-3), math.log(1e-1)),
        'ssm_b_re': nrm(ks[9], (DEPTH, SSM_GROUPS, SSM_STATE, SSM_GROUP), (2 * SSM_GROUP) ** -0.5),
        'ssm_b_im': nrm(ks[10], (DEPTH, SSM_GROUPS, SSM_STATE, SSM_GROUP), (2 * SSM_GROUP) ** -0.5),
        'ssm_c_re': nrm(ks[11], (DEPTH, SSM_GROUPS, SSM_GROUP, SSM_STATE), SSM_STATE ** -0.5),
        'ssm_c_im': nrm(ks[12], (DEPTH, SSM_GROUPS, SSM_GROUP, SSM_STATE), SSM_STATE ** -0.5),
        'ssm_d': nrm(ks[13], (DEPTH, SSM_GROUPS, SSM_GROUP), 1.0),
        'ssm_w_glu': nrm(ks[14], (DEPTH, SSM_WIDTH, SSM_WIDTH), SSM_WIDTH ** -0.5),
        'ssm_w_o': nrm(ks[15], (DEPTH, SSM_WIDTH, D_MODEL), SSM_WIDTH ** -0.5),
        'nsa_cmp_pos': nrm(ks[16], (DEPTH, 2, CMP_BLOCK, HEAD_DIM), 0.02),
        'nsa_cmp_w1': nrm(ks[17], (DEPTH, 2, CMP_BLOCK * HEAD_DIM, CMP_HIDDEN), (CMP_BLOCK * HEAD_DIM) ** -0.5),
        'nsa_cmp_w2': nrm(ks[18], (DEPTH, 2, CMP_HIDDEN, HEAD_DIM), CMP_HIDDEN ** -0.5),
        'nsa_w_o': nrm(ks[19], (DEPTH, ATTN_WIDTH, D_MODEL), ATTN_WIDTH ** -0.5),
        'w_out': nrm(ks[20], (DEPTH, D_MODEL, D_MODEL), D_MODEL ** -0.5),
        'norm2_g': 1.0 + nrm(ks[21], (DEPTH, D_MODEL), 0.02),
        'moe_w_group': nrm(ks[22], (DEPTH, D_MODEL, N_GROUPS), D_MODEL ** -0.5),
        'moe_b_group': nrm(ks[23], (DEPTH, N_GROUPS), 0.01),
        'moe_w_expert': nrm(ks[24], (DEPTH, D_MODEL, N_EXPERTS), D_MODEL ** -0.5),
        'moe_b_expert': nrm(ks[25], (DEPTH, N_EXPERTS), 0.01),
        'moe_w_gate': nrm(ks[26], (DEPTH, N_EXPERTS, D_MODEL, D_EXPERT), D_MODEL ** -0.5),
        'moe_w_up': nrm(ks[27], (DEPTH, N_EXPERTS, D_MODEL, D_EXPERT), D_MODEL ** -0.5),
        'moe_w_down': nrm(ks[28], (DEPTH, N_EXPERTS, D_EXPERT, D_MODEL), D_EXPERT ** -0.5),
        'final_g': 1.0 + nrm(ks[29], (D_MODEL,), 0.02),
    }


def reference(x, c, ada_w, ada_b, norm1_g, w_in, ssm_lam_re, ssm_lam_im, ssm_log_dt, ssm_b_re, ssm_b_im,
              ssm_c_re, ssm_c_im, ssm_d, ssm_w_glu, ssm_w_o, nsa_cmp_pos, nsa_cmp_w1, nsa_cmp_w2, nsa_w_o,
              w_out, norm2_g, moe_w_group, moe_b_group, moe_w_expert, moe_b_expert, moe_w_gate, moe_w_up,
              moe_w_down, final_g):
    cs = jax.nn.silu(c)
    for l in range(DEPTH):
        mod = cs @ ada_w[l] + ada_b[l]
        sh1, sc1, g1, sh2, sc2, g2 = jnp.split(mod[:, None, :], 6, axis=-1)
        h = rms_norm(x, norm1_g[l]) * (1.0 + sc1) + sh1
        x = x + g1 * token_mixer(h, w_in[l], ssm_lam_re[l], ssm_lam_im[l], ssm_log_dt[l], ssm_b_re[l],
                                 ssm_b_im[l], ssm_c_re[l], ssm_c_im[l], ssm_d[l], ssm_w_glu[l], ssm_w_o[l],
                                 nsa_cmp_pos[l], nsa_cmp_w1[l], nsa_cmp_w2[l], nsa_w_o[l], w_out[l])
        h = rms_norm(x, norm2_g[l]) * (1.0 + sc2) + sh2
        x = x + g2 * hier_moe(h, moe_w_group[l], moe_b_group[l], moe_w_expert[l], moe_b_expert[l],
                              moe_w_gate[l], moe_w_up[l], moe_w_down[l])
    return rms_norm(x, final_g)
```

```python
import functools
import math

import jax
import jax.numpy as jnp
from jax import lax
from jax.experimental import pallas as pl
from jax.experimental.pallas import tpu as pltpu

D_MODEL = 1024
DEPTH = 2
SSM_GROUP = 16
SSM_STATE = 64
SSM_WIDTH = D_MODEL // 2
SSM_GROUPS = SSM_WIDTH // SSM_GROUP
HEAD_DIM = 64
N_HEADS = D_MODEL // HEAD_DIM
N_KV = N_HEADS // 4
HPG = N_HEADS // N_KV
ATTN_WIDTH = N_HEADS * HEAD_DIM
KV_WIDTH = N_KV * HEAD_DIM
CMP_BLOCK = 32
CMP_STRIDE = 16
CMP_HIDDEN = 4 * HEAD_DIM
SLC_BLOCK = 64
SLC_TOPK = 16
WINDOW = 512
Q_CHUNK = 64
FORCED_SCORE = 1e4
N_GROUPS = 4
EXP_PER_GROUP = 8
N_EXPERTS = N_GROUPS * EXP_PER_GROUP
TOPK_IN_GROUP = 2
D_EXPERT = D_MODEL // 4
MOE_BLOCK = 128
EPS = 1e-6
NEG = -1e30

VMEM_LIMIT_BYTES = 48 * 1024 * 1024


def _mm_kernel(a_ref, b_ref, o_ref):
    a = a_ref[...].astype(jnp.bfloat16)
    b = b_ref[...].astype(jnp.bfloat16)
    o_ref[...] = jnp.dot(a, b, preferred_element_type=jnp.float32).astype(o_ref.dtype)


def pmm(a, b, tm=512, tn=512, out_dtype=jnp.float32):
    M, K = a.shape
    _, N = b.shape
    tm = min(tm, M)
    tn = min(tn, N)
    assert M % tm == 0 and N % tn == 0, (M, N, tm, tn)
    return pl.pallas_call(
        _mm_kernel,
        grid=(M // tm, N // tn),
        in_specs=[pl.BlockSpec((tm, K), lambda i, j: (i, 0)),
                  pl.BlockSpec((K, tn), lambda i, j: (0, j))],
        out_specs=pl.BlockSpec((tm, tn), lambda i, j: (i, j)),
        out_shape=jax.ShapeDtypeStruct((M, N), out_dtype),
        compiler_params=pltpu.CompilerParams(
            dimension_semantics=("parallel", "parallel"),
            vmem_limit_bytes=VMEM_LIMIT_BYTES),
    )(a, b)


def rms_norm(x, g):
    xf = x.astype(jnp.float32)
    y = xf * lax.rsqrt(jnp.mean(xf * xf, axis=-1, keepdims=True) + EPS)
    return (y * g.astype(jnp.float32)).astype(x.dtype)


def masked_softmax(s, mask):
    p = jax.nn.softmax(jnp.where(mask, s.astype(jnp.float32), NEG), axis=-1)
    return p * mask


def s5_mixer(u, lam_re, lam_im, log_dt, b_re, b_im, c_re, c_im, d_skip, w_glu):
    f32 = jnp.float32
    Bsz, L, _ = u.shape
    ug = u.astype(f32).reshape(Bsz, L, SSM_GROUPS, SSM_GROUP)
    dt = jnp.exp(log_dt.astype(f32))[:, None]
    lr, li = lam_re.astype(f32), lam_im.astype(f32)
    mag = jnp.exp(lr * dt)
    ab_re, ab_im = mag * jnp.cos(li * dt), mag * jnp.sin(li * dt)
    den = lr * lr + li * li
    nr = ab_re - 1.0
    cr = (nr * lr + ab_im * li) / den
    cim = (ab_im * lr - nr * li) / den
    br, bim = b_re.astype(f32), b_im.astype(f32)
    bb_re = cr[..., None] * br - cim[..., None] * bim
    bb_im = cr[..., None] * bim + cim[..., None] * br
    bu_re = jnp.einsum('blgh,gph->blgp', ug, bb_re)
    bu_im = jnp.einsum('blgh,gph->blgp', ug, bb_im)
    a_re = jnp.broadcast_to(ab_re, bu_re.shape)
    a_im = jnp.broadcast_to(ab_im, bu_im.shape)

    def combine(e1, e2):
        a1r, a1i, b1r, b1i = e1
        a2r, a2i, b2r, b2i = e2
        return (a2r * a1r - a2i * a1i, a2r * a1i + a2i * a1r,
                a2r * b1r - a2i * b1i + b2r, a2r * b1i + a2i * b1r + b2i)

    _, _, s_re, s_im = lax.associative_scan(combine, (a_re, a_im, bu_re, bu_im), axis=1)
    y = (jnp.einsum('blgp,ghp->blgh', s_re, c_re.astype(f32))
         - jnp.einsum('blgp,ghp->blgh', s_im, c_im.astype(f32))
         + d_skip.astype(f32) * ug)
    y = jax.nn.gelu(y.reshape(Bsz, L, SSM_WIDTH))
    y = y * jax.nn.sigmoid(pmm(y.reshape(Bsz * L, SSM_WIDTH), w_glu).reshape(Bsz, L, SSM_WIDTH))
    return y.astype(u.dtype)


def compress_kv(kv, pos, w1, w2):
    Bsz, L, G, hd = kv.shape
    n_cmp = (L - CMP_BLOCK) // CMP_STRIDE + 1
    idx = jnp.arange(n_cmp)[:, None] * CMP_STRIDE + jnp.arange(CMP_BLOCK)[None, :]
    blocks = kv[:, idx] + pos[:, None, :]
    blocks = jnp.swapaxes(blocks, 2, 3).reshape(Bsz, n_cmp, G, CMP_BLOCK * hd)
    return jax.nn.gelu(blocks @ w1) @ w2


def nsa_attention(q, k_cmp, v_cmp, k_slc, v_slc, k_win, v_win, gates, cmp_pos, cmp_w1, cmp_w2):
    Bsz, L = q.shape[:2]
    n_cmp = (L - CMP_BLOCK) // CMP_STRIDE + 1
    n_slc = L // SLC_BLOCK
    n_sel = min(SLC_TOPK, n_slc)
    n_chunks = L // Q_CHUNK
    scale = HEAD_DIM ** -0.5
    kc = compress_kv(k_cmp, cmp_pos[0], cmp_w1[0], cmp_w2[0])
    vc = compress_kv(v_cmp, cmp_pos[1], cmp_w1[1], cmp_w2[1])
    cmp_end = jnp.arange(n_cmp) * CMP_STRIDE + CMP_BLOCK - 1
    c_start = jnp.arange(n_cmp)[:, None] * CMP_STRIDE
    s_start = jnp.arange(n_slc)[None, :] * SLC_BLOCK
    overlap = ((c_start < s_start + SLC_BLOCK) & (c_start + CMP_BLOCK > s_start)).astype(jnp.float32)
    ks_blk = jnp.moveaxis(k_slc.reshape(Bsz, n_slc, SLC_BLOCK, N_KV, HEAD_DIM), 3, 1)
    vs_blk = jnp.moveaxis(v_slc.reshape(Bsz, n_slc, SLC_BLOCK, N_KV, HEAD_DIM), 3, 1)
    kw = jnp.pad(k_win, ((0, 0), (WINDOW, 0), (0, 0), (0, 0)))
    vw = jnp.pad(v_win, ((0, 0), (WINDOW, 0), (0, 0), (0, 0)))
    bi = jnp.arange(Bsz)[:, None, None, None]
    gi = jnp.arange(N_KV)[None, :, None, None]
    blk = jnp.arange(n_slc)

    def chunk(args):
        c_idx, qc, gc = args
        t = c_idx * Q_CHUNK + jnp.arange(Q_CHUNK)
        s = jnp.einsum('bqghd,bngd->bqghn', qc, kc) * scale
        m = cmp_end[None, :] <= t[:, None]
        p_cmp = masked_softmax(s, m[None, :, None, None, :])
        o_cmp = jnp.einsum('bqghn,bngd->bqghd', p_cmp.astype(vc.dtype), vc)
        imp = jnp.einsum('bqghn,ns->bqgs', p_cmp, overlap)
        cur = t[:, None] // SLC_BLOCK
        forced = (blk[None, :] == 0) | (blk[None, :] == cur) | (blk[None, :] == cur - 1)
        future = blk[None, :] * SLC_BLOCK > t[:, None]
        imp = jnp.where(forced[None, :, None, :], FORCED_SCORE, imp)
        imp = jnp.where(future[None, :, None, :], -1.0, imp)
        _, sel = lax.top_k(imp, n_sel)
        sel = jnp.transpose(sel, (0, 2, 1, 3))
        ks = ks_blk[bi, gi, sel].reshape(Bsz, N_KV, Q_CHUNK, n_sel * SLC_BLOCK, HEAD_DIM)
        vs = vs_blk[bi, gi, sel].reshape(Bsz, N_KV, Q_CHUNK, n_sel * SLC_BLOCK, HEAD_DIM)
        kpos = (sel[..., None] * SLC_BLOCK + jnp.arange(SLC_BLOCK)).reshape(Bsz, N_KV, Q_CHUNK, n_sel * SLC_BLOCK)
        m = jnp.transpose(kpos <= t[None, None, :, None], (0, 2, 1, 3))[:, :, :, None, :]
        s = jnp.einsum('bqghd,bgqkd->bqghk', qc, ks) * scale
        p = masked_softmax(s, m)
        o_slc = jnp.einsum('bqghk,bgqkd->bqghd', p.astype(vs.dtype), vs)
        start = c_idx * Q_CHUNK
        kwc = lax.dynamic_slice_in_dim(kw, start, WINDOW + Q_CHUNK, axis=1)
        vwc = lax.dynamic_slice_in_dim(vw, start, WINDOW + Q_CHUNK, axis=1)
        wpos = start - WINDOW + jnp.arange(WINDOW + Q_CHUNK)
        d = t[:, None] - wpos[None, :]
        m = (d >= 0) & (d < WINDOW) & (wpos[None, :] >= 0)
        s = jnp.einsum('bqghd,bkgd->bqghk', qc, kwc) * scale
        p = masked_softmax(s, m[None, :, None, None, :])
        o_win = jnp.einsum('bqghk,bkgd->bqghd', p.astype(vwc.dtype), vwc)
        return (gc[:, :, 0, :, :, None] * o_cmp + gc[:, :, 1, :, :, None] * o_slc
                + gc[:, :, 2, :, :, None] * o_win)

    q_ch = jnp.moveaxis(q.reshape(Bsz, n_chunks, Q_CHUNK, N_KV, HPG, HEAD_DIM), 1, 0)
    g_ch = jnp.moveaxis(gates.reshape(Bsz, n_chunks, Q_CHUNK, 3, N_KV, HPG), 1, 0)
    o = lax.map(chunk, (jnp.arange(n_chunks), q_ch, g_ch))
    return jnp.moveaxis(o, 0, 1).reshape(Bsz, L, ATTN_WIDTH)


def token_mixer(h, w_in, ssm_lam_re, ssm_lam_im, ssm_log_dt, ssm_b_re, ssm_b_im, ssm_c_re, ssm_c_im,
                ssm_d, ssm_w_glu, ssm_w_o, nsa_cmp_pos, nsa_cmp_w1, nsa_cmp_w2, nsa_w_o, w_out):
    Bsz, L, D = h.shape
    N = Bsz * L
    o1 = ATTN_WIDTH
    o2 = o1 + 6 * KV_WIDTH
    o3 = o2 + 3 * N_HEADS
    o4 = o3 + SSM_WIDTH
    h2 = h.reshape(N, D)
    q = pmm(h2, w_in[:, :o1]).reshape(Bsz, L, -1)
    kv = pmm(h2, w_in[:, o1:o2]).reshape(Bsz, L, -1)
    wg = jnp.pad(w_in[:, o2:o3], ((0, 0), (0, 128 - 3 * N_HEADS)))
    nsa_g = pmm(h2, wg)[:, :3 * N_HEADS].reshape(Bsz, L, -1)
    u = pmm(h2, w_in[:, o3:o4]).reshape(Bsz, L, -1)
    merge_g = pmm(h2, w_in[:, o4:]).reshape(Bsz, L, -1)
    k_cmp, v_cmp, k_slc, v_slc, k_win, v_win = [t.reshape(Bsz, L, N_KV, HEAD_DIM) for t in jnp.split(kv, 6, axis=-1)]
    attn = nsa_attention(q.reshape(Bsz, L, N_KV, HPG, HEAD_DIM), k_cmp, v_cmp, k_slc, v_slc, k_win, v_win,
                         jax.nn.sigmoid(nsa_g).reshape(Bsz, L, 3, N_KV, HPG), nsa_cmp_pos, nsa_cmp_w1, nsa_cmp_w2)
    ssm = s5_mixer(u, ssm_lam_re, ssm_lam_im, ssm_log_dt, ssm_b_re, ssm_b_im, ssm_c_re, ssm_c_im, ssm_d, ssm_w_glu)
    g_attn, g_ssm = jnp.split(jax.nn.sigmoid(merge_g), 2, axis=-1)
    merged = (g_attn * pmm(attn.reshape(N, -1), nsa_w_o).reshape(Bsz, L, D)
              + g_ssm * pmm(ssm.reshape(N, -1), ssm_w_o).reshape(Bsz, L, D))
    return pmm(merged.reshape(N, D), w_out).reshape(Bsz, L, D)


def hier_moe(h, w_group, b_group, w_expert, b_expert, w_gate, w_up, w_down):
    Bsz, L, D = h.shape
    xt = h.reshape(-1, D)
    N = xt.shape[0]
    g_prob = jax.nn.softmax((xt @ w_group + b_group).astype(jnp.float32), axis=-1)
    g_p, g_idx = lax.top_k(g_prob, 1)
    e_logits = (xt @ w_expert + b_expert).astype(jnp.float32).reshape(N, N_GROUPS, EXP_PER_GROUP)
    e_logits = jnp.take_along_axis(e_logits, jnp.broadcast_to(g_idx[:, :, None], (N, 1, EXP_PER_GROUP)), axis=1)[:, 0]
    e_p, e_idx = lax.top_k(jax.nn.softmax(e_logits, axis=-1), TOPK_IN_GROUP)
    w = g_p * e_p / jnp.sum(e_p, axis=-1, keepdims=True)
    eid = g_idx * EXP_PER_GROUP + e_idx
    A = N * TOPK_IN_GROUP
    flat_e = eid.reshape(-1)
    flat_t = jnp.repeat(jnp.arange(N, dtype=jnp.int32), TOPK_IN_GROUP)
    flat_w = w.reshape(-1)
    order = jnp.argsort(flat_e)
    se = flat_e[order]
    counts = jnp.zeros(N_EXPERTS, jnp.int32).at[flat_e].add(1)
    starts = jnp.cumsum(counts) - counts
    pcounts = (counts + MOE_BLOCK - 1) // MOE_BLOCK * MOE_BLOCK
    pends = jnp.cumsum(pcounts)
    pstarts = pends - pcounts
    dest = pstarts[se] + jnp.arange(A) - starts[se]
    P = A + N_EXPERTS * MOE_BLOCK
    n_blk = P // MOE_BLOCK
    buf_t = jnp.zeros(P, jnp.int32).at[dest].set(flat_t[order])
    buf_w = jnp.zeros(P, jnp.float32).at[dest].set(flat_w[order])
    blk_e = jnp.minimum(jnp.searchsorted(pends, jnp.arange(n_blk) * MOE_BLOCK, side='right'), N_EXPERTS - 1)
    xs = xt[buf_t].reshape(n_blk, MOE_BLOCK, D)

    def expert_block(args):
        xb, e = args
        return (jax.nn.silu(xb @ w_gate[e]) * (xb @ w_up[e])) @ w_down[e]

    ys = lax.map(expert_block, (xs, blk_e)).reshape(P, D)
    out = jnp.zeros((N, D), jnp.float32).at[buf_t].add(ys.astype(jnp.float32) * buf_w[:, None])
    return out.astype(h.dtype).reshape(Bsz, L, D)


def kernel(x, c, ada_w, ada_b, norm1_g, w_in, ssm_lam_re, ssm_lam_im, ssm_log_dt, ssm_b_re, ssm_b_im, ssm_c_re, ssm_c_im, ssm_d, ssm_w_glu, ssm_w_o, nsa_cmp_pos, nsa_cmp_w1, nsa_cmp_w2, nsa_w_o, w_out, norm2_g, moe_w_group, moe_b_group, moe_w_expert, moe_b_expert, moe_w_gate, moe_w_up, moe_w_down, final_g):
    cs = jax.nn.silu(c)
    for l in range(DEPTH):
        mod = cs @ ada_w[l] + ada_b[l]
        sh1, sc1, g1, sh2, sc2, g2 = jnp.split(mod[:, None, :], 6, axis=-1)
        h = rms_norm(x, norm1_g[l]) * (1.0 + sc1) + sh1
        x = x + g1 * token_mixer(h, w_in[l], ssm_lam_re[l], ssm_lam_im[l], ssm_log_dt[l], ssm_b_re[l],
                                 ssm_b_im[l], ssm_c_re[l], ssm_c_im[l], ssm_d[l], ssm_w_glu[l], ssm_w_o[l],
                                 nsa_cmp_pos[l], nsa_cmp_w1[l], nsa_cmp_w2[l], nsa_w_o[l], w_out[l])
        h = rms_norm(x, norm2_g[l]) * (1.0 + sc2) + sh2
        x = x + g2 * hier_moe(h, moe_w_group[l], moe_b_group[l], moe_w_expert[l], moe_b_expert[l],
                              moe_w_gate[l], moe_w_up[l], moe_w_down[l])
    return rms_norm(x, final_g)
```

```python
import functools
import math

import jax
import jax.numpy as jnp
from jax import lax
from jax.experimental import pallas as pl
from jax.experimental.pallas import tpu as pltpu

D_MODEL = 1024
DEPTH = 2
SSM_GROUP = 16
SSM_STATE = 64
SSM_WIDTH = D_MODEL // 2
SSM_GROUPS = SSM_WIDTH // SSM_GROUP
HEAD_DIM = 64
N_HEADS = D_MODEL // HEAD_DIM
N_KV = N_HEADS // 4
HPG = N_HEADS // N_KV
ATTN_WIDTH = N_HEADS * HEAD_DIM
KV_WIDTH = N_KV * HEAD_DIM
CMP_BLOCK = 32
CMP_STRIDE = 16
CMP_HIDDEN = 4 * HEAD_DIM
SLC_BLOCK = 64
SLC_TOPK = 16
WINDOW = 512
Q_CHUNK = 64
FORCED_SCORE = 1e4
N_GROUPS = 4
EXP_PER_GROUP = 8
N_EXPERTS = N_GROUPS * EXP_PER_GROUP
TOPK_IN_GROUP = 2
D_EXPERT = D_MODEL // 4
MOE_BLOCK = 128
EPS = 1e-6
NEG = -1e30

VMEM_LIMIT_BYTES = 48 * 1024 * 1024


def _mm_kernel(a_ref, b_ref, o_ref):
    a = a_ref[...].astype(jnp.bfloat16)
    b = b_ref[...].astype(jnp.bfloat16)
    o_ref[...] = jnp.dot(a, b, preferred_element_type=jnp.float32).astype(o_ref.dtype)


def pmm(a, b, tm=512, tn=512, out_dtype=jnp.float32, interpret=False):
    M, K = a.shape
    _, N = b.shape
    tm = min(tm, M)
    tn = min(tn, N)
    assert M % tm == 0 and N % tn == 0, (M, N, tm, tn)
    return pl.pallas_call(
        _mm_kernel,
        interpret=interpret,
        grid=(M // tm, N // tn),
        in_specs=[pl.BlockSpec((tm, K), lambda i, j: (i, 0)),
                  pl.BlockSpec((K, tn), lambda i, j: (0, j))],
        out_specs=pl.BlockSpec((tm, tn), lambda i, j: (i, j)),
        out_shape=jax.ShapeDtypeStruct((M, N), out_dtype),
        compiler_params=pltpu.CompilerParams(
            dimension_semantics=("parallel", "parallel"),
            vmem_limit_bytes=VMEM_LIMIT_BYTES),
    )(a, b)


MASK_FILL = -2e30
MAX_FLOOR = -1e30
KEY_STEP = 256
LANES = 128


def _online_step(carry, st, vT, vis):
    m, l, acc = carry
    st = jnp.where(vis, st, MASK_FILL)
    m_new = jnp.maximum(m, jnp.max(st, axis=0, keepdims=True))
    alpha = jnp.exp(m - m_new)
    p = jnp.exp(st - m_new)
    l = alpha * l + jnp.sum(p, axis=0, keepdims=True)
    acc = alpha * acc + jnp.dot(vT, p.astype(jnp.bfloat16), preferred_element_type=jnp.float32)
    return m_new, l, acc


def _nsa_kernel(qT_ref, kc_ref, vcT_ref, ks_ref, vsT_ref, kw_ref, vwT_ref, g_ref, ovT_ref, o_ref, sel_ref,
                *, n_cmp, n_slc, n_sel):
    f32 = jnp.float32
    bf16 = jnp.bfloat16
    c = pl.program_id(2)
    q = qT_ref[0, 0, 0]
    ncols = q.shape[1]
    ncp = kc_ref.shape[2]
    lane = lax.broadcasted_iota(jnp.int32, (1, ncols), 1)
    tpos = c * Q_CHUNK + (lane & (Q_CHUNK - 1))

    s = jnp.dot(kc_ref[0, 0], q, preferred_element_type=f32)
    n_idx = lax.broadcasted_iota(jnp.int32, (ncp, 1), 0)
    vis = (n_idx * CMP_STRIDE + (CMP_BLOCK - 1) <= tpos) & (n_idx < n_cmp)
    s = jnp.where(vis, s, MASK_FILL)
    m = jnp.max(s, axis=0, keepdims=True)
    p = jnp.where(vis, jnp.exp(s - m), 0.0)
    l = jnp.sum(p, axis=0, keepdims=True)
    p = p / jnp.where(l > 0.0, l, 1.0)
    p_hi = p.astype(bf16)
    o_cmp = jnp.dot(vcT_ref[0, 0], p_hi, preferred_element_type=f32)

    p_lo = (p - p_hi.astype(f32)).astype(bf16)
    ov = ovT_ref[...]
    imp4 = (jnp.dot(ov, p_hi, preferred_element_type=f32)
            + jnp.dot(ov, p_lo, preferred_element_type=f32))
    z = imp4[:, :LANES] + imp4[:, LANES:]
    imp = z + pltpu.roll(z, Q_CHUNK, axis=1)

    jidx = lax.broadcasted_iota(jnp.int32, (n_slc, LANES), 0)
    forced = (jidx == 0) | (jidx == c) | (jidx == c - 1)
    work = jnp.where(forced, FORCED_SCORE, imp)
    work = jnp.where(jidx > c, -1.0, work)

    def pick(_, carry):
        work, sel = carry
        mx = jnp.max(work, axis=0, keepdims=True)
        first = jnp.min(jnp.where(work == mx, jidx, n_slc), axis=0, keepdims=True)
        hit = jidx == first
        return jnp.where(hit, -2.0, work), jnp.where(hit, 1.0, sel)

    _, sel = lax.fori_loop(0, n_sel, pick, (work, jnp.zeros((n_slc, LANES), f32)))
    sel_ref[...] = jnp.concatenate([sel, sel], axis=1)

    init = (jnp.full((1, ncols), MAX_FLOOR, f32), jnp.zeros((1, ncols), f32),
            jnp.zeros((HEAD_DIM, ncols), f32))
    blocks_per_step = KEY_STEP // SLC_BLOCK

    def slc_step(s_i, carry, causal):
        off = pl.multiple_of(s_i * KEY_STEP, KEY_STEP)
        st = jnp.dot(ks_ref[0, 0, pl.ds(off, KEY_STEP), :], q, preferred_element_type=f32)
        vis = jnp.concatenate(
            [jnp.broadcast_to(sel_ref[pl.ds(s_i * blocks_per_step + i, 1), :] > 0.5, (SLC_BLOCK, ncols))
             for i in range(blocks_per_step)], axis=0)
        if causal:
            kpos = off + lax.broadcasted_iota(jnp.int32, (KEY_STEP, 1), 0)
            vis = vis & (kpos <= tpos)
        return _online_step(carry, st, vsT_ref[0, 0, :, pl.ds(off, KEY_STEP)], vis)

    n_full = c // blocks_per_step
    carry = lax.fori_loop(0, n_full, lambda i, cr: slc_step(i, cr, False), init)
    _, l_s, acc_s = slc_step(n_full, carry, True)
    o_slc = acc_s / l_s

    base = (c // 2) * LANES
    carry = init
    for o, sz in ((0, 256), (256, 256), (512, 128)):
        off = pl.multiple_of(base + o, LANES)
        st = jnp.dot(kw_ref[0, 0, pl.ds(off, sz), :], q, preferred_element_type=f32)
        kpos = off - WINDOW + lax.broadcasted_iota(jnp.int32, (sz, 1), 0)
        d = tpos - kpos
        vis = (d >= 0) & (d < WINDOW) & (kpos >= 0)
        carry = _online_step(carry, st, vwT_ref[0, 0, :, pl.ds(off, sz)], vis)
    _, l_w, acc_w = carry
    o_win = acc_w / l_w

    g = g_ref[0, 0, 0]
    o = g[0:1] * o_cmp + g[1:2] * o_slc + g[2:3] * o_win
    o_ref[0, 0, 0] = o.astype(o_ref.dtype)


def nsa_attention_pallas(q2, kvs, gates2, cmp_pos, cmp_w1, cmp_w2, B, L, interpret=False):
    f32, bf16 = jnp.float32, jnp.bfloat16
    G, H, hd = N_KV, HPG, HEAD_DIM
    C = L // Q_CHUNK
    n16 = L // CMP_STRIDE
    n_cmp = (L - CMP_BLOCK) // CMP_STRIDE + 1
    ncp = -(-n_cmp // LANES) * LANES
    n_slc = L // SLC_BLOCK
    n_sel = min(SLC_TOPK, n_slc)
    k_cmp, v_cmp, k_slc, v_slc, k_win, v_win = kvs

    def compress(kv2, pos, w1, w2):
        x = kv2.reshape(B, n16, CMP_STRIDE, G, hd).transpose(0, 3, 1, 2, 4).reshape(B * G * n16, CMP_STRIDE * hd)
        half = CMP_STRIDE * hd
        wcat = jnp.concatenate([w1[:half], w1[half:]], axis=1)
        ab = pmm(x, wcat, interpret=interpret).reshape(B, G, n16, 2 * CMP_HIDDEN)
        posb = pmm(jnp.pad(pos.reshape(1, -1), ((0, 7), (0, 0))), w1, interpret=interpret)[0]
        hid = ab[:, :, :-1, :CMP_HIDDEN] + ab[:, :, 1:, CMP_HIDDEN:] + posb
        hid = jnp.pad(jax.nn.gelu(hid), ((0, 0), (0, 0), (0, ncp - n_cmp), (0, 0)))
        out = pmm(hid.reshape(B * G * ncp, CMP_HIDDEN), w2, interpret=interpret)
        return out.reshape(B, G, ncp, hd)

    kc = compress(k_cmp, cmp_pos[0], cmp_w1[0], cmp_w2[0]).astype(bf16)
    vcT = compress(v_cmp, cmp_pos[1], cmp_w1[1], cmp_w2[1]).astype(bf16).transpose(0, 1, 3, 2)

    def key_major(a):
        return a.astype(bf16).reshape(B, L, G, hd).transpose(0, 2, 1, 3)

    def dim_major(a):
        return a.astype(bf16).reshape(B, L, G, hd).transpose(0, 2, 3, 1)

    ks = key_major(k_slc)
    vsT = dim_major(v_slc)
    kw = jnp.pad(key_major(k_win), ((0, 0), (0, 0), (WINDOW, 0), (0, 0)))
    vwT = jnp.pad(dim_major(v_win), ((0, 0), (0, 0), (0, 0), (WINDOW, 0)))
    scale = HEAD_DIM ** -0.5
    qT = (q2.astype(f32) * scale).astype(bf16).reshape(B, C, Q_CHUNK, G, H, hd)
    qT = qT.transpose(0, 3, 1, 5, 4, 2).reshape(B, G, C, hd, H * Q_CHUNK)
    gT = gates2.astype(f32).reshape(B, C, Q_CHUNK, 3, G, H).transpose(0, 4, 1, 3, 5, 2).reshape(B, G, C, 3, H * Q_CHUNK)
    c_start = jnp.arange(ncp)[None, :] * CMP_STRIDE
    s_start = jnp.arange(n_slc)[:, None] * SLC_BLOCK
    ovT = ((c_start < s_start + SLC_BLOCK) & (c_start + CMP_BLOCK > s_start)
           & (jnp.arange(ncp)[None, :] < n_cmp)).astype(bf16)

    ncols = H * Q_CHUNK
    bg = lambda b, g, c: (b, g, 0, 0)
    bgc = lambda b, g, c: (b, g, c, 0, 0)
    oT = pl.pallas_call(
        functools.partial(_nsa_kernel, n_cmp=n_cmp, n_slc=n_slc, n_sel=n_sel),
        grid=(B, G, C),
        in_specs=[pl.BlockSpec((1, 1, 1, hd, ncols), bgc),
                  pl.BlockSpec((1, 1, ncp, hd), bg),
                  pl.BlockSpec((1, 1, hd, ncp), bg),
                  pl.BlockSpec((1, 1, L, hd), bg),
                  pl.BlockSpec((1, 1, hd, L), bg),
                  pl.BlockSpec((1, 1, L + WINDOW, hd), bg),
                  pl.BlockSpec((1, 1, hd, L + WINDOW), bg),
                  pl.BlockSpec((1, 1, 1, 3, ncols), bgc),
                  pl.BlockSpec((n_slc, ncp), lambda b, g, c: (0, 0))],
        out_specs=pl.BlockSpec((1, 1, 1, hd, ncols), bgc),
        out_shape=jax.ShapeDtypeStruct((B, G, C, hd, ncols), bf16),
        scratch_shapes=[pltpu.VMEM((n_slc, ncols), f32)],
        compiler_params=pltpu.CompilerParams(
            dimension_semantics=("parallel", "parallel", "arbitrary"),
            vmem_limit_bytes=VMEM_LIMIT_BYTES),
        interpret=interpret,
        name="nsa_attention",
    )(qT, kc, vcT, ks, vsT, kw, vwT, gT, ovT)
    o = oT.reshape(B, G, C, hd, H, Q_CHUNK).transpose(0, 2, 5, 1, 4, 3)
    return o.reshape(B * L, G * H * hd)


def rms_norm(x, g):
    xf = x.astype(jnp.float32)
    y = xf * lax.rsqrt(jnp.mean(xf * xf, axis=-1, keepdims=True) + EPS)
    return (y * g.astype(jnp.float32)).astype(x.dtype)


def masked_softmax(s, mask):
    p = jax.nn.softmax(jnp.where(mask, s.astype(jnp.float32), NEG), axis=-1)
    return p * mask


def s5_mixer(u, lam_re, lam_im, log_dt, b_re, b_im, c_re, c_im, d_skip, w_glu):
    f32 = jnp.float32
    Bsz, L, _ = u.shape
    ug = u.astype(f32).reshape(Bsz, L, SSM_GROUPS, SSM_GROUP)
    dt = jnp.exp(log_dt.astype(f32))[:, None]
    lr, li = lam_re.astype(f32), lam_im.astype(f32)
    mag = jnp.exp(lr * dt)
    ab_re, ab_im = mag * jnp.cos(li * dt), mag * jnp.sin(li * dt)
    den = lr * lr + li * li
    nr = ab_re - 1.0
    cr = (nr * lr + ab_im * li) / den
    cim = (ab_im * lr - nr * li) / den
    br, bim = b_re.astype(f32), b_im.astype(f32)
    bb_re = cr[..., None] * br - cim[..., None] * bim
    bb_im = cr[..., None] * bim + cim[..., None] * br
    bu_re = jnp.einsum('blgh,gph->blgp', ug, bb_re)
    bu_im = jnp.einsum('blgh,gph->blgp', ug, bb_im)
    a_re = jnp.broadcast_to(ab_re, bu_re.shape)
    a_im = jnp.broadcast_to(ab_im, bu_im.shape)

    def combine(e1, e2):
        a1r, a1i, b1r, b1i = e1
        a2r, a2i, b2r, b2i = e2
        return (a2r * a1r - a2i * a1i, a2r * a1i + a2i * a1r,
                a2r * b1r - a2i * b1i + b2r, a2r * b1i + a2i * b1r + b2i)

    _, _, s_re, s_im = lax.associative_scan(combine, (a_re, a_im, bu_re, bu_im), axis=1)
    y = (jnp.einsum('blgp,ghp->blgh', s_re, c_re.astype(f32))
         - jnp.einsum('blgp,ghp->blgh', s_im, c_im.astype(f32))
         + d_skip.astype(f32) * ug)
    y = jax.nn.gelu(y.reshape(Bsz, L, SSM_WIDTH))
    y = y * jax.nn.sigmoid(pmm(y.reshape(Bsz * L, SSM_WIDTH), w_glu).reshape(Bsz, L, SSM_WIDTH))
    return y.astype(u.dtype)


def compress_kv(kv, pos, w1, w2):
    Bsz, L, G, hd = kv.shape
    n_cmp = (L - CMP_BLOCK) // CMP_STRIDE + 1
    idx = jnp.arange(n_cmp)[:, None] * CMP_STRIDE + jnp.arange(CMP_BLOCK)[None, :]
    blocks = kv[:, idx] + pos[:, None, :]
    blocks = jnp.swapaxes(blocks, 2, 3).reshape(Bsz, n_cmp, G, CMP_BLOCK * hd)
    return jax.nn.gelu(blocks @ w1) @ w2


def nsa_attention(q, k_cmp, v_cmp, k_slc, v_slc, k_win, v_win, gates, cmp_pos, cmp_w1, cmp_w2):
    Bsz, L = q.shape[:2]
    n_cmp = (L - CMP_BLOCK) // CMP_STRIDE + 1
    n_slc = L // SLC_BLOCK
    n_sel = min(SLC_TOPK, n_slc)
    n_chunks = L // Q_CHUNK
    scale = HEAD_DIM ** -0.5
    kc = compress_kv(k_cmp, cmp_pos[0], cmp_w1[0], cmp_w2[0])
    vc = compress_kv(v_cmp, cmp_pos[1], cmp_w1[1], cmp_w2[1])
    cmp_end = jnp.arange(n_cmp) * CMP_STRIDE + CMP_BLOCK - 1
    c_start = jnp.arange(n_cmp)[:, None] * CMP_STRIDE
    s_start = jnp.arange(n_slc)[None, :] * SLC_BLOCK
    overlap = ((c_start < s_start + SLC_BLOCK) & (c_start + CMP_BLOCK > s_start)).astype(jnp.float32)
    ks_blk = jnp.moveaxis(k_slc.reshape(Bsz, n_slc, SLC_BLOCK, N_KV, HEAD_DIM), 3, 1)
    vs_blk = jnp.moveaxis(v_slc.reshape(Bsz, n_slc, SLC_BLOCK, N_KV, HEAD_DIM), 3, 1)
    kw = jnp.pad(k_win, ((0, 0), (WINDOW, 0), (0, 0), (0, 0)))
    vw = jnp.pad(v_win, ((0, 0), (WINDOW, 0), (0, 0), (0, 0)))
    bi = jnp.arange(Bsz)[:, None, None, None]
    gi = jnp.arange(N_KV)[None, :, None, None]
    blk = jnp.arange(n_slc)

    def chunk(args):
        c_idx, qc, gc = args
        t = c_idx * Q_CHUNK + jnp.arange(Q_CHUNK)
        s = jnp.einsum('bqghd,bngd->bqghn', qc, kc) * scale
        m = cmp_end[None, :] <= t[:, None]
        p_cmp = masked_softmax(s, m[None, :, None, None, :])
        o_cmp = jnp.einsum('bqghn,bngd->bqghd', p_cmp.astype(vc.dtype), vc)
        imp = jnp.einsum('bqghn,ns->bqgs', p_cmp, overlap)
        cur = t[:, None] // SLC_BLOCK
        forced = (blk[None, :] == 0) | (blk[None, :] == cur) | (blk[None, :] == cur - 1)
        future = blk[None, :] * SLC_BLOCK > t[:, None]
        imp = jnp.where(forced[None, :, None, :], FORCED_SCORE, imp)
        imp = jnp.where(future[None, :, None, :], -1.0, imp)
        _, sel = lax.top_k(imp, n_sel)
        sel = jnp.transpose(sel, (0, 2, 1, 3))
        ks = ks_blk[bi, gi, sel].reshape(Bsz, N_KV, Q_CHUNK, n_sel * SLC_BLOCK, HEAD_DIM)
        vs = vs_blk[bi, gi, sel].reshape(Bsz, N_KV, Q_CHUNK, n_sel * SLC_BLOCK, HEAD_DIM)
        kpos = (sel[..., None] * SLC_BLOCK + jnp.arange(SLC_BLOCK)).reshape(Bsz, N_KV, Q_CHUNK, n_sel * SLC_BLOCK)
        m = jnp.transpose(kpos <= t[None, None, :, None], (0, 2, 1, 3))[:, :, :, None, :]
        s = jnp.einsum('bqghd,bgqkd->bqghk', qc, ks) * scale
        p = masked_softmax(s, m)
        o_slc = jnp.einsum('bqghk,bgqkd->bqghd', p.astype(vs.dtype), vs)
        start = c_idx * Q_CHUNK
        kwc = lax.dynamic_slice_in_dim(kw, start, WINDOW + Q_CHUNK, axis=1)
        vwc = lax.dynamic_slice_in_dim(vw, start, WINDOW + Q_CHUNK, axis=1)
        wpos = start - WINDOW + jnp.arange(WINDOW + Q_CHUNK)
        d = t[:, None] - wpos[None, :]
        m = (d >= 0) & (d < WINDOW) & (wpos[None, :] >= 0)
        s = jnp.einsum('bqghd,bkgd->bqghk', qc, kwc) * scale
        p = masked_softmax(s, m[None, :, None, None, :])
        o_win = jnp.einsum('bqghk,bkgd->bqghd', p.astype(vwc.dtype), vwc)
        return (gc[:, :, 0, :, :, None] * o_cmp + gc[:, :, 1, :, :, None] * o_slc
                + gc[:, :, 2, :, :, None] * o_win)

    q_ch = jnp.moveaxis(q.reshape(Bsz, n_chunks, Q_CHUNK, N_KV, HPG, HEAD_DIM), 1, 0)
    g_ch = jnp.moveaxis(gates.reshape(Bsz, n_chunks, Q_CHUNK, 3, N_KV, HPG), 1, 0)
    o = lax.map(chunk, (jnp.arange(n_chunks), q_ch, g_ch))
    return jnp.moveaxis(o, 0, 1).reshape(Bsz, L, ATTN_WIDTH)


def token_mixer(h, w_in, ssm_lam_re, ssm_lam_im, ssm_log_dt, ssm_b_re, ssm_b_im, ssm_c_re, ssm_c_im,
                ssm_d, ssm_w_glu, ssm_w_o, nsa_cmp_pos, nsa_cmp_w1, nsa_cmp_w2, nsa_w_o, w_out):
    Bsz, L, D = h.shape
    N = Bsz * L
    o1 = ATTN_WIDTH
    o2 = o1 + 6 * KV_WIDTH
    o3 = o2 + 3 * N_HEADS
    o4 = o3 + SSM_WIDTH
    h2 = h.reshape(N, D)
    q = pmm(h2, w_in[:, :o1]).reshape(Bsz, L, -1)
    kv = pmm(h2, w_in[:, o1:o2]).reshape(Bsz, L, -1)
    wg = jnp.pad(w_in[:, o2:o3], ((0, 0), (0, 128 - 3 * N_HEADS)))
    nsa_g = pmm(h2, wg)[:, :3 * N_HEADS].reshape(Bsz, L, -1)
    u = pmm(h2, w_in[:, o3:o4]).reshape(Bsz, L, -1)
    merge_g = pmm(h2, w_in[:, o4:]).reshape(Bsz, L, -1)
    kvs = [t.reshape(N, KV_WIDTH) for t in jnp.split(kv, 6, axis=-1)]
    attn = nsa_attention_pallas(q.reshape(N, ATTN_WIDTH), kvs, jax.nn.sigmoid(nsa_g).reshape(N, 3 * N_HEADS),
                                nsa_cmp_pos, nsa_cmp_w1, nsa_cmp_w2, Bsz, L)
    ssm = s5_mixer(u, ssm_lam_re, ssm_lam_im, ssm_log_dt, ssm_b_re, ssm_b_im, ssm_c_re, ssm_c_im, ssm_d, ssm_w_glu)
    g_attn, g_ssm = jnp.split(jax.nn.sigmoid(merge_g), 2, axis=-1)
    merged = (g_attn * pmm(attn.reshape(N, -1), nsa_w_o).reshape(Bsz, L, D)
              + g_ssm * pmm(ssm.reshape(N, -1), ssm_w_o).reshape(Bsz, L, D))
    return pmm(merged.reshape(N, D), w_out).reshape(Bsz, L, D)


def hier_moe(h, w_group, b_group, w_expert, b_expert, w_gate, w_up, w_down):
    Bsz, L, D = h.shape
    xt = h.reshape(-1, D)
    N = xt.shape[0]
    g_prob = jax.nn.softmax((xt @ w_group + b_group).astype(jnp.float32), axis=-1)
    g_p, g_idx = lax.top_k(g_prob, 1)
    e_logits = (xt @ w_expert + b_expert).astype(jnp.float32).reshape(N, N_GROUPS, EXP_PER_GROUP)
    e_logits = jnp.take_along_axis(e_logits, jnp.broadcast_to(g_idx[:, :, None], (N, 1, EXP_PER_GROUP)), axis=1)[:, 0]
    e_p, e_idx = lax.top_k(jax.nn.softmax(e_logits, axis=-1), TOPK_IN_GROUP)
    w = g_p * e_p / jnp.sum(e_p, axis=-1, keepdims=True)
    eid = g_idx * EXP_PER_GROUP + e_idx
    A = N * TOPK_IN_GROUP
    flat_e = eid.reshape(-1)
    flat_t = jnp.repeat(jnp.arange(N, dtype=jnp.int32), TOPK_IN_GROUP)
    flat_w = w.reshape(-1)
    order = jnp.argsort(flat_e)
    se = flat_e[order]
    counts = jnp.zeros(N_EXPERTS, jnp.int32).at[flat_e].add(1)
    starts = jnp.cumsum(counts) - counts
    pcounts = (counts + MOE_BLOCK - 1) // MOE_BLOCK * MOE_BLOCK
    pends = jnp.cumsum(pcounts)
    pstarts = pends - pcounts
    dest = pstarts[se] + jnp.arange(A) - starts[se]
    P = A + N_EXPERTS * MOE_BLOCK
    n_blk = P // MOE_BLOCK
    buf_t = jnp.zeros(P, jnp.int32).at[dest].set(flat_t[order])
    buf_w = jnp.zeros(P, jnp.float32).at[dest].set(flat_w[order])
    blk_e = jnp.minimum(jnp.searchsorted(pends, jnp.arange(n_blk) * MOE_BLOCK, side='right'), N_EXPERTS - 1)
    xs = xt[buf_t].reshape(n_blk, MOE_BLOCK, D)

    def expert_block(args):
        xb, e = args
        return (jax.nn.silu(xb @ w_gate[e]) * (xb @ w_up[e])) @ w_down[e]

    ys = lax.map(expert_block, (xs, blk_e)).reshape(P, D)
    out = jnp.zeros((N, D), jnp.float32).at[buf_t].add(ys.astype(jnp.float32) * buf_w[:, None])
    return out.astype(h.dtype).reshape(Bsz, L, D)


def kernel(x, c, ada_w, ada_b, norm1_g, w_in, ssm_lam_re, ssm_lam_im, ssm_log_dt, ssm_b_re, ssm_b_im, ssm_c_re, ssm_c_im, ssm_d, ssm_w_glu, ssm_w_o, nsa_cmp_pos, nsa_cmp_w1, nsa_cmp_w2, nsa_w_o, w_out, norm2_g, moe_w_group, moe_b_group, moe_w_expert, moe_b_expert, moe_w_gate, moe_w_up, moe_w_down, final_g):
    cs = jax.nn.silu(c)
    for l in range(DEPTH):
        mod = cs @ ada_w[l] + ada_b[l]
        sh1, sc1, g1, sh2, sc2, g2 = jnp.split(mod[:, None, :], 6, axis=-1)
        h = rms_norm(x, norm1_g[l]) * (1.0 + sc1) + sh1
        x = x + g1 * token_mixer(h, w_in[l], ssm_lam_re[l], ssm_lam_im[l], ssm_log_dt[l], ssm_b_re[l],
                                 ssm_b_im[l], ssm_c_re[l], ssm_c_im[l], ssm_d[l], ssm_w_glu[l], ssm_w_o[l],
                                 nsa_cmp_pos[l], nsa_cmp_w1[l], nsa_cmp_w2[l], nsa_w_o[l], w_out[l])
        h = rms_norm(x, norm2_g[l]) * (1.0 + sc2) + sh2
        x = x + g2 * hier_moe(h, moe_w_group[l], moe_b_group[l], moe_w_expert[l], moe_b_expert[l],
                              moe_w_gate[l], moe_w_up[l], moe_w_down[l])
    return rms_norm(x, final_g)
```

```python
import functools
import math

import jax
import jax.numpy as jnp
from jax import lax
from jax.experimental import pallas as pl
from jax.experimental.pallas import tpu as pltpu

D_MODEL = 1024
DEPTH = 2
SSM_GROUP = 16
SSM_STATE = 64
SSM_WIDTH = D_MODEL // 2
SSM_GROUPS = SSM_WIDTH // SSM_GROUP
HEAD_DIM = 64
N_HEADS = D_MODEL // HEAD_DIM
N_KV = N_HEADS // 4
HPG = N_HEADS // N_KV
ATTN_WIDTH = N_HEADS * HEAD_DIM
KV_WIDTH = N_KV * HEAD_DIM
CMP_BLOCK = 32
CMP_STRIDE = 16
CMP_HIDDEN = 4 * HEAD_DIM
SLC_BLOCK = 64
SLC_TOPK = 16
WINDOW = 512
Q_CHUNK = 64
FORCED_SCORE = 1e4
N_GROUPS = 4
EXP_PER_GROUP = 8
N_EXPERTS = N_GROUPS * EXP_PER_GROUP
TOPK_IN_GROUP = 2
D_EXPERT = D_MODEL // 4
MOE_BLOCK = 128
EPS = 1e-6
NEG = -1e30

VMEM_LIMIT_BYTES = 48 * 1024 * 1024


def _mm_kernel(a_ref, b_ref, o_ref):
    a = a_ref[...].astype(jnp.bfloat16)
    b = b_ref[...].astype(jnp.bfloat16)
    o_ref[...] = jnp.dot(a, b, preferred_element_type=jnp.float32).astype(o_ref.dtype)


def pmm(a, b, tm=512, tn=512, out_dtype=jnp.float32, interpret=False):
    M, K = a.shape
    _, N = b.shape
    tm = min(tm, M)
    tn = min(tn, N)
    assert M % tm == 0 and N % tn == 0, (M, N, tm, tn)
    return pl.pallas_call(
        _mm_kernel,
        interpret=interpret,
        grid=(M // tm, N // tn),
        in_specs=[pl.BlockSpec((tm, K), lambda i, j: (i, 0)),
                  pl.BlockSpec((K, tn), lambda i, j: (0, j))],
        out_specs=pl.BlockSpec((tm, tn), lambda i, j: (i, j)),
        out_shape=jax.ShapeDtypeStruct((M, N), out_dtype),
        compiler_params=pltpu.CompilerParams(
            dimension_semantics=("parallel", "parallel"),
            vmem_limit_bytes=VMEM_LIMIT_BYTES),
    )(a, b)


MASK_FILL = -2e30
MAX_FLOOR = -1e30
KEY_STEP = 256
LANES = 128


def _online_step(carry, st, vT, vis):
    m, l, acc = carry
    st = jnp.where(vis, st, MASK_FILL)
    m_new = jnp.maximum(m, jnp.max(st, axis=0, keepdims=True))
    alpha = jnp.exp(m - m_new)
    p = jnp.exp(st - m_new)
    l = alpha * l + jnp.sum(p, axis=0, keepdims=True)
    acc = alpha * acc + jnp.dot(vT, p.astype(jnp.bfloat16), preferred_element_type=jnp.float32)
    return m_new, l, acc


def _nsa_kernel(qT_ref, kc_ref, vcT_ref, ks_ref, vsT_ref, kw_ref, vwT_ref, g_ref, ovT_ref, o_ref, sel_ref,
                *, n_cmp, n_slc, n_sel):
    f32 = jnp.float32
    bf16 = jnp.bfloat16
    c = pl.program_id(2)
    q = qT_ref[0, 0, 0]
    ncols = q.shape[1]
    ncp = kc_ref.shape[2]
    lane = lax.broadcasted_iota(jnp.int32, (1, ncols), 1)
    tpos = c * Q_CHUNK + (lane & (Q_CHUNK - 1))

    s = jnp.dot(kc_ref[0, 0], q, preferred_element_type=f32)
    n_idx = lax.broadcasted_iota(jnp.int32, (ncp, 1), 0)
    vis = (n_idx * CMP_STRIDE + (CMP_BLOCK - 1) <= tpos) & (n_idx < n_cmp)
    s = jnp.where(vis, s, MASK_FILL)
    m = jnp.max(s, axis=0, keepdims=True)
    p = jnp.where(vis, jnp.exp(s - m), 0.0)
    l = jnp.sum(p, axis=0, keepdims=True)
    p = p / jnp.where(l > 0.0, l, 1.0)
    p_hi = p.astype(bf16)
    o_cmp = jnp.dot(vcT_ref[0, 0], p_hi, preferred_element_type=f32)

    p_lo = (p - p_hi.astype(f32)).astype(bf16)
    ov = ovT_ref[...]
    imp4 = (jnp.dot(ov, p_hi, preferred_element_type=f32)
            + jnp.dot(ov, p_lo, preferred_element_type=f32))
    z = imp4[:, :LANES] + imp4[:, LANES:]
    imp = z + pltpu.roll(z, Q_CHUNK, axis=1)

    jidx = lax.broadcasted_iota(jnp.int32, (n_slc, LANES), 0)
    forced = (jidx == 0) | (jidx == c) | (jidx == c - 1)
    work = jnp.where(forced, FORCED_SCORE, imp)
    work = jnp.where(jidx > c, -1.0, work)

    def pick(_, carry):
        work, sel = carry
        mx = jnp.max(work, axis=0, keepdims=True)
        first = jnp.min(jnp.where(work == mx, jidx, n_slc), axis=0, keepdims=True)
        hit = jidx == first
        return jnp.where(hit, -2.0, work), jnp.where(hit, 1.0, sel)

    _, sel = lax.fori_loop(0, n_sel, pick, (work, jnp.zeros((n_slc, LANES), f32)))
    sel_ref[...] = jnp.concatenate([sel, sel], axis=1)

    init = (jnp.full((1, ncols), MAX_FLOOR, f32), jnp.zeros((1, ncols), f32),
            jnp.zeros((HEAD_DIM, ncols), f32))
    blocks_per_step = KEY_STEP // SLC_BLOCK

    def slc_step(s_i, carry, causal):
        off = pl.multiple_of(s_i * KEY_STEP, KEY_STEP)
        st = jnp.dot(ks_ref[0, 0, pl.ds(off, KEY_STEP), :], q, preferred_element_type=f32)
        vis = jnp.concatenate(
            [jnp.broadcast_to(sel_ref[pl.ds(s_i * blocks_per_step + i, 1), :] > 0.5, (SLC_BLOCK, ncols))
             for i in range(blocks_per_step)], axis=0)
        if causal:
            kpos = off + lax.broadcasted_iota(jnp.int32, (KEY_STEP, 1), 0)
            vis = vis & (kpos <= tpos)
        return _online_step(carry, st, vsT_ref[0, 0, :, pl.ds(off, KEY_STEP)], vis)

    n_full = c // blocks_per_step
    carry = lax.fori_loop(0, n_full, lambda i, cr: slc_step(i, cr, False), init)
    _, l_s, acc_s = slc_step(n_full, carry, True)
    o_slc = acc_s / l_s

    base = (c // 2) * LANES
    carry = init
    for o, sz in ((0, 256), (256, 256), (512, 128)):
        off = pl.multiple_of(base + o, LANES)
        st = jnp.dot(kw_ref[0, 0, pl.ds(off, sz), :], q, preferred_element_type=f32)
        kpos = off - WINDOW + lax.broadcasted_iota(jnp.int32, (sz, 1), 0)
        d = tpos - kpos
        vis = (d >= 0) & (d < WINDOW) & (kpos >= 0)
        carry = _online_step(carry, st, vwT_ref[0, 0, :, pl.ds(off, sz)], vis)
    _, l_w, acc_w = carry
    o_win = acc_w / l_w

    g = g_ref[0, 0, 0]
    o = g[0:1] * o_cmp + g[1:2] * o_slc + g[2:3] * o_win
    o_ref[0, 0, 0] = o.astype(o_ref.dtype)


def nsa_attention_pallas(q2, kvs, gates2, cmp_pos, cmp_w1, cmp_w2, B, L, interpret=False):
    f32, bf16 = jnp.float32, jnp.bfloat16
    G, H, hd = N_KV, HPG, HEAD_DIM
    C = L // Q_CHUNK
    n16 = L // CMP_STRIDE
    n_cmp = (L - CMP_BLOCK) // CMP_STRIDE + 1
    ncp = -(-n_cmp // LANES) * LANES
    n_slc = L // SLC_BLOCK
    n_sel = min(SLC_TOPK, n_slc)
    k_cmp, v_cmp, k_slc, v_slc, k_win, v_win = kvs

    def compress(kv2, pos, w1, w2):
        x = kv2.reshape(B, n16, CMP_STRIDE, G, hd).transpose(0, 3, 1, 2, 4).reshape(B * G * n16, CMP_STRIDE * hd)
        half = CMP_STRIDE * hd
        wcat = jnp.concatenate([w1[:half], w1[half:]], axis=1)
        ab = pmm(x, wcat, interpret=interpret).reshape(B, G, n16, 2 * CMP_HIDDEN)
        posb = pmm(jnp.pad(pos.reshape(1, -1), ((0, 7), (0, 0))), w1, interpret=interpret)[0]
        hid = ab[:, :, :-1, :CMP_HIDDEN] + ab[:, :, 1:, CMP_HIDDEN:] + posb
        hid = jnp.pad(jax.nn.gelu(hid), ((0, 0), (0, 0), (0, ncp - n_cmp), (0, 0)))
        out = pmm(hid.reshape(B * G * ncp, CMP_HIDDEN), w2, interpret=interpret)
        return out.reshape(B, G, ncp, hd)

    kc = compress(k_cmp, cmp_pos[0], cmp_w1[0], cmp_w2[0]).astype(bf16)
    vcT = compress(v_cmp, cmp_pos[1], cmp_w1[1], cmp_w2[1]).astype(bf16).transpose(0, 1, 3, 2)

    def key_major(a):
        return a.astype(bf16).reshape(B, L, G, hd).transpose(0, 2, 1, 3)

    def dim_major(a):
        return a.astype(bf16).reshape(B, L, G, hd).transpose(0, 2, 3, 1)

    ks = key_major(k_slc)
    vsT = dim_major(v_slc)
    kw = jnp.pad(key_major(k_win), ((0, 0), (0, 0), (WINDOW, 0), (0, 0)))
    vwT = jnp.pad(dim_major(v_win), ((0, 0), (0, 0), (0, 0), (WINDOW, 0)))
    scale = HEAD_DIM ** -0.5
    qT = (q2.astype(f32) * scale).astype(bf16).reshape(B, C, Q_CHUNK, G, H, hd)
    qT = qT.transpose(0, 3, 1, 5, 4, 2).reshape(B, G, C, hd, H * Q_CHUNK)
    gT = gates2.astype(f32).reshape(B, C, Q_CHUNK, 3, G, H).transpose(0, 4, 1, 3, 5, 2).reshape(B, G, C, 3, H * Q_CHUNK)
    c_start = jnp.arange(ncp)[None, :] * CMP_STRIDE
    s_start = jnp.arange(n_slc)[:, None] * SLC_BLOCK
    ovT = ((c_start < s_start + SLC_BLOCK) & (c_start + CMP_BLOCK > s_start)
           & (jnp.arange(ncp)[None, :] < n_cmp)).astype(bf16)

    ncols = H * Q_CHUNK
    bg = lambda b, g, c: (b, g, 0, 0)
    bgc = lambda b, g, c: (b, g, c, 0, 0)
    oT = pl.pallas_call(
        functools.partial(_nsa_kernel, n_cmp=n_cmp, n_slc=n_slc, n_sel=n_sel),
        grid=(B, G, C),
        in_specs=[pl.BlockSpec((1, 1, 1, hd, ncols), bgc),
                  pl.BlockSpec((1, 1, ncp, hd), bg),
                  pl.BlockSpec((1, 1, hd, ncp), bg),
                  pl.BlockSpec((1, 1, L, hd), bg),
                  pl.BlockSpec((1, 1, hd, L), bg),
                  pl.BlockSpec((1, 1, L + WINDOW, hd), bg),
                  pl.BlockSpec((1, 1, hd, L + WINDOW), bg),
                  pl.BlockSpec((1, 1, 1, 3, ncols), bgc),
                  pl.BlockSpec((n_slc, ncp), lambda b, g, c: (0, 0))],
        out_specs=pl.BlockSpec((1, 1, 1, hd, ncols), bgc),
        out_shape=jax.ShapeDtypeStruct((B, G, C, hd, ncols), bf16),
        scratch_shapes=[pltpu.VMEM((n_slc, ncols), f32)],
        compiler_params=pltpu.CompilerParams(
            dimension_semantics=("parallel", "parallel", "arbitrary"),
            vmem_limit_bytes=VMEM_LIMIT_BYTES),
        interpret=interpret,
        name="nsa_attention",
    )(qT, kc, vcT, ks, vsT, kw, vwT, gT, ovT)
    o = oT.reshape(B, G, C, hd, H, Q_CHUNK).transpose(0, 2, 5, 1, 4, 3)
    return o.reshape(B * L, G * H * hd)


S5_CHUNK = 32


def _s5_state_kernel(x_ref, ws_ref, s_ref):
    s_ref[0] = jnp.dot(x_ref[0].astype(jnp.bfloat16), ws_ref[0], preferred_element_type=jnp.float32)


def _s5_scan_kernel(sre_ref, sim_ref, are_ref, aim_ref, hre_ref, him_ref, *, chunks_per_batch):
    a_re = are_ref[...]
    a_im = aim_ref[...]
    n_batch = sre_ref.shape[0] // chunks_per_batch

    def step(c, carry):
        new = []
        for b in range(n_batch):
            h_re, h_im = carry[b]
            row = b * chunks_per_batch + c
            hre_ref[row] = h_re
            him_ref[row] = h_im
            new.append((a_re * h_re - a_im * h_im + sre_ref[row],
                        a_re * h_im + a_im * h_re + sim_ref[row]))
        return tuple(new)

    zero = jnp.zeros(a_re.shape, jnp.float32)
    lax.fori_loop(0, chunks_per_batch, step, tuple((zero, zero) for _ in range(n_batch)))


def _s5_out_kernel(x_ref, hp_ref, m_ref, wo_ref, d_ref, y_ref):
    x = x_ref[0]
    y = jnp.dot(x.astype(jnp.bfloat16), m_ref[0], preferred_element_type=jnp.float32)
    y = y + jnp.dot(hp_ref[0].astype(jnp.bfloat16), wo_ref[0], preferred_element_type=jnp.float32)
    y_ref[0] = y + x * d_ref[0]


def s5_scan_pallas(u2, lam_re, lam_im, log_dt, b_re, b_im, c_re, c_im, d_skip, B, L, interpret=False):
    f32, bf16 = jnp.float32, jnp.bfloat16
    G, P, H, T = SSM_GROUPS, SSM_STATE, SSM_GROUP, S5_CHUNK
    hi = lax.Precision.HIGHEST
    NC = B * L // T
    dt = jnp.exp(log_dt.astype(f32))[:, None]
    lr, li = lam_re.astype(f32), lam_im.astype(f32)
    mag = jnp.exp(lr * dt)
    ab_re, ab_im = mag * jnp.cos(li * dt), mag * jnp.sin(li * dt)
    den = lr * lr + li * li
    nr = ab_re - 1.0
    cr = (nr * lr + ab_im * li) / den
    cim = (ab_im * lr - nr * li) / den
    br, bim = b_re.astype(f32), b_im.astype(f32)
    bb_re = cr[..., None] * br - cim[..., None] * bim
    bb_im = cr[..., None] * bim + cim[..., None] * br
    k = jnp.arange(T + 1, dtype=f32)[:, None, None]
    pw_mag = jnp.exp(k * (lr * dt))
    pw_re = pw_mag * jnp.cos(k * (li * dt))
    pw_im = pw_mag * jnp.sin(k * (li * dt))
    cre, cimg = c_re.astype(f32), c_im.astype(f32)
    ab_r = pw_re[:T, :, :, None] * bb_re - pw_im[:T, :, :, None] * bb_im
    ab_i = pw_re[:T, :, :, None] * bb_im + pw_im[:T, :, :, None] * bb_re
    kern = (jnp.einsum('gop,tgpi->tgoi', cre, ab_r, precision=hi)
            - jnp.einsum('gop,tgpi->tgoi', cimg, ab_i, precision=hi))
    tt = jnp.arange(T)
    lag = tt[None, :] - tt[:, None]
    m = jnp.where((lag >= 0)[:, :, None, None, None], kern[jnp.clip(lag, 0, T - 1)], 0.0)
    m = m.transpose(2, 0, 4, 1, 3).reshape(G, T * H, T * H).astype(bf16)
    ws_re = ab_r[::-1].transpose(1, 0, 3, 2).reshape(G, T * H, P)
    ws_im = ab_i[::-1].transpose(1, 0, 3, 2).reshape(G, T * H, P)
    ws = jnp.concatenate([ws_re, ws_im], axis=-1).astype(bf16)
    ca_re = cre[None] * pw_re[1:, :, None, :] - cimg[None] * pw_im[1:, :, None, :]
    ca_im = cre[None] * pw_im[1:, :, None, :] + cimg[None] * pw_re[1:, :, None, :]
    wo = jnp.concatenate([ca_re.transpose(1, 3, 0, 2).reshape(G, P, T * H),
                          -ca_im.transpose(1, 3, 0, 2).reshape(G, P, T * H)], axis=1).astype(bf16)
    dtile = jnp.tile(d_skip.astype(f32), (1, T)).reshape(G, 1, T * H)

    xg = u2.astype(f32).reshape(NC, T, G, H).transpose(2, 0, 1, 3).reshape(G, NC, T * H)
    cparams = pltpu.CompilerParams(dimension_semantics=("parallel",), vmem_limit_bytes=VMEM_LIMIT_BYTES)
    s = pl.pallas_call(
        _s5_state_kernel, grid=(G,),
        in_specs=[pl.BlockSpec((1, NC, T * H), lambda g: (g, 0, 0)),
                  pl.BlockSpec((1, T * H, 2 * P), lambda g: (g, 0, 0))],
        out_specs=pl.BlockSpec((1, NC, 2 * P), lambda g: (g, 0, 0)),
        out_shape=jax.ShapeDtypeStruct((G, NC, 2 * P), f32),
        compiler_params=cparams, interpret=interpret, name="s5_chunk_state",
    )(xg, ws)
    sp = s.reshape(G, NC, 2, P).transpose(1, 2, 0, 3).reshape(NC, 2, G // 2, 2 * P)
    a_re = pw_re[T].reshape(G // 2, 2 * P)
    a_im = pw_im[T].reshape(G // 2, 2 * P)
    full = lambda shape: pl.BlockSpec(shape, lambda i: (0,) * len(shape))
    st_shape = (NC, G // 2, 2 * P)
    h_re, h_im = pl.pallas_call(
        functools.partial(_s5_scan_kernel, chunks_per_batch=L // T), grid=(1,),
        in_specs=[full(st_shape), full(st_shape), full((G // 2, 2 * P)), full((G // 2, 2 * P))],
        out_specs=[full(st_shape), full(st_shape)],
        out_shape=[jax.ShapeDtypeStruct(st_shape, f32)] * 2,
        compiler_params=pltpu.CompilerParams(dimension_semantics=("arbitrary",), vmem_limit_bytes=VMEM_LIMIT_BYTES),
        interpret=interpret, name="s5_state_scan",
    )(sp[:, 0], sp[:, 1], a_re, a_im)
    hp = jnp.stack([h_re, h_im], axis=1).reshape(NC, 2, G, P).transpose(2, 0, 1, 3).reshape(G, NC, 2 * P)
    y = pl.pallas_call(
        _s5_out_kernel, grid=(G,),
        in_specs=[pl.BlockSpec((1, NC, T * H), lambda g: (g, 0, 0)),
                  pl.BlockSpec((1, NC, 2 * P), lambda g: (g, 0, 0)),
                  pl.BlockSpec((1, T * H, T * H), lambda g: (g, 0, 0)),
                  pl.BlockSpec((1, 2 * P, T * H), lambda g: (g, 0, 0)),
                  pl.BlockSpec((1, 1, T * H), lambda g: (g, 0, 0))],
        out_specs=pl.BlockSpec((1, NC, T * H), lambda g: (g, 0, 0)),
        out_shape=jax.ShapeDtypeStruct((G, NC, T * H), f32),
        compiler_params=cparams, interpret=interpret, name="s5_chunk_out",
    )(xg, hp, m, wo, dtile)
    return y.reshape(G, NC, T, H).transpose(1, 2, 0, 3).reshape(B * L, G * H)


def rms_norm(x, g):
    xf = x.astype(jnp.float32)
    y = xf * lax.rsqrt(jnp.mean(xf * xf, axis=-1, keepdims=True) + EPS)
    return (y * g.astype(jnp.float32)).astype(x.dtype)


def masked_softmax(s, mask):
    p = jax.nn.softmax(jnp.where(mask, s.astype(jnp.float32), NEG), axis=-1)
    return p * mask


def s5_mixer(u, lam_re, lam_im, log_dt, b_re, b_im, c_re, c_im, d_skip, w_glu):
    f32 = jnp.float32
    Bsz, L, _ = u.shape
    ug = u.astype(f32).reshape(Bsz, L, SSM_GROUPS, SSM_GROUP)
    dt = jnp.exp(log_dt.astype(f32))[:, None]
    lr, li = lam_re.astype(f32), lam_im.astype(f32)
    mag = jnp.exp(lr * dt)
    ab_re, ab_im = mag * jnp.cos(li * dt), mag * jnp.sin(li * dt)
    den = lr * lr + li * li
    nr = ab_re - 1.0
    cr = (nr * lr + ab_im * li) / den
    cim = (ab_im * lr - nr * li) / den
    br, bim = b_re.astype(f32), b_im.astype(f32)
    bb_re = cr[..., None] * br - cim[..., None] * bim
    bb_im = cr[..., None] * bim + cim[..., None] * br
    bu_re = jnp.einsum('blgh,gph->blgp', ug, bb_re)
    bu_im = jnp.einsum('blgh,gph->blgp', ug, bb_im)
    a_re = jnp.broadcast_to(ab_re, bu_re.shape)
    a_im = jnp.broadcast_to(ab_im, bu_im.shape)

    def combine(e1, e2):
        a1r, a1i, b1r, b1i = e1
        a2r, a2i, b2r, b2i = e2
        return (a2r * a1r - a2i * a1i, a2r * a1i + a2i * a1r,
                a2r * b1r - a2i * b1i + b2r, a2r * b1i + a2i * b1r + b2i)

    _, _, s_re, s_im = lax.associative_scan(combine, (a_re, a_im, bu_re, bu_im), axis=1)
    y = (jnp.einsum('blgp,ghp->blgh', s_re, c_re.astype(f32))
         - jnp.einsum('blgp,ghp->blgh', s_im, c_im.astype(f32))
         + d_skip.astype(f32) * ug)
    y = jax.nn.gelu(y.reshape(Bsz, L, SSM_WIDTH))
    y = y * jax.nn.sigmoid(pmm(y.reshape(Bsz * L, SSM_WIDTH), w_glu).reshape(Bsz, L, SSM_WIDTH))
    return y.astype(u.dtype)


def compress_kv(kv, pos, w1, w2):
    Bsz, L, G, hd = kv.shape
    n_cmp = (L - CMP_BLOCK) // CMP_STRIDE + 1
    idx = jnp.arange(n_cmp)[:, None] * CMP_STRIDE + jnp.arange(CMP_BLOCK)[None, :]
    blocks = kv[:, idx] + pos[:, None, :]
    blocks = jnp.swapaxes(blocks, 2, 3).reshape(Bsz, n_cmp, G, CMP_BLOCK * hd)
    return jax.nn.gelu(blocks @ w1) @ w2


def nsa_attention(q, k_cmp, v_cmp, k_slc, v_slc, k_win, v_win, gates, cmp_pos, cmp_w1, cmp_w2):
    Bsz, L = q.shape[:2]
    n_cmp = (L - CMP_BLOCK) // CMP_STRIDE + 1
    n_slc = L // SLC_BLOCK
    n_sel = min(SLC_TOPK, n_slc)
    n_chunks = L // Q_CHUNK
    scale = HEAD_DIM ** -0.5
    kc = compress_kv(k_cmp, cmp_pos[0], cmp_w1[0], cmp_w2[0])
    vc = compress_kv(v_cmp, cmp_pos[1], cmp_w1[1], cmp_w2[1])
    cmp_end = jnp.arange(n_cmp) * CMP_STRIDE + CMP_BLOCK - 1
    c_start = jnp.arange(n_cmp)[:, None] * CMP_STRIDE
    s_start = jnp.arange(n_slc)[None, :] * SLC_BLOCK
    overlap = ((c_start < s_start + SLC_BLOCK) & (c_start + CMP_BLOCK > s_start)).astype(jnp.float32)
    ks_blk = jnp.moveaxis(k_slc.reshape(Bsz, n_slc, SLC_BLOCK, N_KV, HEAD_DIM), 3, 1)
    vs_blk = jnp.moveaxis(v_slc.reshape(Bsz, n_slc, SLC_BLOCK, N_KV, HEAD_DIM), 3, 1)
    kw = jnp.pad(k_win, ((0, 0), (WINDOW, 0), (0, 0), (0, 0)))
    vw = jnp.pad(v_win, ((0, 0), (WINDOW, 0), (0, 0), (0, 0)))
    bi = jnp.arange(Bsz)[:, None, None, None]
    gi = jnp.arange(N_KV)[None, :, None, None]
    blk = jnp.arange(n_slc)

    def chunk(args):
        c_idx, qc, gc = args
        t = c_idx * Q_CHUNK + jnp.arange(Q_CHUNK)
        s = jnp.einsum('bqghd,bngd->bqghn', qc, kc) * scale
        m = cmp_end[None, :] <= t[:, None]
        p_cmp = masked_softmax(s, m[None, :, None, None, :])
        o_cmp = jnp.einsum('bqghn,bngd->bqghd', p_cmp.astype(vc.dtype), vc)
        imp = jnp.einsum('bqghn,ns->bqgs', p_cmp, overlap)
        cur = t[:, None] // SLC_BLOCK
        forced = (blk[None, :] == 0) | (blk[None, :] == cur) | (blk[None, :] == cur - 1)
        future = blk[None, :] * SLC_BLOCK > t[:, None]
        imp = jnp.where(forced[None, :, None, :], FORCED_SCORE, imp)
        imp = jnp.where(future[None, :, None, :], -1.0, imp)
        _, sel = lax.top_k(imp, n_sel)
        sel = jnp.transpose(sel, (0, 2, 1, 3))
        ks = ks_blk[bi, gi, sel].reshape(Bsz, N_KV, Q_CHUNK, n_sel * SLC_BLOCK, HEAD_DIM)
        vs = vs_blk[bi, gi, sel].reshape(Bsz, N_KV, Q_CHUNK, n_sel * SLC_BLOCK, HEAD_DIM)
        kpos = (sel[..., None] * SLC_BLOCK + jnp.arange(SLC_BLOCK)).reshape(Bsz, N_KV, Q_CHUNK, n_sel * SLC_BLOCK)
        m = jnp.transpose(kpos <= t[None, None, :, None], (0, 2, 1, 3))[:, :, :, None, :]
        s = jnp.einsum('bqghd,bgqkd->bqghk', qc, ks) * scale
        p = masked_softmax(s, m)
        o_slc = jnp.einsum('bqghk,bgqkd->bqghd', p.astype(vs.dtype), vs)
        start = c_idx * Q_CHUNK
        kwc = lax.dynamic_slice_in_dim(kw, start, WINDOW + Q_CHUNK, axis=1)
        vwc = lax.dynamic_slice_in_dim(vw, start, WINDOW + Q_CHUNK, axis=1)
        wpos = start - WINDOW + jnp.arange(WINDOW + Q_CHUNK)
        d = t[:, None] - wpos[None, :]
        m = (d >= 0) & (d < WINDOW) & (wpos[None, :] >= 0)
        s = jnp.einsum('bqghd,bkgd->bqghk', qc, kwc) * scale
        p = masked_softmax(s, m[None, :, None, None, :])
        o_win = jnp.einsum('bqghk,bkgd->bqghd', p.astype(vwc.dtype), vwc)
        return (gc[:, :, 0, :, :, None] * o_cmp + gc[:, :, 1, :, :, None] * o_slc
                + gc[:, :, 2, :, :, None] * o_win)

    q_ch = jnp.moveaxis(q.reshape(Bsz, n_chunks, Q_CHUNK, N_KV, HPG, HEAD_DIM), 1, 0)
    g_ch = jnp.moveaxis(gates.reshape(Bsz, n_chunks, Q_CHUNK, 3, N_KV, HPG), 1, 0)
    o = lax.map(chunk, (jnp.arange(n_chunks), q_ch, g_ch))
    return jnp.moveaxis(o, 0, 1).reshape(Bsz, L, ATTN_WIDTH)


def token_mixer(h, w_in, ssm_lam_re, ssm_lam_im, ssm_log_dt, ssm_b_re, ssm_b_im, ssm_c_re, ssm_c_im,
                ssm_d, ssm_w_glu, ssm_w_o, nsa_cmp_pos, nsa_cmp_w1, nsa_cmp_w2, nsa_w_o, w_out):
    Bsz, L, D = h.shape
    N = Bsz * L
    o1 = ATTN_WIDTH
    o2 = o1 + 6 * KV_WIDTH
    o3 = o2 + 3 * N_HEADS
    o4 = o3 + SSM_WIDTH
    h2 = h.reshape(N, D)
    q = pmm(h2, w_in[:, :o1]).reshape(Bsz, L, -1)
    kv = pmm(h2, w_in[:, o1:o2]).reshape(Bsz, L, -1)
    wg = jnp.pad(w_in[:, o2:o3], ((0, 0), (0, 128 - 3 * N_HEADS)))
    nsa_g = pmm(h2, wg)[:, :3 * N_HEADS].reshape(Bsz, L, -1)
    u = pmm(h2, w_in[:, o3:o4]).reshape(Bsz, L, -1)
    merge_g = pmm(h2, w_in[:, o4:]).reshape(Bsz, L, -1)
    kvs = [t.reshape(N, KV_WIDTH) for t in jnp.split(kv, 6, axis=-1)]
    attn = nsa_attention_pallas(q.reshape(N, ATTN_WIDTH), kvs, jax.nn.sigmoid(nsa_g).reshape(N, 3 * N_HEADS),
                                nsa_cmp_pos, nsa_cmp_w1, nsa_cmp_w2, Bsz, L)
    y = s5_scan_pallas(u.reshape(N, SSM_WIDTH), ssm_lam_re, ssm_lam_im, ssm_log_dt, ssm_b_re, ssm_b_im,
                       ssm_c_re, ssm_c_im, ssm_d, Bsz, L)
    y = jax.nn.gelu(y)
    ssm = (y * jax.nn.sigmoid(pmm(y, ssm_w_glu))).reshape(Bsz, L, SSM_WIDTH)
    g_attn, g_ssm = jnp.split(jax.nn.sigmoid(merge_g), 2, axis=-1)
    merged = (g_attn * pmm(attn.reshape(N, -1), nsa_w_o).reshape(Bsz, L, D)
              + g_ssm * pmm(ssm.reshape(N, -1), ssm_w_o).reshape(Bsz, L, D))
    return pmm(merged.reshape(N, D), w_out).reshape(Bsz, L, D)


def hier_moe(h, w_group, b_group, w_expert, b_expert, w_gate, w_up, w_down):
    Bsz, L, D = h.shape
    xt = h.reshape(-1, D)
    N = xt.shape[0]
    g_prob = jax.nn.softmax((xt @ w_group + b_group).astype(jnp.float32), axis=-1)
    g_p, g_idx = lax.top_k(g_prob, 1)
    e_logits = (xt @ w_expert + b_expert).astype(jnp.float32).reshape(N, N_GROUPS, EXP_PER_GROUP)
    e_logits = jnp.take_along_axis(e_logits, jnp.broadcast_to(g_idx[:, :, None], (N, 1, EXP_PER_GROUP)), axis=1)[:, 0]
    e_p, e_idx = lax.top_k(jax.nn.softmax(e_logits, axis=-1), TOPK_IN_GROUP)
    w = g_p * e_p / jnp.sum(e_p, axis=-1, keepdims=True)
    eid = g_idx * EXP_PER_GROUP + e_idx
    A = N * TOPK_IN_GROUP
    flat_e = eid.reshape(-1)
    flat_t = jnp.repeat(jnp.arange(N, dtype=jnp.int32), TOPK_IN_GROUP)
    flat_w = w.reshape(-1)
    order = jnp.argsort(flat_e)
    se = flat_e[order]
    counts = jnp.zeros(N_EXPERTS, jnp.int32).at[flat_e].add(1)
    starts = jnp.cumsum(counts) - counts
    pcounts = (counts + MOE_BLOCK - 1) // MOE_BLOCK * MOE_BLOCK
    pends = jnp.cumsum(pcounts)
    pstarts = pends - pcounts
    dest = pstarts[se] + jnp.arange(A) - starts[se]
    P = A + N_EXPERTS * MOE_BLOCK
    n_blk = P // MOE_BLOCK
    buf_t = jnp.zeros(P, jnp.int32).at[dest].set(flat_t[order])
    buf_w = jnp.zeros(P, jnp.float32).at[dest].set(flat_w[order])
    blk_e = jnp.minimum(jnp.searchsorted(pends, jnp.arange(n_blk) * MOE_BLOCK, side='right'), N_EXPERTS - 1)
    xs = xt[buf_t].reshape(n_blk, MOE_BLOCK, D)

    def expert_block(args):
        xb, e = args
        return (jax.nn.silu(xb @ w_gate[e]) * (xb @ w_up[e])) @ w_down[e]

    ys = lax.map(expert_block, (xs, blk_e)).reshape(P, D)
    out = jnp.zeros((N, D), jnp.float32).at[buf_t].add(ys.astype(jnp.float32) * buf_w[:, None])
    return out.astype(h.dtype).reshape(Bsz, L, D)


def kernel(x, c, ada_w, ada_b, norm1_g, w_in, ssm_lam_re, ssm_lam_im, ssm_log_dt, ssm_b_re, ssm_b_im, ssm_c_re, ssm_c_im, ssm_d, ssm_w_glu, ssm_w_o, nsa_cmp_pos, nsa_cmp_w1, nsa_cmp_w2, nsa_w_o, w_out, norm2_g, moe_w_group, moe_b_group, moe_w_expert, moe_b_expert, moe_w_gate, moe_w_up, moe_w_down, final_g):
    cs = jax.nn.silu(c)
    for l in range(DEPTH):
        mod = cs @ ada_w[l] + ada_b[l]
        sh1, sc1, g1, sh2, sc2, g2 = jnp.split(mod[:, None, :], 6, axis=-1)
        h = rms_norm(x, norm1_g[l]) * (1.0 + sc1) + sh1
        x = x + g1 * token_mixer(h, w_in[l], ssm_lam_re[l], ssm_lam_im[l], ssm_log_dt[l], ssm_b_re[l],
                                 ssm_b_im[l], ssm_c_re[l], ssm_c_im[l], ssm_d[l], ssm_w_glu[l], ssm_w_o[l],
                                 nsa_cmp_pos[l], nsa_cmp_w1[l], nsa_cmp_w2[l], nsa_w_o[l], w_out[l])
        h = rms_norm(x, norm2_g[l]) * (1.0 + sc2) + sh2
        x = x + g2 * hier_moe(h, moe_w_group[l], moe_b_group[l], moe_w_expert[l], moe_b_expert[l],
                              moe_w_gate[l], moe_w_up[l], moe_w_down[l])
    return rms_norm(x, final_g)
```

```python
import functools
import math

import jax
import jax.numpy as jnp
from jax import lax
from jax.experimental import pallas as pl
from jax.experimental.pallas import tpu as pltpu

D_MODEL = 1024
DEPTH = 2
SSM_GROUP = 16
SSM_STATE = 64
SSM_WIDTH = D_MODEL // 2
SSM_GROUPS = SSM_WIDTH // SSM_GROUP
HEAD_DIM = 64
N_HEADS = D_MODEL // HEAD_DIM
N_KV = N_HEADS // 4
HPG = N_HEADS // N_KV
ATTN_WIDTH = N_HEADS * HEAD_DIM
KV_WIDTH = N_KV * HEAD_DIM
CMP_BLOCK = 32
CMP_STRIDE = 16
CMP_HIDDEN = 4 * HEAD_DIM
SLC_BLOCK = 64
SLC_TOPK = 16
WINDOW = 512
Q_CHUNK = 64
FORCED_SCORE = 1e4
N_GROUPS = 4
EXP_PER_GROUP = 8
N_EXPERTS = N_GROUPS * EXP_PER_GROUP
TOPK_IN_GROUP = 2
D_EXPERT = D_MODEL // 4
MOE_BLOCK = 128
EPS = 1e-6
NEG = -1e30

VMEM_LIMIT_BYTES = 48 * 1024 * 1024


def _mm_kernel(a_ref, b_ref, o_ref):
    a = a_ref[...].astype(jnp.bfloat16)
    b = b_ref[...].astype(jnp.bfloat16)
    o_ref[...] = jnp.dot(a, b, preferred_element_type=jnp.float32).astype(o_ref.dtype)


def pmm(a, b, tm=512, tn=512, out_dtype=jnp.float32, interpret=False):
    M, K = a.shape
    _, N = b.shape
    tm = min(tm, M)
    tn = min(tn, N)
    assert M % tm == 0 and N % tn == 0, (M, N, tm, tn)
    return pl.pallas_call(
        _mm_kernel,
        interpret=interpret,
        grid=(M // tm, N // tn),
        in_specs=[pl.BlockSpec((tm, K), lambda i, j: (i, 0)),
                  pl.BlockSpec((K, tn), lambda i, j: (0, j))],
        out_specs=pl.BlockSpec((tm, tn), lambda i, j: (i, j)),
        out_shape=jax.ShapeDtypeStruct((M, N), out_dtype),
        compiler_params=pltpu.CompilerParams(
            dimension_semantics=("parallel", "parallel"),
            vmem_limit_bytes=VMEM_LIMIT_BYTES),
    )(a, b)


MASK_FILL = -2e30
MAX_FLOOR = -1e30
KEY_STEP = 256
BIG_KEY_STEP = 1024
WIN_SPAN = WINDOW + 2 * Q_CHUNK
LANES = 128


def _softmax_partial(st, vT, vis):
    st = jnp.where(vis, st, MASK_FILL)
    m = jnp.maximum(jnp.max(st, axis=0, keepdims=True), MAX_FLOOR)
    p = jnp.exp(st - m)
    l = jnp.sum(p, axis=0, keepdims=True)
    acc = jnp.dot(vT, p.astype(jnp.bfloat16), preferred_element_type=jnp.float32)
    return m, l, acc


def _softmax_merge(parts):
    m = functools.reduce(jnp.maximum, [p[0] for p in parts])
    l = 0.0
    acc = 0.0
    for m_j, l_j, acc_j in parts:
        w = jnp.exp(m_j - m)
        l = l + w * l_j
        acc = acc + w * acc_j
    return m, l, acc


def _nsa_kernel(qT_ref, kc_ref, vcT_ref, ks_ref, vsT_ref, kw_ref, vwT_ref, g_ref, ovT_ref, o_ref, sel_ref,
                *, n_cmp, n_slc, n_sel):
    f32 = jnp.float32
    bf16 = jnp.bfloat16
    c = pl.program_id(2)
    q = qT_ref[0, 0, 0]
    ncols = q.shape[1]
    ncp = kc_ref.shape[2]
    lane = lax.broadcasted_iota(jnp.int32, (1, ncols), 1)
    tpos = c * Q_CHUNK + (lane & (Q_CHUNK - 1))

    s = jnp.dot(kc_ref[0, 0], q, preferred_element_type=f32)
    n_idx = lax.broadcasted_iota(jnp.int32, (ncp, 1), 0)
    vis = (n_idx * CMP_STRIDE + (CMP_BLOCK - 1) <= tpos) & (n_idx < n_cmp)
    s = jnp.where(vis, s, MASK_FILL)
    m = jnp.max(s, axis=0, keepdims=True)
    p = jnp.where(vis, jnp.exp(s - m), 0.0)
    l = jnp.sum(p, axis=0, keepdims=True)
    p = p / jnp.where(l > 0.0, l, 1.0)
    p_hi = p.astype(bf16)
    o_cmp = jnp.dot(vcT_ref[0, 0], p_hi, preferred_element_type=f32)

    p_lo = (p - p_hi.astype(f32)).astype(bf16)
    ov = ovT_ref[...]
    imp4 = (jnp.dot(ov, p_hi, preferred_element_type=f32)
            + jnp.dot(ov, p_lo, preferred_element_type=f32))
    z = imp4[:, :LANES] + imp4[:, LANES:]
    imp = z + pltpu.roll(z, Q_CHUNK, axis=1)

    jidx = lax.broadcasted_iota(jnp.int32, (n_slc, LANES), 0)
    forced = (jidx == 0) | (jidx == c) | (jidx == c - 1)
    work = jnp.where(forced, FORCED_SCORE, imp)
    work = jnp.where(jidx > c, -1.0, work)

    def pick(_, carry):
        work, sel = carry
        mx = jnp.max(work, axis=0, keepdims=True)
        first = jnp.min(jnp.where(work == mx, jidx, n_slc), axis=0, keepdims=True)
        hit = jidx == first
        return jnp.where(hit, -2.0, work), jnp.where(hit, 1.0, sel)

    _, sel = lax.fori_loop(0, n_sel, pick, (work, jnp.zeros((n_slc, LANES), f32)))
    sel_ref[...] = jnp.concatenate([sel, sel], axis=1)

    init = (jnp.full((1, ncols), MAX_FLOOR, f32), jnp.zeros((1, ncols), f32),
            jnp.zeros((HEAD_DIM, ncols), f32))

    def slc_step(step, nkeys, carry, causal):
        off = pl.multiple_of(step * nkeys, nkeys)
        blk0 = step * (nkeys // SLC_BLOCK)
        st = jnp.dot(ks_ref[0, 0, pl.ds(off, nkeys), :], q, preferred_element_type=f32)
        vis = jnp.concatenate(
            [jnp.broadcast_to(sel_ref[pl.ds(blk0 + i, 1), :] > 0.5, (SLC_BLOCK, ncols))
             for i in range(nkeys // SLC_BLOCK)], axis=0)
        if causal:
            kpos = off + lax.broadcasted_iota(jnp.int32, (nkeys, 1), 0)
            vis = vis & (kpos <= tpos)
        return _softmax_merge([carry, _softmax_partial(st, vsT_ref[0, 0, :, pl.ds(off, nkeys)], vis)])

    n_small = c // (KEY_STEP // SLC_BLOCK)
    n_big = c // (BIG_KEY_STEP // SLC_BLOCK)
    small_per_big = BIG_KEY_STEP // KEY_STEP
    carry = lax.fori_loop(0, n_big, lambda i, cr: slc_step(i, BIG_KEY_STEP, cr, False), init)
    carry = lax.fori_loop(n_big * small_per_big, n_small, lambda i, cr: slc_step(i, KEY_STEP, cr, False), carry)
    _, l_s, acc_s = slc_step(n_small, KEY_STEP, carry, True)
    o_slc = acc_s / l_s

    base = pl.multiple_of((c // 2) * LANES, LANES)
    st = jnp.dot(kw_ref[0, 0, pl.ds(base, WIN_SPAN), :], q, preferred_element_type=f32)
    kpos = base - WINDOW + lax.broadcasted_iota(jnp.int32, (WIN_SPAN, 1), 0)
    d = tpos - kpos
    st = jnp.where((d >= 0) & (d < WINDOW) & (kpos >= 0), st, MASK_FILL)
    p_w = jnp.exp(st - jnp.max(st, axis=0, keepdims=True))
    l_w = jnp.sum(p_w, axis=0, keepdims=True)
    acc_w = jnp.dot(vwT_ref[0, 0, :, pl.ds(base, WIN_SPAN)], p_w.astype(bf16), preferred_element_type=f32)
    o_win = acc_w / l_w

    g = g_ref[0, 0, 0]
    o = g[0:1] * o_cmp + g[1:2] * o_slc + g[2:3] * o_win
    o_ref[0, 0, 0] = o.astype(o_ref.dtype)


def nsa_attention_pallas(q2, kvs, gates2, cmp_pos, cmp_w1, cmp_w2, B, L, interpret=False):
    f32, bf16 = jnp.float32, jnp.bfloat16
    G, H, hd = N_KV, HPG, HEAD_DIM
    C = L // Q_CHUNK
    n16 = L // CMP_STRIDE
    n_cmp = (L - CMP_BLOCK) // CMP_STRIDE + 1
    ncp = -(-n_cmp // LANES) * LANES
    n_slc = L // SLC_BLOCK
    n_sel = min(SLC_TOPK, n_slc)
    k_cmp, v_cmp, k_slc, v_slc, k_win, v_win = kvs

    def compress(kv2, pos, w1, w2):
        x = kv2.reshape(B, n16, CMP_STRIDE, G, hd).transpose(0, 3, 1, 2, 4).reshape(B * G * n16, CMP_STRIDE * hd)
        half = CMP_STRIDE * hd
        wcat = jnp.concatenate([w1[:half], w1[half:]], axis=1)
        ab = pmm(x, wcat, interpret=interpret).reshape(B, G, n16, 2 * CMP_HIDDEN)
        posb = pmm(jnp.pad(pos.reshape(1, -1), ((0, 7), (0, 0))), w1, interpret=interpret)[0]
        hid = ab[:, :, :-1, :CMP_HIDDEN] + ab[:, :, 1:, CMP_HIDDEN:] + posb
        hid = jnp.pad(jax.nn.gelu(hid), ((0, 0), (0, 0), (0, ncp - n_cmp), (0, 0)))
        out = pmm(hid.reshape(B * G * ncp, CMP_HIDDEN), w2, interpret=interpret)
        return out.reshape(B, G, ncp, hd)

    kc = compress(k_cmp, cmp_pos[0], cmp_w1[0], cmp_w2[0]).astype(bf16)
    vcT = compress(v_cmp, cmp_pos[1], cmp_w1[1], cmp_w2[1]).astype(bf16).transpose(0, 1, 3, 2)

    def key_major(a):
        return a.astype(bf16).reshape(B, L, G, hd).transpose(0, 2, 1, 3)

    def dim_major(a):
        return a.astype(bf16).reshape(B, L, G, hd).transpose(0, 2, 3, 1)

    ks = key_major(k_slc)
    vsT = dim_major(v_slc)
    kw = jnp.pad(key_major(k_win), ((0, 0), (0, 0), (WINDOW, 0), (0, 0)))
    vwT = jnp.pad(dim_major(v_win), ((0, 0), (0, 0), (0, 0), (WINDOW, 0)))
    scale = HEAD_DIM ** -0.5
    qT = (q2.astype(f32) * scale).astype(bf16).reshape(B, C, Q_CHUNK, G, H, hd)
    qT = qT.transpose(0, 3, 1, 5, 4, 2).reshape(B, G, C, hd, H * Q_CHUNK)
    gT = gates2.astype(f32).reshape(B, C, Q_CHUNK, 3, G, H).transpose(0, 4, 1, 3, 5, 2).reshape(B, G, C, 3, H * Q_CHUNK)
    c_start = jnp.arange(ncp)[None, :] * CMP_STRIDE
    s_start = jnp.arange(n_slc)[:, None] * SLC_BLOCK
    ovT = ((c_start < s_start + SLC_BLOCK) & (c_start + CMP_BLOCK > s_start)
           & (jnp.arange(ncp)[None, :] < n_cmp)).astype(bf16)

    ncols = H * Q_CHUNK
    bg = lambda b, g, c: (b, g, 0, 0)
    bgc = lambda b, g, c: (b, g, c, 0, 0)
    oT = pl.pallas_call(
        functools.partial(_nsa_kernel, n_cmp=n_cmp, n_slc=n_slc, n_sel=n_sel),
        grid=(B, G, C),
        in_specs=[pl.BlockSpec((1, 1, 1, hd, ncols), bgc),
                  pl.BlockSpec((1, 1, ncp, hd), bg),
                  pl.BlockSpec((1, 1, hd, ncp), bg),
                  pl.BlockSpec((1, 1, L, hd), bg),
                  pl.BlockSpec((1, 1, hd, L), bg),
                  pl.BlockSpec((1, 1, L + WINDOW, hd), bg),
                  pl.BlockSpec((1, 1, hd, L + WINDOW), bg),
                  pl.BlockSpec((1, 1, 1, 3, ncols), bgc),
                  pl.BlockSpec((n_slc, ncp), lambda b, g, c: (0, 0))],
        out_specs=pl.BlockSpec((1, 1, 1, hd, ncols), bgc),
        out_shape=jax.ShapeDtypeStruct((B, G, C, hd, ncols), bf16),
        scratch_shapes=[pltpu.VMEM((n_slc, ncols), f32)],
        compiler_params=pltpu.CompilerParams(
            dimension_semantics=("parallel", "parallel", "arbitrary"),
            vmem_limit_bytes=VMEM_LIMIT_BYTES),
        interpret=interpret,
        name="nsa_attention",
    )(qT, kc, vcT, ks, vsT, kw, vwT, gT, ovT)
    o = oT.reshape(B, G, C, hd, H, Q_CHUNK).transpose(0, 2, 5, 1, 4, 3)
    return o.reshape(B * L, G * H * hd)


S5_CHUNK = 32


def _s5_state_kernel(x_ref, ws_ref, s_ref):
    s_ref[0] = jnp.dot(x_ref[0].astype(jnp.bfloat16), ws_ref[0], preferred_element_type=jnp.float32)


def _s5_scan_kernel(sre_ref, sim_ref, are_ref, aim_ref, hre_ref, him_ref, *, chunks_per_batch):
    a_re = are_ref[...]
    a_im = aim_ref[...]
    n_batch = sre_ref.shape[0] // chunks_per_batch

    def step(c, carry):
        new = []
        for b in range(n_batch):
            h_re, h_im = carry[b]
            row = b * chunks_per_batch + c
            hre_ref[row] = h_re
            him_ref[row] = h_im
            new.append((a_re * h_re - a_im * h_im + sre_ref[row],
                        a_re * h_im + a_im * h_re + sim_ref[row]))
        return tuple(new)

    zero = jnp.zeros(a_re.shape, jnp.float32)
    lax.fori_loop(0, chunks_per_batch, step, tuple((zero, zero) for _ in range(n_batch)))


def _s5_out_kernel(x_ref, hp_ref, m_ref, wo_ref, d_ref, y_ref):
    x = x_ref[0]
    y = jnp.dot(x.astype(jnp.bfloat16), m_ref[0], preferred_element_type=jnp.float32)
    y = y + jnp.dot(hp_ref[0].astype(jnp.bfloat16), wo_ref[0], preferred_element_type=jnp.float32)
    y_ref[0] = y + x * d_ref[0]


def s5_scan_pallas(u2, lam_re, lam_im, log_dt, b_re, b_im, c_re, c_im, d_skip, B, L, interpret=False):
    f32, bf16 = jnp.float32, jnp.bfloat16
    G, P, H, T = SSM_GROUPS, SSM_STATE, SSM_GROUP, S5_CHUNK
    hi = lax.Precision.HIGHEST
    NC = B * L // T
    dt = jnp.exp(log_dt.astype(f32))[:, None]
    lr, li = lam_re.astype(f32), lam_im.astype(f32)
    mag = jnp.exp(lr * dt)
    ab_re, ab_im = mag * jnp.cos(li * dt), mag * jnp.sin(li * dt)
    den = lr * lr + li * li
    nr = ab_re - 1.0
    cr = (nr * lr + ab_im * li) / den
    cim = (ab_im * lr - nr * li) / den
    br, bim = b_re.astype(f32), b_im.astype(f32)
    bb_re = cr[..., None] * br - cim[..., None] * bim
    bb_im = cr[..., None] * bim + cim[..., None] * br
    k = jnp.arange(T + 1, dtype=f32)[:, None, None]
    pw_mag = jnp.exp(k * (lr * dt))
    pw_re = pw_mag * jnp.cos(k * (li * dt))
    pw_im = pw_mag * jnp.sin(k * (li * dt))
    cre, cimg = c_re.astype(f32), c_im.astype(f32)
    ab_r = pw_re[:T, :, :, None] * bb_re - pw_im[:T, :, :, None] * bb_im
    ab_i = pw_re[:T, :, :, None] * bb_im + pw_im[:T, :, :, None] * bb_re
    kern = (jnp.einsum('gop,tgpi->tgoi', cre, ab_r, precision=hi)
            - jnp.einsum('gop,tgpi->tgoi', cimg, ab_i, precision=hi))
    tt = jnp.arange(T)
    lag = tt[None, :] - tt[:, None]
    m = jnp.where((lag >= 0)[:, :, None, None, None], kern[jnp.clip(lag, 0, T - 1)], 0.0)
    m = m.transpose(2, 0, 4, 1, 3).reshape(G, T * H, T * H).astype(bf16)
    ws_re = ab_r[::-1].transpose(1, 0, 3, 2).reshape(G, T * H, P)
    ws_im = ab_i[::-1].transpose(1, 0, 3, 2).reshape(G, T * H, P)
    ws = jnp.concatenate([ws_re, ws_im], axis=-1).astype(bf16)
    ca_re = cre[None] * pw_re[1:, :, None, :] - cimg[None] * pw_im[1:, :, None, :]
    ca_im = cre[None] * pw_im[1:, :, None, :] + cimg[None] * pw_re[1:, :, None, :]
    wo = jnp.concatenate([ca_re.transpose(1, 3, 0, 2).reshape(G, P, T * H),
                          -ca_im.transpose(1, 3, 0, 2).reshape(G, P, T * H)], axis=1).astype(bf16)
    dtile = jnp.tile(d_skip.astype(f32), (1, T)).reshape(G, 1, T * H)

    xg = u2.astype(f32).reshape(NC, T, G, H).transpose(2, 0, 1, 3).reshape(G, NC, T * H)
    cparams = pltpu.CompilerParams(dimension_semantics=("parallel",), vmem_limit_bytes=VMEM_LIMIT_BYTES)
    s = pl.pallas_call(
        _s5_state_kernel, grid=(G,),
        in_specs=[pl.BlockSpec((1, NC, T * H), lambda g: (g, 0, 0)),
                  pl.BlockSpec((1, T * H, 2 * P), lambda g: (g, 0, 0))],
        out_specs=pl.BlockSpec((1, NC, 2 * P), lambda g: (g, 0, 0)),
        out_shape=jax.ShapeDtypeStruct((G, NC, 2 * P), f32),
        compiler_params=cparams, interpret=interpret, name="s5_chunk_state",
    )(xg, ws)
    sp = s.reshape(G, NC, 2, P).transpose(1, 2, 0, 3).reshape(NC, 2, G // 2, 2 * P)
    a_re = pw_re[T].reshape(G // 2, 2 * P)
    a_im = pw_im[T].reshape(G // 2, 2 * P)
    full = lambda shape: pl.BlockSpec(shape, lambda i: (0,) * len(shape))
    st_shape = (NC, G // 2, 2 * P)
    h_re, h_im = pl.pallas_call(
        functools.partial(_s5_scan_kernel, chunks_per_batch=L // T), grid=(1,),
        in_specs=[full(st_shape), full(st_shape), full((G // 2, 2 * P)), full((G // 2, 2 * P))],
        out_specs=[full(st_shape), full(st_shape)],
        out_shape=[jax.ShapeDtypeStruct(st_shape, f32)] * 2,
        compiler_params=pltpu.CompilerParams(dimension_semantics=("arbitrary",), vmem_limit_bytes=VMEM_LIMIT_BYTES),
        interpret=interpret, name="s5_state_scan",
    )(sp[:, 0], sp[:, 1], a_re, a_im)
    hp = jnp.stack([h_re, h_im], axis=1).reshape(NC, 2, G, P).transpose(2, 0, 1, 3).reshape(G, NC, 2 * P)
    y = pl.pallas_call(
        _s5_out_kernel, grid=(G,),
        in_specs=[pl.BlockSpec((1, NC, T * H), lambda g: (g, 0, 0)),
                  pl.BlockSpec((1, NC, 2 * P), lambda g: (g, 0, 0)),
                  pl.BlockSpec((1, T * H, T * H), lambda g: (g, 0, 0)),
                  pl.BlockSpec((1, 2 * P, T * H), lambda g: (g, 0, 0)),
                  pl.BlockSpec((1, 1, T * H), lambda g: (g, 0, 0))],
        out_specs=pl.BlockSpec((1, NC, T * H), lambda g: (g, 0, 0)),
        out_shape=jax.ShapeDtypeStruct((G, NC, T * H), f32),
        compiler_params=cparams, interpret=interpret, name="s5_chunk_out",
    )(xg, hp, m, wo, dtile)
    return y.reshape(G, NC, T, H).transpose(1, 2, 0, 3).reshape(B * L, G * H)


def rms_norm(x, g):
    xf = x.astype(jnp.float32)
    y = xf * lax.rsqrt(jnp.mean(xf * xf, axis=-1, keepdims=True) + EPS)
    return (y * g.astype(jnp.float32)).astype(x.dtype)


def masked_softmax(s, mask):
    p = jax.nn.softmax(jnp.where(mask, s.astype(jnp.float32), NEG), axis=-1)
    return p * mask


def s5_mixer(u, lam_re, lam_im, log_dt, b_re, b_im, c_re, c_im, d_skip, w_glu):
    f32 = jnp.float32
    Bsz, L, _ = u.shape
    ug = u.astype(f32).reshape(Bsz, L, SSM_GROUPS, SSM_GROUP)
    dt = jnp.exp(log_dt.astype(f32))[:, None]
    lr, li = lam_re.astype(f32), lam_im.astype(f32)
    mag = jnp.exp(lr * dt)
    ab_re, ab_im = mag * jnp.cos(li * dt), mag * jnp.sin(li * dt)
    den = lr * lr + li * li
    nr = ab_re - 1.0
    cr = (nr * lr + ab_im * li) / den
    cim = (ab_im * lr - nr * li) / den
    br, bim = b_re.astype(f32), b_im.astype(f32)
    bb_re = cr[..., None] * br - cim[..., None] * bim
    bb_im = cr[..., None] * bim + cim[..., None] * br
    bu_re = jnp.einsum('blgh,gph->blgp', ug, bb_re)
    bu_im = jnp.einsum('blgh,gph->blgp', ug, bb_im)
    a_re = jnp.broadcast_to(ab_re, bu_re.shape)
    a_im = jnp.broadcast_to(ab_im, bu_im.shape)

    def combine(e1, e2):
        a1r, a1i, b1r, b1i = e1
        a2r, a2i, b2r, b2i = e2
        return (a2r * a1r - a2i * a1i, a2r * a1i + a2i * a1r,
                a2r * b1r - a2i * b1i + b2r, a2r * b1i + a2i * b1r + b2i)

    _, _, s_re, s_im = lax.associative_scan(combine, (a_re, a_im, bu_re, bu_im), axis=1)
    y = (jnp.einsum('blgp,ghp->blgh', s_re, c_re.astype(f32))
         - jnp.einsum('blgp,ghp->blgh', s_im, c_im.astype(f32))
         + d_skip.astype(f32) * ug)
    y = jax.nn.gelu(y.reshape(Bsz, L, SSM_WIDTH))
    y = y * jax.nn.sigmoid(pmm(y.reshape(Bsz * L, SSM_WIDTH), w_glu).reshape(Bsz, L, SSM_WIDTH))
    return y.astype(u.dtype)


def compress_kv(kv, pos, w1, w2):
    Bsz, L, G, hd = kv.shape
    n_cmp = (L - CMP_BLOCK) // CMP_STRIDE + 1
    idx = jnp.arange(n_cmp)[:, None] * CMP_STRIDE + jnp.arange(CMP_BLOCK)[None, :]
    blocks = kv[:, idx] + pos[:, None, :]
    blocks = jnp.swapaxes(blocks, 2, 3).reshape(Bsz, n_cmp, G, CMP_BLOCK * hd)
    return jax.nn.gelu(blocks @ w1) @ w2


def nsa_attention(q, k_cmp, v_cmp, k_slc, v_slc, k_win, v_win, gates, cmp_pos, cmp_w1, cmp_w2):
    Bsz, L = q.shape[:2]
    n_cmp = (L - CMP_BLOCK) // CMP_STRIDE + 1
    n_slc = L // SLC_BLOCK
    n_sel = min(SLC_TOPK, n_slc)
    n_chunks = L // Q_CHUNK
    scale = HEAD_DIM ** -0.5
    kc = compress_kv(k_cmp, cmp_pos[0], cmp_w1[0], cmp_w2[0])
    vc = compress_kv(v_cmp, cmp_pos[1], cmp_w1[1], cmp_w2[1])
    cmp_end = jnp.arange(n_cmp) * CMP_STRIDE + CMP_BLOCK - 1
    c_start = jnp.arange(n_cmp)[:, None] * CMP_STRIDE
    s_start = jnp.arange(n_slc)[None, :] * SLC_BLOCK
    overlap = ((c_start < s_start + SLC_BLOCK) & (c_start + CMP_BLOCK > s_start)).astype(jnp.float32)
    ks_blk = jnp.moveaxis(k_slc.reshape(Bsz, n_slc, SLC_BLOCK, N_KV, HEAD_DIM), 3, 1)
    vs_blk = jnp.moveaxis(v_slc.reshape(Bsz, n_slc, SLC_BLOCK, N_KV, HEAD_DIM), 3, 1)
    kw = jnp.pad(k_win, ((0, 0), (WINDOW, 0), (0, 0), (0, 0)))
    vw = jnp.pad(v_win, ((0, 0), (WINDOW, 0), (0, 0), (0, 0)))
    bi = jnp.arange(Bsz)[:, None, None, None]
    gi = jnp.arange(N_KV)[None, :, None, None]
    blk = jnp.arange(n_slc)

    def chunk(args):
        c_idx, qc, gc = args
        t = c_idx * Q_CHUNK + jnp.arange(Q_CHUNK)
        s = jnp.einsum('bqghd,bngd->bqghn', qc, kc) * scale
        m = cmp_end[None, :] <= t[:, None]
        p_cmp = masked_softmax(s, m[None, :, None, None, :])
        o_cmp = jnp.einsum('bqghn,bngd->bqghd', p_cmp.astype(vc.dtype), vc)
        imp = jnp.einsum('bqghn,ns->bqgs', p_cmp, overlap)
        cur = t[:, None] // SLC_BLOCK
        forced = (blk[None, :] == 0) | (blk[None, :] == cur) | (blk[None, :] == cur - 1)
        future = blk[None, :] * SLC_BLOCK > t[:, None]
        imp = jnp.where(forced[None, :, None, :], FORCED_SCORE, imp)
        imp = jnp.where(future[None, :, None, :], -1.0, imp)
        _, sel = lax.top_k(imp, n_sel)
        sel = jnp.transpose(sel, (0, 2, 1, 3))
        ks = ks_blk[bi, gi, sel].reshape(Bsz, N_KV, Q_CHUNK, n_sel * SLC_BLOCK, HEAD_DIM)
        vs = vs_blk[bi, gi, sel].reshape(Bsz, N_KV, Q_CHUNK, n_sel * SLC_BLOCK, HEAD_DIM)
        kpos = (sel[..., None] * SLC_BLOCK + jnp.arange(SLC_BLOCK)).reshape(Bsz, N_KV, Q_CHUNK, n_sel * SLC_BLOCK)
        m = jnp.transpose(kpos <= t[None, None, :, None], (0, 2, 1, 3))[:, :, :, None, :]
        s = jnp.einsum('bqghd,bgqkd->bqghk', qc, ks) * scale
        p = masked_softmax(s, m)
        o_slc = jnp.einsum('bqghk,bgqkd->bqghd', p.astype(vs.dtype), vs)
        start = c_idx * Q_CHUNK
        kwc = lax.dynamic_slice_in_dim(kw, start, WINDOW + Q_CHUNK, axis=1)
        vwc = lax.dynamic_slice_in_dim(vw, start, WINDOW + Q_CHUNK, axis=1)
        wpos = start - WINDOW + jnp.arange(WINDOW + Q_CHUNK)
        d = t[:, None] - wpos[None, :]
        m = (d >= 0) & (d < WINDOW) & (wpos[None, :] >= 0)
        s = jnp.einsum('bqghd,bkgd->bqghk', qc, kwc) * scale
        p = masked_softmax(s, m[None, :, None, None, :])
        o_win = jnp.einsum('bqghk,bkgd->bqghd', p.astype(vwc.dtype), vwc)
        return (gc[:, :, 0, :, :, None] * o_cmp + gc[:, :, 1, :, :, None] * o_slc
                + gc[:, :, 2, :, :, None] * o_win)

    q_ch = jnp.moveaxis(q.reshape(Bsz, n_chunks, Q_CHUNK, N_KV, HPG, HEAD_DIM), 1, 0)
    g_ch = jnp.moveaxis(gates.reshape(Bsz, n_chunks, Q_CHUNK, 3, N_KV, HPG), 1, 0)
    o = lax.map(chunk, (jnp.arange(n_chunks), q_ch, g_ch))
    return jnp.moveaxis(o, 0, 1).reshape(Bsz, L, ATTN_WIDTH)


def token_mixer(h, w_in, ssm_lam_re, ssm_lam_im, ssm_log_dt, ssm_b_re, ssm_b_im, ssm_c_re, ssm_c_im,
                ssm_d, ssm_w_glu, ssm_w_o, nsa_cmp_pos, nsa_cmp_w1, nsa_cmp_w2, nsa_w_o, w_out):
    Bsz, L, D = h.shape
    N = Bsz * L
    o1 = ATTN_WIDTH
    o2 = o1 + 6 * KV_WIDTH
    o3 = o2 + 3 * N_HEADS
    o4 = o3 + SSM_WIDTH
    h2 = h.reshape(N, D)
    q = pmm(h2, w_in[:, :o1]).reshape(Bsz, L, -1)
    kv = pmm(h2, w_in[:, o1:o2]).reshape(Bsz, L, -1)
    wg = jnp.pad(w_in[:, o2:o3], ((0, 0), (0, 128 - 3 * N_HEADS)))
    nsa_g = pmm(h2, wg)[:, :3 * N_HEADS].reshape(Bsz, L, -1)
    u = pmm(h2, w_in[:, o3:o4]).reshape(Bsz, L, -1)
    merge_g = pmm(h2, w_in[:, o4:]).reshape(Bsz, L, -1)
    kvs = [t.reshape(N, KV_WIDTH) for t in jnp.split(kv, 6, axis=-1)]
    attn = nsa_attention_pallas(q.reshape(N, ATTN_WIDTH), kvs, jax.nn.sigmoid(nsa_g).reshape(N, 3 * N_HEADS),
                                nsa_cmp_pos, nsa_cmp_w1, nsa_cmp_w2, Bsz, L)
    y = s5_scan_pallas(u.reshape(N, SSM_WIDTH), ssm_lam_re, ssm_lam_im, ssm_log_dt, ssm_b_re, ssm_b_im,
                       ssm_c_re, ssm_c_im, ssm_d, Bsz, L)
    y = jax.nn.gelu(y)
    ssm = (y * jax.nn.sigmoid(pmm(y, ssm_w_glu))).reshape(Bsz, L, SSM_WIDTH)
    g_attn, g_ssm = jnp.split(jax.nn.sigmoid(merge_g), 2, axis=-1)
    merged = (g_attn * pmm(attn.reshape(N, -1), nsa_w_o).reshape(Bsz, L, D)
              + g_ssm * pmm(ssm.reshape(N, -1), ssm_w_o).reshape(Bsz, L, D))
    return pmm(merged.reshape(N, D), w_out).reshape(Bsz, L, D)


def hier_moe(h, w_group, b_group, w_expert, b_expert, w_gate, w_up, w_down):
    Bsz, L, D = h.shape
    xt = h.reshape(-1, D)
    N = xt.shape[0]
    g_prob = jax.nn.softmax((xt @ w_group + b_group).astype(jnp.float32), axis=-1)
    g_p, g_idx = lax.top_k(g_prob, 1)
    e_logits = (xt @ w_expert + b_expert).astype(jnp.float32).reshape(N, N_GROUPS, EXP_PER_GROUP)
    e_logits = jnp.take_along_axis(e_logits, jnp.broadcast_to(g_idx[:, :, None], (N, 1, EXP_PER_GROUP)), axis=1)[:, 0]
    e_p, e_idx = lax.top_k(jax.nn.softmax(e_logits, axis=-1), TOPK_IN_GROUP)
    w = g_p * e_p / jnp.sum(e_p, axis=-1, keepdims=True)
    eid = g_idx * EXP_PER_GROUP + e_idx
    A = N * TOPK_IN_GROUP
    flat_e = eid.reshape(-1)
    flat_t = jnp.repeat(jnp.arange(N, dtype=jnp.int32), TOPK_IN_GROUP)
    flat_w = w.reshape(-1)
    order = jnp.argsort(flat_e)
    se = flat_e[order]
    counts = jnp.zeros(N_EXPERTS, jnp.int32).at[flat_e].add(1)
    starts = jnp.cumsum(counts) - counts
    pcounts = (counts + MOE_BLOCK - 1) // MOE_BLOCK * MOE_BLOCK
    pends = jnp.cumsum(pcounts)
    pstarts = pends - pcounts
    dest = pstarts[se] + jnp.arange(A) - starts[se]
    P = A + N_EXPERTS * MOE_BLOCK
    n_blk = P // MOE_BLOCK
    buf_t = jnp.zeros(P, jnp.int32).at[dest].set(flat_t[order])
    buf_w = jnp.zeros(P, jnp.float32).at[dest].set(flat_w[order])
    blk_e = jnp.minimum(jnp.searchsorted(pends, jnp.arange(n_blk) * MOE_BLOCK, side='right'), N_EXPERTS - 1)
    xs = xt[buf_t].reshape(n_blk, MOE_BLOCK, D)

    def expert_block(args):
        xb, e = args
        return (jax.nn.silu(xb @ w_gate[e]) * (xb @ w_up[e])) @ w_down[e]

    ys = lax.map(expert_block, (xs, blk_e)).reshape(P, D)
    out = jnp.zeros((N, D), jnp.float32).at[buf_t].add(ys.astype(jnp.float32) * buf_w[:, None])
    return out.astype(h.dtype).reshape(Bsz, L, D)


def kernel(x, c, ada_w, ada_b, norm1_g, w_in, ssm_lam_re, ssm_lam_im, ssm_log_dt, ssm_b_re, ssm_b_im, ssm_c_re, ssm_c_im, ssm_d, ssm_w_glu, ssm_w_o, nsa_cmp_pos, nsa_cmp_w1, nsa_cmp_w2, nsa_w_o, w_out, norm2_g, moe_w_group, moe_b_group, moe_w_expert, moe_b_expert, moe_w_gate, moe_w_up, moe_w_down, final_g):
    cs = jax.nn.silu(c)
    for l in range(DEPTH):
        mod = cs @ ada_w[l] + ada_b[l]
        sh1, sc1, g1, sh2, sc2, g2 = jnp.split(mod[:, None, :], 6, axis=-1)
        h = rms_norm(x, norm1_g[l]) * (1.0 + sc1) + sh1
        x = x + g1 * token_mixer(h, w_in[l], ssm_lam_re[l], ssm_lam_im[l], ssm_log_dt[l], ssm_b_re[l],
                                 ssm_b_im[l], ssm_c_re[l], ssm_c_im[l], ssm_d[l], ssm_w_glu[l], ssm_w_o[l],
                                 nsa_cmp_pos[l], nsa_cmp_w1[l], nsa_cmp_w2[l], nsa_w_o[l], w_out[l])
        h = rms_norm(x, norm2_g[l]) * (1.0 + sc2) + sh2
        x = x + g2 * hier_moe(h, moe_w_group[l], moe_b_group[l], moe_w_expert[l], moe_b_expert[l],
                              moe_w_gate[l], moe_w_up[l], moe_w_down[l])
    return rms_norm(x, final_g)
```

```python
import functools
import math

import jax
import jax.numpy as jnp
from jax import lax
from jax.experimental import pallas as pl
from jax.experimental.pallas import tpu as pltpu

D_MODEL = 1024
DEPTH = 2
SSM_GROUP = 16
SSM_STATE = 64
SSM_WIDTH = D_MODEL // 2
SSM_GROUPS = SSM_WIDTH // SSM_GROUP
HEAD_DIM = 64
N_HEADS = D_MODEL // HEAD_DIM
N_KV = N_HEADS // 4
HPG = N_HEADS // N_KV
ATTN_WIDTH = N_HEADS * HEAD_DIM
KV_WIDTH = N_KV * HEAD_DIM
CMP_BLOCK = 32
CMP_STRIDE = 16
CMP_HIDDEN = 4 * HEAD_DIM
SLC_BLOCK = 64
SLC_TOPK = 16
WINDOW = 512
Q_CHUNK = 64
FORCED_SCORE = 1e4
N_GROUPS = 4
EXP_PER_GROUP = 8
N_EXPERTS = N_GROUPS * EXP_PER_GROUP
TOPK_IN_GROUP = 2
D_EXPERT = D_MODEL // 4
MOE_BLOCK = 128
EPS = 1e-6
NEG = -1e30

VMEM_LIMIT_BYTES = 48 * 1024 * 1024


def _mm_kernel(a_ref, b_ref, o_ref):
    a = a_ref[...].astype(jnp.bfloat16)
    b = b_ref[...].astype(jnp.bfloat16)
    o_ref[...] = jnp.dot(a, b, preferred_element_type=jnp.float32).astype(o_ref.dtype)


def pmm(a, b, tm=512, tn=512, out_dtype=jnp.float32, interpret=False):
    M, K = a.shape
    _, N = b.shape
    tm = min(tm, M)
    tn = min(tn, N)
    assert M % tm == 0 and N % tn == 0, (M, N, tm, tn)
    return pl.pallas_call(
        _mm_kernel,
        interpret=interpret,
        grid=(M // tm, N // tn),
        in_specs=[pl.BlockSpec((tm, K), lambda i, j: (i, 0)),
                  pl.BlockSpec((K, tn), lambda i, j: (0, j))],
        out_specs=pl.BlockSpec((tm, tn), lambda i, j: (i, j)),
        out_shape=jax.ShapeDtypeStruct((M, N), out_dtype),
        compiler_params=pltpu.CompilerParams(
            dimension_semantics=("parallel", "parallel"),
            vmem_limit_bytes=VMEM_LIMIT_BYTES),
    )(a, b)


MASK_FILL = -2e30
MAX_FLOOR = -1e30
KEY_STEP = 256
BIG_KEY_STEP = 1024
WIN_SPAN = WINDOW + 2 * Q_CHUNK
LANES = 128


def _softmax_partial(st, vT, vis):
    st = jnp.where(vis, st, MASK_FILL)
    m = jnp.maximum(jnp.max(st, axis=0, keepdims=True), MAX_FLOOR)
    p = jnp.exp(st - m)
    l = jnp.sum(p, axis=0, keepdims=True)
    acc = jnp.dot(vT, p.astype(jnp.bfloat16), preferred_element_type=jnp.float32)
    return m, l, acc


def _softmax_merge(parts):
    m = functools.reduce(jnp.maximum, [p[0] for p in parts])
    l = 0.0
    acc = 0.0
    for m_j, l_j, acc_j in parts:
        w = jnp.exp(m_j - m)
        l = l + w * l_j
        acc = acc + w * acc_j
    return m, l, acc


def _nsa_kernel(qT_ref, kc_ref, vcT_ref, ks_ref, vsT_ref, kw_ref, vwT_ref, g_ref, ovT_ref, o_ref, sel_ref,
                *, n_cmp, n_slc, n_sel):
    f32 = jnp.float32
    bf16 = jnp.bfloat16
    c = pl.program_id(2)
    q = qT_ref[0, 0, 0]
    ncols = q.shape[1]
    ncp = kc_ref.shape[2]
    lane = lax.broadcasted_iota(jnp.int32, (1, ncols), 1)
    tpos = c * Q_CHUNK + (lane & (Q_CHUNK - 1))

    s = jnp.dot(kc_ref[0, 0], q, preferred_element_type=f32)
    n_idx = lax.broadcasted_iota(jnp.int32, (ncp, 1), 0)
    vis = (n_idx * CMP_STRIDE + (CMP_BLOCK - 1) <= tpos) & (n_idx < n_cmp)
    s = jnp.where(vis, s, MASK_FILL)
    m = jnp.max(s, axis=0, keepdims=True)
    p = jnp.where(vis, jnp.exp(s - m), 0.0)
    l = jnp.sum(p, axis=0, keepdims=True)
    p = p / jnp.where(l > 0.0, l, 1.0)
    p_hi = p.astype(bf16)
    o_cmp = jnp.dot(vcT_ref[0, 0], p_hi, preferred_element_type=f32)

    p_lo = (p - p_hi.astype(f32)).astype(bf16)
    ov = ovT_ref[...]
    imp4 = (jnp.dot(ov, p_hi, preferred_element_type=f32)
            + jnp.dot(ov, p_lo, preferred_element_type=f32))
    z = imp4[:, :LANES] + imp4[:, LANES:]
    imp = z + pltpu.roll(z, Q_CHUNK, axis=1)

    jidx = lax.broadcasted_iota(jnp.int32, (n_slc, LANES), 0)
    forced = (jidx == 0) | (jidx == c) | (jidx == c - 1)
    work = jnp.where(forced, FORCED_SCORE, imp)
    work = jnp.where(jidx > c, -1.0, work)

    def pick(_, carry):
        work, sel = carry
        mx = jnp.max(work, axis=0, keepdims=True)
        first = jnp.min(jnp.where(work == mx, jidx, n_slc), axis=0, keepdims=True)
        hit = jidx == first
        return jnp.where(hit, -2.0, work), jnp.where(hit, 1.0, sel)

    _, sel = lax.fori_loop(0, n_sel, pick, (work, jnp.zeros((n_slc, LANES), f32)))
    sel_ref[...] = jnp.concatenate([sel, sel], axis=1)

    init = (jnp.full((1, ncols), MAX_FLOOR, f32), jnp.zeros((1, ncols), f32),
            jnp.zeros((HEAD_DIM, ncols), f32))

    def slc_step(step, nkeys, carry, causal):
        off = pl.multiple_of(step * nkeys, nkeys)
        blk0 = step * (nkeys // SLC_BLOCK)
        st = jnp.dot(ks_ref[0, 0, pl.ds(off, nkeys), :], q, preferred_element_type=f32)
        vis = jnp.concatenate(
            [jnp.broadcast_to(sel_ref[pl.ds(blk0 + i, 1), :] > 0.5, (SLC_BLOCK, ncols))
             for i in range(nkeys // SLC_BLOCK)], axis=0)
        if causal:
            kpos = off + lax.broadcasted_iota(jnp.int32, (nkeys, 1), 0)
            vis = vis & (kpos <= tpos)
        return _softmax_merge([carry, _softmax_partial(st, vsT_ref[0, 0, :, pl.ds(off, nkeys)], vis)])

    n_small = c // (KEY_STEP // SLC_BLOCK)
    n_big = c // (BIG_KEY_STEP // SLC_BLOCK)
    small_per_big = BIG_KEY_STEP // KEY_STEP
    carry = lax.fori_loop(0, n_big, lambda i, cr: slc_step(i, BIG_KEY_STEP, cr, False), init)
    carry = lax.fori_loop(n_big * small_per_big, n_small, lambda i, cr: slc_step(i, KEY_STEP, cr, False), carry)
    _, l_s, acc_s = slc_step(n_small, KEY_STEP, carry, True)
    o_slc = acc_s / l_s

    base = pl.multiple_of((c // 2) * LANES, LANES)
    st = jnp.dot(kw_ref[0, 0, pl.ds(base, WIN_SPAN), :], q, preferred_element_type=f32)
    kpos = base - WINDOW + lax.broadcasted_iota(jnp.int32, (WIN_SPAN, 1), 0)
    d = tpos - kpos
    st = jnp.where((d >= 0) & (d < WINDOW) & (kpos >= 0), st, MASK_FILL)
    p_w = jnp.exp(st - jnp.max(st, axis=0, keepdims=True))
    l_w = jnp.sum(p_w, axis=0, keepdims=True)
    acc_w = jnp.dot(vwT_ref[0, 0, :, pl.ds(base, WIN_SPAN)], p_w.astype(bf16), preferred_element_type=f32)
    o_win = acc_w / l_w

    g = g_ref[0, 0, 0]
    o = g[0:1] * o_cmp + g[1:2] * o_slc + g[2:3] * o_win
    o_ref[0, 0, 0] = o.astype(o_ref.dtype)


def nsa_attention_pallas(q2, kvs, gates2, cmp_pos, cmp_w1, cmp_w2, B, L, interpret=False):
    f32, bf16 = jnp.float32, jnp.bfloat16
    G, H, hd = N_KV, HPG, HEAD_DIM
    C = L // Q_CHUNK
    n16 = L // CMP_STRIDE
    n_cmp = (L - CMP_BLOCK) // CMP_STRIDE + 1
    ncp = -(-n_cmp // LANES) * LANES
    n_slc = L // SLC_BLOCK
    n_sel = min(SLC_TOPK, n_slc)
    k_cmp, v_cmp, k_slc, v_slc, k_win, v_win = kvs

    def compress(kv2, pos, w1, w2):
        x = kv2.reshape(B, n16, CMP_STRIDE, G, hd).transpose(0, 3, 1, 2, 4).reshape(B * G * n16, CMP_STRIDE * hd)
        half = CMP_STRIDE * hd
        wcat = jnp.concatenate([w1[:half], w1[half:]], axis=1)
        ab = pmm(x, wcat, interpret=interpret).reshape(B, G, n16, 2 * CMP_HIDDEN)
        posb = pmm(jnp.pad(pos.reshape(1, -1), ((0, 7), (0, 0))), w1, interpret=interpret)[0]
        hid = ab[:, :, :-1, :CMP_HIDDEN] + ab[:, :, 1:, CMP_HIDDEN:] + posb
        hid = jnp.pad(jax.nn.gelu(hid), ((0, 0), (0, 0), (0, ncp - n_cmp), (0, 0)))
        out = pmm(hid.reshape(B * G * ncp, CMP_HIDDEN), w2, interpret=interpret)
        return out.reshape(B, G, ncp, hd)

    kc = compress(k_cmp, cmp_pos[0], cmp_w1[0], cmp_w2[0]).astype(bf16)
    vcT = compress(v_cmp, cmp_pos[1], cmp_w1[1], cmp_w2[1]).astype(bf16).transpose(0, 1, 3, 2)

    def key_major(a):
        return a.astype(bf16).reshape(B, L, G, hd).transpose(0, 2, 1, 3)

    def dim_major(a):
        return a.astype(bf16).reshape(B, L, G, hd).transpose(0, 2, 3, 1)

    ks = key_major(k_slc)
    vsT = dim_major(v_slc)
    kw = jnp.pad(key_major(k_win), ((0, 0), (0, 0), (WINDOW, 0), (0, 0)))
    vwT = jnp.pad(dim_major(v_win), ((0, 0), (0, 0), (0, 0), (WINDOW, 0)))
    scale = HEAD_DIM ** -0.5
    qT = (q2.astype(f32) * scale).astype(bf16).reshape(B, C, Q_CHUNK, G, H, hd)
    qT = qT.transpose(0, 3, 1, 5, 4, 2).reshape(B, G, C, hd, H * Q_CHUNK)
    gT = gates2.astype(f32).reshape(B, C, Q_CHUNK, 3, G, H).transpose(0, 4, 1, 3, 5, 2).reshape(B, G, C, 3, H * Q_CHUNK)
    c_start = jnp.arange(ncp)[None, :] * CMP_STRIDE
    s_start = jnp.arange(n_slc)[:, None] * SLC_BLOCK
    ovT = ((c_start < s_start + SLC_BLOCK) & (c_start + CMP_BLOCK > s_start)
           & (jnp.arange(ncp)[None, :] < n_cmp)).astype(bf16)

    ncols = H * Q_CHUNK
    bg = lambda b, g, c: (b, g, 0, 0)
    bgc = lambda b, g, c: (b, g, c, 0, 0)
    oT = pl.pallas_call(
        functools.partial(_nsa_kernel, n_cmp=n_cmp, n_slc=n_slc, n_sel=n_sel),
        grid=(B, G, C),
        in_specs=[pl.BlockSpec((1, 1, 1, hd, ncols), bgc),
                  pl.BlockSpec((1, 1, ncp, hd), bg),
                  pl.BlockSpec((1, 1, hd, ncp), bg),
                  pl.BlockSpec((1, 1, L, hd), bg),
                  pl.BlockSpec((1, 1, hd, L), bg),
                  pl.BlockSpec((1, 1, L + WINDOW, hd), bg),
                  pl.BlockSpec((1, 1, hd, L + WINDOW), bg),
                  pl.BlockSpec((1, 1, 1, 3, ncols), bgc),
                  pl.BlockSpec((n_slc, ncp), lambda b, g, c: (0, 0))],
        out_specs=pl.BlockSpec((1, 1, 1, hd, ncols), bgc),
        out_shape=jax.ShapeDtypeStruct((B, G, C, hd, ncols), bf16),
        scratch_shapes=[pltpu.VMEM((n_slc, ncols), f32)],
        compiler_params=pltpu.CompilerParams(
            dimension_semantics=("parallel", "parallel", "arbitrary"),
            vmem_limit_bytes=VMEM_LIMIT_BYTES),
        interpret=interpret,
        name="nsa_attention",
    )(qT, kc, vcT, ks, vsT, kw, vwT, gT, ovT)
    o = oT.reshape(B, G, C, hd, H, Q_CHUNK).transpose(0, 2, 5, 1, 4, 3)
    return o.reshape(B * L, G * H * hd)


S5_CHUNK = 32


def _s5_state_kernel(x_ref, ws_ref, s_ref):
    s_ref[0] = jnp.dot(x_ref[0].astype(jnp.bfloat16), ws_ref[0], preferred_element_type=jnp.float32)


def _s5_scan_kernel(sre_ref, sim_ref, are_ref, aim_ref, hre_ref, him_ref, *, chunks_per_batch):
    a_re = are_ref[...]
    a_im = aim_ref[...]
    n_batch = sre_ref.shape[0] // chunks_per_batch

    def step(c, carry):
        new = []
        for b in range(n_batch):
            h_re, h_im = carry[b]
            row = b * chunks_per_batch + c
            hre_ref[row] = h_re
            him_ref[row] = h_im
            new.append((a_re * h_re - a_im * h_im + sre_ref[row],
                        a_re * h_im + a_im * h_re + sim_ref[row]))
        return tuple(new)

    zero = jnp.zeros(a_re.shape, jnp.float32)
    lax.fori_loop(0, chunks_per_batch, step, tuple((zero, zero) for _ in range(n_batch)))


def _s5_out_kernel(x_ref, hp_ref, m_ref, wo_ref, d_ref, y_ref):
    x = x_ref[0]
    y = jnp.dot(x.astype(jnp.bfloat16), m_ref[0], preferred_element_type=jnp.float32)
    y = y + jnp.dot(hp_ref[0].astype(jnp.bfloat16), wo_ref[0], preferred_element_type=jnp.float32)
    y_ref[0] = y + x * d_ref[0]


def s5_scan_pallas(u2, lam_re, lam_im, log_dt, b_re, b_im, c_re, c_im, d_skip, B, L, interpret=False):
    f32, bf16 = jnp.float32, jnp.bfloat16
    G, P, H, T = SSM_GROUPS, SSM_STATE, SSM_GROUP, S5_CHUNK
    hi = lax.Precision.HIGHEST
    NC = B * L // T
    dt = jnp.exp(log_dt.astype(f32))[:, None]
    lr, li = lam_re.astype(f32), lam_im.astype(f32)
    mag = jnp.exp(lr * dt)
    ab_re, ab_im = mag * jnp.cos(li * dt), mag * jnp.sin(li * dt)
    den = lr * lr + li * li
    nr = ab_re - 1.0
    cr = (nr * lr + ab_im * li) / den
    cim = (ab_im * lr - nr * li) / den
    br, bim = b_re.astype(f32), b_im.astype(f32)
    bb_re = cr[..., None] * br - cim[..., None] * bim
    bb_im = cr[..., None] * bim + cim[..., None] * br
    k = jnp.arange(T + 1, dtype=f32)[:, None, None]
    pw_mag = jnp.exp(k * (lr * dt))
    pw_re = pw_mag * jnp.cos(k * (li * dt))
    pw_im = pw_mag * jnp.sin(k * (li * dt))
    cre, cimg = c_re.astype(f32), c_im.astype(f32)
    ab_r = pw_re[:T, :, :, None] * bb_re - pw_im[:T, :, :, None] * bb_im
    ab_i = pw_re[:T, :, :, None] * bb_im + pw_im[:T, :, :, None] * bb_re
    kern = (jnp.einsum('gop,tgpi->tgoi', cre, ab_r, precision=hi)
            - jnp.einsum('gop,tgpi->tgoi', cimg, ab_i, precision=hi))
    tt = jnp.arange(T)
    lag = tt[None, :] - tt[:, None]
    m = jnp.where((lag >= 0)[:, :, None, None, None], kern[jnp.clip(lag, 0, T - 1)], 0.0)
    m = m.transpose(2, 0, 4, 1, 3).reshape(G, T * H, T * H).astype(bf16)
    ws_re = ab_r[::-1].transpose(1, 0, 3, 2).reshape(G, T * H, P)
    ws_im = ab_i[::-1].transpose(1, 0, 3, 2).reshape(G, T * H, P)
    ws = jnp.concatenate([ws_re, ws_im], axis=-1).astype(bf16)
    ca_re = cre[None] * pw_re[1:, :, None, :] - cimg[None] * pw_im[1:, :, None, :]
    ca_im = cre[None] * pw_im[1:, :, None, :] + cimg[None] * pw_re[1:, :, None, :]
    wo = jnp.concatenate([ca_re.transpose(1, 3, 0, 2).reshape(G, P, T * H),
                          -ca_im.transpose(1, 3, 0, 2).reshape(G, P, T * H)], axis=1).astype(bf16)
    dtile = jnp.tile(d_skip.astype(f32), (1, T)).reshape(G, 1, T * H)

    xg = u2.astype(f32).reshape(NC, T, G, H).transpose(2, 0, 1, 3).reshape(G, NC, T * H)
    cparams = pltpu.CompilerParams(dimension_semantics=("parallel",), vmem_limit_bytes=VMEM_LIMIT_BYTES)
    s = pl.pallas_call(
        _s5_state_kernel, grid=(G,),
        in_specs=[pl.BlockSpec((1, NC, T * H), lambda g: (g, 0, 0)),
                  pl.BlockSpec((1, T * H, 2 * P), lambda g: (g, 0, 0))],
        out_specs=pl.BlockSpec((1, NC, 2 * P), lambda g: (g, 0, 0)),
        out_shape=jax.ShapeDtypeStruct((G, NC, 2 * P), f32),
        compiler_params=cparams, interpret=interpret, name="s5_chunk_state",
    )(xg, ws)
    sp = s.reshape(G, NC, 2, P).transpose(1, 2, 0, 3).reshape(NC, 2, G // 2, 2 * P)
    a_re = pw_re[T].reshape(G // 2, 2 * P)
    a_im = pw_im[T].reshape(G // 2, 2 * P)
    full = lambda shape: pl.BlockSpec(shape, lambda i: (0,) * len(shape))
    st_shape = (NC, G // 2, 2 * P)
    h_re, h_im = pl.pallas_call(
        functools.partial(_s5_scan_kernel, chunks_per_batch=L // T), grid=(1,),
        in_specs=[full(st_shape), full(st_shape), full((G // 2, 2 * P)), full((G // 2, 2 * P))],
        out_specs=[full(st_shape), full(st_shape)],
        out_shape=[jax.ShapeDtypeStruct(st_shape, f32)] * 2,
        compiler_params=pltpu.CompilerParams(dimension_semantics=("arbitrary",), vmem_limit_bytes=VMEM_LIMIT_BYTES),
        interpret=interpret, name="s5_state_scan",
    )(sp[:, 0], sp[:, 1], a_re, a_im)
    hp = jnp.stack([h_re, h_im], axis=1).reshape(NC, 2, G, P).transpose(2, 0, 1, 3).reshape(G, NC, 2 * P)
    y = pl.pallas_call(
        _s5_out_kernel, grid=(G,),
        in_specs=[pl.BlockSpec((1, NC, T * H), lambda g: (g, 0, 0)),
                  pl.BlockSpec((1, NC, 2 * P), lambda g: (g, 0, 0)),
                  pl.BlockSpec((1, T * H, T * H), lambda g: (g, 0, 0)),
                  pl.BlockSpec((1, 2 * P, T * H), lambda g: (g, 0, 0)),
                  pl.BlockSpec((1, 1, T * H), lambda g: (g, 0, 0))],
        out_specs=pl.BlockSpec((1, NC, T * H), lambda g: (g, 0, 0)),
        out_shape=jax.ShapeDtypeStruct((G, NC, T * H), f32),
        compiler_params=cparams, interpret=interpret, name="s5_chunk_out",
    )(xg, hp, m, wo, dtile)
    return y.reshape(G, NC, T, H).transpose(1, 2, 0, 3).reshape(B * L, G * H)


MOE_TILE = 512
MOE_SEG = 32
MOE_SLOTS = TOPK_IN_GROUP * MOE_TILE + N_EXPERTS * MOE_SEG
ROUTE_E1, ROUTE_E2, ROUTE_W1, ROUTE_W2 = 0, 1, 2, 3


def _split_bf16(a):
    hi = a.astype(jnp.bfloat16)
    return hi, (a - hi.astype(jnp.float32)).astype(jnp.bfloat16)


def _moe_route_kernel(x_ref, g_ref, sc_ref, sh_ref, wr_ref, br_ref, h_ref, route_ref, routeT_ref, cnt_ref):
    f32 = jnp.float32
    x = x_ref[...]
    y = x * lax.rsqrt(jnp.mean(x * x, axis=-1, keepdims=True) + EPS)
    h = y * g_ref[...] * (1.0 + sc_ref[0]) + sh_ref[0]
    h_ref[...] = h.astype(h_ref.dtype)
    h_hi, h_lo = _split_bf16(h)
    w_hi, w_lo = _split_bf16(wr_ref[...])
    lg = (jnp.dot(h_hi, w_hi, preferred_element_type=f32) + jnp.dot(h_hi, w_lo, preferred_element_type=f32)
          + jnp.dot(h_lo, w_hi, preferred_element_type=f32) + br_ref[...])
    lane = lax.broadcasted_iota(jnp.int32, lg.shape, 1)
    ninf = -jnp.inf
    gmask = lane < N_GROUPS
    gmax = jnp.max(jnp.where(gmask, lg, ninf), axis=1, keepdims=True)
    gsum = jnp.sum(jnp.where(gmask, jnp.exp(lg - gmax), 0.0), axis=1, keepdims=True)
    g_idx = jnp.min(jnp.where(gmask & (lg == gmax), lane, LANES), axis=1, keepdims=True)
    g_p = 1.0 / gsum
    lo = N_GROUPS + EXP_PER_GROUP * g_idx
    emask = (lane >= lo) & (lane < lo + EXP_PER_GROUP)
    el = jnp.where(emask, lg, ninf)
    emax = jnp.max(el, axis=1, keepdims=True)
    i1 = jnp.min(jnp.where(el == emax, lane, LANES), axis=1, keepdims=True)
    el2 = jnp.where(lane == i1, ninf, el)
    emax2 = jnp.max(el2, axis=1, keepdims=True)
    i2 = jnp.min(jnp.where(el2 == emax2, lane, LANES), axis=1, keepdims=True)
    r = jnp.exp(emax2 - emax)
    w1 = g_p / (1.0 + r)
    w2 = g_p * r / (1.0 + r)
    e1 = (i1 - N_GROUPS).astype(f32)
    e2 = (i2 - N_GROUPS).astype(f32)
    route = jnp.where(lane == ROUTE_E1, e1, jnp.where(lane == ROUTE_E2, e2,
                      jnp.where(lane == ROUTE_W1, w1, jnp.where(lane == ROUTE_W2, w2, 0.0))))
    route_ref[...] = route
    routeT_ref[...] = route.T[:8, :]
    cnt = jnp.sum(jnp.where((lane == i1) | (lane == i2), 1.0, 0.0), axis=0, keepdims=True)
    cnt_ref[0] = jnp.broadcast_to(cnt, (8, LANES))


def _moe_ffn_kernel(seg_start_ref, seg_nblk_ref, h_ref, route_ref, routeT_ref, lrow_ref, lcol_ref, wgu_ref, wd_ref,
                    x_ref, gate_ref, fg_ref, o_ref, xs_ref, yw_ref, ws_ref, *, final_norm):
    f32, bf16 = jnp.float32, jnp.bfloat16
    t = pl.program_id(0)
    e = pl.program_id(1)
    tm = h_ref.shape[0]

    @pl.when(e == 0)
    def _dispatch():
        rT = routeT_ref[...]
        e1r, e2r = rT[ROUTE_E1:ROUTE_E1 + 1], rT[ROUTE_E2:ROUTE_E2 + 1]
        w1r, w2r = rT[ROUTE_W1:ROUTE_W1 + 1], rT[ROUTE_W2:ROUTE_W2 + 1]
        eio = lax.broadcasted_iota(jnp.int32, (N_EXPERTS, tm), 0).astype(f32)
        oh1 = eio == e1r
        oh2 = eio == e2r
        oh = jnp.where(oh1 | oh2, 1.0, 0.0).astype(bf16)
        before = (lax.broadcasted_iota(jnp.int32, (tm, tm), 0)
                  < lax.broadcasted_iota(jnp.int32, (tm, tm), 1))
        rank = jnp.dot(oh, jnp.where(before, 1.0, 0.0).astype(bf16), preferred_element_type=f32)
        pos = lcol_ref[0][:, :1] + rank
        d1 = jnp.sum(jnp.where(oh1, pos, 0.0), axis=0, keepdims=True)
        d2 = jnp.sum(jnp.where(oh2, pos, 0.0), axis=0, keepdims=True)
        slot = lax.broadcasted_iota(jnp.int32, (MOE_SLOTS, tm), 0).astype(f32)
        m1 = slot == d1
        m2 = slot == d2
        perm = jnp.where(m1 | m2, 1.0, 0.0).astype(bf16)
        xs_ref[...] = jnp.dot(perm, h_ref[...], preferred_element_type=f32).astype(bf16)
        ws_ref[...] = jnp.sum(jnp.where(m1, w1r, 0.0) + jnp.where(m2, w2r, 0.0), axis=1, keepdims=True)
        yw_ref[...] = jnp.zeros(yw_ref.shape, yw_ref.dtype)

    start = seg_start_ref[t * N_EXPERTS + e]
    nblk = seg_nblk_ref[t * N_EXPERTS + e]

    def block(i, _):
        r0 = pl.multiple_of(start + i * MOE_SEG, MOE_SEG)
        xb = xs_ref[pl.ds(r0, MOE_SEG), :]
        gu = jnp.dot(xb, wgu_ref[0], preferred_element_type=f32)
        a = jax.nn.silu(gu[:, :D_EXPERT]) * gu[:, D_EXPERT:]
        y = jnp.dot(a.astype(bf16), wd_ref[0], preferred_element_type=f32)
        yw_ref[pl.ds(r0, MOE_SEG), :] = (y * ws_ref[pl.ds(r0, MOE_SEG), :]).astype(bf16)
        return 0

    lax.fori_loop(0, nblk, block, 0)

    @pl.when(e == N_EXPERTS - 1)
    def _combine():
        r = route_ref[...]
        e1c, e2c = r[:, ROUTE_E1:ROUTE_E1 + 1], r[:, ROUTE_E2:ROUTE_E2 + 1]
        lane = lax.broadcasted_iota(jnp.int32, (tm, LANES), 1).astype(f32)
        oh1 = lane == e1c
        oh2 = lane == e2c
        oh = jnp.where(oh1 | oh2, 1.0, 0.0).astype(bf16)
        after = (lax.broadcasted_iota(jnp.int32, (tm, tm), 1)
                 < lax.broadcasted_iota(jnp.int32, (tm, tm), 0))
        rank = jnp.dot(jnp.where(after, 1.0, 0.0).astype(bf16), oh, preferred_element_type=f32)
        pos = lrow_ref[0][:1, :] + rank
        d1 = jnp.sum(jnp.where(oh1, pos, 0.0), axis=1, keepdims=True)
        d2 = jnp.sum(jnp.where(oh2, pos, 0.0), axis=1, keepdims=True)
        slot = lax.broadcasted_iota(jnp.int32, (tm, MOE_SLOTS), 1).astype(f32)
        comb = jnp.where((slot == d1) | (slot == d2), 1.0, 0.0).astype(bf16)
        moe = jnp.dot(comb, yw_ref[...], preferred_element_type=f32)
        xn = x_ref[...] + gate_ref[0] * moe
        if final_norm:
            xn = xn * lax.rsqrt(jnp.mean(xn * xn, axis=-1, keepdims=True) + EPS) * fg_ref[...]
        o_ref[...] = xn


def moe_layer_pallas(x2, norm_g, sc2, sh2, g2, w_group, b_group, w_expert, b_expert, w_gate, w_up, w_down,
                     final_g, final_norm, B, L, interpret=False):
    f32, bf16 = jnp.float32, jnp.bfloat16
    N, D = x2.shape
    tm = MOE_TILE
    n_tiles = N // tm
    tpb = L // tm
    wr = jnp.zeros((D, LANES), f32).at[:, :N_GROUPS].set(w_group).at[:, N_GROUPS:N_GROUPS + N_EXPERTS].set(w_expert)
    br = jnp.zeros((1, LANES), f32).at[0, :N_GROUPS].set(b_group).at[0, N_GROUPS:N_GROUPS + N_EXPERTS].set(b_expert)
    row = lambda v: v.reshape(1, D).astype(f32)
    per_batch = lambda v: v.reshape(B, 1, D).astype(f32)
    cp = pltpu.CompilerParams(dimension_semantics=("parallel",), vmem_limit_bytes=VMEM_LIMIT_BYTES)
    h, route, routeT, cnt = pl.pallas_call(
        _moe_route_kernel, grid=(n_tiles,),
        in_specs=[pl.BlockSpec((tm, D), lambda i: (i, 0)),
                  pl.BlockSpec((1, D), lambda i: (0, 0)),
                  pl.BlockSpec((1, 1, D), lambda i: (i // tpb, 0, 0)),
                  pl.BlockSpec((1, 1, D), lambda i: (i // tpb, 0, 0)),
                  pl.BlockSpec((D, LANES), lambda i: (0, 0)),
                  pl.BlockSpec((1, LANES), lambda i: (0, 0))],
        out_specs=[pl.BlockSpec((tm, D), lambda i: (i, 0)),
                   pl.BlockSpec((tm, LANES), lambda i: (i, 0)),
                   pl.BlockSpec((8, tm), lambda i: (0, i)),
                   pl.BlockSpec((1, 8, LANES), lambda i: (i, 0, 0))],
        out_shape=[jax.ShapeDtypeStruct((N, D), bf16), jax.ShapeDtypeStruct((N, LANES), f32),
                   jax.ShapeDtypeStruct((8, N), f32), jax.ShapeDtypeStruct((n_tiles, 8, LANES), f32)],
        compiler_params=cp, interpret=interpret, name="moe_route",
    )(x2, row(norm_g), per_batch(sc2), per_batch(sh2), wr, br)
    counts = cnt[:, 0, N_GROUPS:N_GROUPS + N_EXPERTS].astype(jnp.int32)
    nblk = (counts + MOE_SEG - 1) // MOE_SEG
    starts = (jnp.cumsum(nblk, axis=1) - nblk) * MOE_SEG
    starts_f = starts.astype(f32)
    lrow = jnp.zeros((n_tiles, 8, LANES), f32).at[:, :, :N_EXPERTS].set(starts_f[:, None, :])
    lcol = jnp.broadcast_to(starts_f[:, :, None], (n_tiles, N_EXPERTS, LANES))
    wgu = jnp.concatenate([w_gate, w_up], axis=-1).astype(bf16)
    wd = w_down.astype(bf16)
    tile = lambda t, e, *_: (t, 0)
    out = pl.pallas_call(
        functools.partial(_moe_ffn_kernel, final_norm=final_norm),
        grid_spec=pltpu.PrefetchScalarGridSpec(
            num_scalar_prefetch=2, grid=(n_tiles, N_EXPERTS),
            in_specs=[pl.BlockSpec((tm, D), tile),
                      pl.BlockSpec((tm, LANES), tile),
                      pl.BlockSpec((8, tm), lambda t, e, *_: (0, t)),
                      pl.BlockSpec((1, 8, LANES), lambda t, e, *_: (t, 0, 0)),
                      pl.BlockSpec((1, N_EXPERTS, LANES), lambda t, e, *_: (t, 0, 0)),
                      pl.BlockSpec((1, D, 2 * D_EXPERT), lambda t, e, *_: (e, 0, 0)),
                      pl.BlockSpec((1, D_EXPERT, D), lambda t, e, *_: (e, 0, 0)),
                      pl.BlockSpec((tm, D), tile),
                      pl.BlockSpec((1, 1, D), lambda t, e, *_: (t // tpb, 0, 0)),
                      pl.BlockSpec((1, D), lambda t, e, *_: (0, 0))],
            out_specs=pl.BlockSpec((tm, D), tile),
            scratch_shapes=[pltpu.VMEM((MOE_SLOTS, D), bf16), pltpu.VMEM((MOE_SLOTS, D), bf16),
                            pltpu.VMEM((MOE_SLOTS, 1), f32)]),
        out_shape=jax.ShapeDtypeStruct((N, D), f32),
        compiler_params=pltpu.CompilerParams(dimension_semantics=("parallel", "arbitrary"),
                                             vmem_limit_bytes=VMEM_LIMIT_BYTES),
        interpret=interpret, name="moe_ffn",
    )(starts.reshape(-1), nblk.reshape(-1), h, route, routeT, lrow, lcol, wgu, wd, x2, per_batch(g2), row(final_g))
    return out


def rms_norm(x, g):
    xf = x.astype(jnp.float32)
    y = xf * lax.rsqrt(jnp.mean(xf * xf, axis=-1, keepdims=True) + EPS)
    return (y * g.astype(jnp.float32)).astype(x.dtype)


def masked_softmax(s, mask):
    p = jax.nn.softmax(jnp.where(mask, s.astype(jnp.float32), NEG), axis=-1)
    return p * mask


def s5_mixer(u, lam_re, lam_im, log_dt, b_re, b_im, c_re, c_im, d_skip, w_glu):
    f32 = jnp.float32
    Bsz, L, _ = u.shape
    ug = u.astype(f32).reshape(Bsz, L, SSM_GROUPS, SSM_GROUP)
    dt = jnp.exp(log_dt.astype(f32))[:, None]
    lr, li = lam_re.astype(f32), lam_im.astype(f32)
    mag = jnp.exp(lr * dt)
    ab_re, ab_im = mag * jnp.cos(li * dt), mag * jnp.sin(li * dt)
    den = lr * lr + li * li
    nr = ab_re - 1.0
    cr = (nr * lr + ab_im * li) / den
    cim = (ab_im * lr - nr * li) / den
    br, bim = b_re.astype(f32), b_im.astype(f32)
    bb_re = cr[..., None] * br - cim[..., None] * bim
    bb_im = cr[..., None] * bim + cim[..., None] * br
    bu_re = jnp.einsum('blgh,gph->blgp', ug, bb_re)
    bu_im = jnp.einsum('blgh,gph->blgp', ug, bb_im)
    a_re = jnp.broadcast_to(ab_re, bu_re.shape)
    a_im = jnp.broadcast_to(ab_im, bu_im.shape)

    def combine(e1, e2):
        a1r, a1i, b1r, b1i = e1
        a2r, a2i, b2r, b2i = e2
        return (a2r * a1r - a2i * a1i, a2r * a1i + a2i * a1r,
                a2r * b1r - a2i * b1i + b2r, a2r * b1i + a2i * b1r + b2i)

    _, _, s_re, s_im = lax.associative_scan(combine, (a_re, a_im, bu_re, bu_im), axis=1)
    y = (jnp.einsum('blgp,ghp->blgh', s_re, c_re.astype(f32))
         - jnp.einsum('blgp,ghp->blgh', s_im, c_im.astype(f32))
         + d_skip.astype(f32) * ug)
    y = jax.nn.gelu(y.reshape(Bsz, L, SSM_WIDTH))
    y = y * jax.nn.sigmoid(pmm(y.reshape(Bsz * L, SSM_WIDTH), w_glu).reshape(Bsz, L, SSM_WIDTH))
    return y.astype(u.dtype)


def compress_kv(kv, pos, w1, w2):
    Bsz, L, G, hd = kv.shape
    n_cmp = (L - CMP_BLOCK) // CMP_STRIDE + 1
    idx = jnp.arange(n_cmp)[:, None] * CMP_STRIDE + jnp.arange(CMP_BLOCK)[None, :]
    blocks = kv[:, idx] + pos[:, None, :]
    blocks = jnp.swapaxes(blocks, 2, 3).reshape(Bsz, n_cmp, G, CMP_BLOCK * hd)
    return jax.nn.gelu(blocks @ w1) @ w2


def nsa_attention(q, k_cmp, v_cmp, k_slc, v_slc, k_win, v_win, gates, cmp_pos, cmp_w1, cmp_w2):
    Bsz, L = q.shape[:2]
    n_cmp = (L - CMP_BLOCK) // CMP_STRIDE + 1
    n_slc = L // SLC_BLOCK
    n_sel = min(SLC_TOPK, n_slc)
    n_chunks = L // Q_CHUNK
    scale = HEAD_DIM ** -0.5
    kc = compress_kv(k_cmp, cmp_pos[0], cmp_w1[0], cmp_w2[0])
    vc = compress_kv(v_cmp, cmp_pos[1], cmp_w1[1], cmp_w2[1])
    cmp_end = jnp.arange(n_cmp) * CMP_STRIDE + CMP_BLOCK - 1
    c_start = jnp.arange(n_cmp)[:, None] * CMP_STRIDE
    s_start = jnp.arange(n_slc)[None, :] * SLC_BLOCK
    overlap = ((c_start < s_start + SLC_BLOCK) & (c_start + CMP_BLOCK > s_start)).astype(jnp.float32)
    ks_blk = jnp.moveaxis(k_slc.reshape(Bsz, n_slc, SLC_BLOCK, N_KV, HEAD_DIM), 3, 1)
    vs_blk = jnp.moveaxis(v_slc.reshape(Bsz, n_slc, SLC_BLOCK, N_KV, HEAD_DIM), 3, 1)
    kw = jnp.pad(k_win, ((0, 0), (WINDOW, 0), (0, 0), (0, 0)))
    vw = jnp.pad(v_win, ((0, 0), (WINDOW, 0), (0, 0), (0, 0)))
    bi = jnp.arange(Bsz)[:, None, None, None]
    gi = jnp.arange(N_KV)[None, :, None, None]
    blk = jnp.arange(n_slc)

    def chunk(args):
        c_idx, qc, gc = args
        t = c_idx * Q_CHUNK + jnp.arange(Q_CHUNK)
        s = jnp.einsum('bqghd,bngd->bqghn', qc, kc) * scale
        m = cmp_end[None, :] <= t[:, None]
        p_cmp = masked_softmax(s, m[None, :, None, None, :])
        o_cmp = jnp.einsum('bqghn,bngd->bqghd', p_cmp.astype(vc.dtype), vc)
        imp = jnp.einsum('bqghn,ns->bqgs', p_cmp, overlap)
        cur = t[:, None] // SLC_BLOCK
        forced = (blk[None, :] == 0) | (blk[None, :] == cur) | (blk[None, :] == cur - 1)
        future = blk[None, :] * SLC_BLOCK > t[:, None]
        imp = jnp.where(forced[None, :, None, :], FORCED_SCORE, imp)
        imp = jnp.where(future[None, :, None, :], -1.0, imp)
        _, sel = lax.top_k(imp, n_sel)
        sel = jnp.transpose(sel, (0, 2, 1, 3))
        ks = ks_blk[bi, gi, sel].reshape(Bsz, N_KV, Q_CHUNK, n_sel * SLC_BLOCK, HEAD_DIM)
        vs = vs_blk[bi, gi, sel].reshape(Bsz, N_KV, Q_CHUNK, n_sel * SLC_BLOCK, HEAD_DIM)
        kpos = (sel[..., None] * SLC_BLOCK + jnp.arange(SLC_BLOCK)).reshape(Bsz, N_KV, Q_CHUNK, n_sel * SLC_BLOCK)
        m = jnp.transpose(kpos <= t[None, None, :, None], (0, 2, 1, 3))[:, :, :, None, :]
        s = jnp.einsum('bqghd,bgqkd->bqghk', qc, ks) * scale
        p = masked_softmax(s, m)
        o_slc = jnp.einsum('bqghk,bgqkd->bqghd', p.astype(vs.dtype), vs)
        start = c_idx * Q_CHUNK
        kwc = lax.dynamic_slice_in_dim(kw, start, WINDOW + Q_CHUNK, axis=1)
        vwc = lax.dynamic_slice_in_dim(vw, start, WINDOW + Q_CHUNK, axis=1)
        wpos = start - WINDOW + jnp.arange(WINDOW + Q_CHUNK)
        d = t[:, None] - wpos[None, :]
        m = (d >= 0) & (d < WINDOW) & (wpos[None, :] >= 0)
        s = jnp.einsum('bqghd,bkgd->bqghk', qc, kwc) * scale
        p = masked_softmax(s, m[None, :, None, None, :])
        o_win = jnp.einsum('bqghk,bkgd->bqghd', p.astype(vwc.dtype), vwc)
        return (gc[:, :, 0, :, :, None] * o_cmp + gc[:, :, 1, :, :, None] * o_slc
                + gc[:, :, 2, :, :, None] * o_win)

    q_ch = jnp.moveaxis(q.reshape(Bsz, n_chunks, Q_CHUNK, N_KV, HPG, HEAD_DIM), 1, 0)
    g_ch = jnp.moveaxis(gates.reshape(Bsz, n_chunks, Q_CHUNK, 3, N_KV, HPG), 1, 0)
    o = lax.map(chunk, (jnp.arange(n_chunks), q_ch, g_ch))
    return jnp.moveaxis(o, 0, 1).reshape(Bsz, L, ATTN_WIDTH)


def token_mixer(h, w_in, ssm_lam_re, ssm_lam_im, ssm_log_dt, ssm_b_re, ssm_b_im, ssm_c_re, ssm_c_im,
                ssm_d, ssm_w_glu, ssm_w_o, nsa_cmp_pos, nsa_cmp_w1, nsa_cmp_w2, nsa_w_o, w_out):
    Bsz, L, D = h.shape
    N = Bsz * L
    o1 = ATTN_WIDTH
    o2 = o1 + 6 * KV_WIDTH
    o3 = o2 + 3 * N_HEADS
    o4 = o3 + SSM_WIDTH
    h2 = h.reshape(N, D)
    q = pmm(h2, w_in[:, :o1]).reshape(Bsz, L, -1)
    kv = pmm(h2, w_in[:, o1:o2]).reshape(Bsz, L, -1)
    wg = jnp.pad(w_in[:, o2:o3], ((0, 0), (0, 128 - 3 * N_HEADS)))
    nsa_g = pmm(h2, wg)[:, :3 * N_HEADS].reshape(Bsz, L, -1)
    u = pmm(h2, w_in[:, o3:o4]).reshape(Bsz, L, -1)
    merge_g = pmm(h2, w_in[:, o4:]).reshape(Bsz, L, -1)
    kvs = [t.reshape(N, KV_WIDTH) for t in jnp.split(kv, 6, axis=-1)]
    attn = nsa_attention_pallas(q.reshape(N, ATTN_WIDTH), kvs, jax.nn.sigmoid(nsa_g).reshape(N, 3 * N_HEADS),
                                nsa_cmp_pos, nsa_cmp_w1, nsa_cmp_w2, Bsz, L)
    y = s5_scan_pallas(u.reshape(N, SSM_WIDTH), ssm_lam_re, ssm_lam_im, ssm_log_dt, ssm_b_re, ssm_b_im,
                       ssm_c_re, ssm_c_im, ssm_d, Bsz, L)
    y = jax.nn.gelu(y)
    ssm = (y * jax.nn.sigmoid(pmm(y, ssm_w_glu))).reshape(Bsz, L, SSM_WIDTH)
    g_attn, g_ssm = jnp.split(jax.nn.sigmoid(merge_g), 2, axis=-1)
    merged = (g_attn * pmm(attn.reshape(N, -1), nsa_w_o).reshape(Bsz, L, D)
              + g_ssm * pmm(ssm.reshape(N, -1), ssm_w_o).reshape(Bsz, L, D))
    return pmm(merged.reshape(N, D), w_out).reshape(Bsz, L, D)


def hier_moe(h, w_group, b_group, w_expert, b_expert, w_gate, w_up, w_down):
    Bsz, L, D = h.shape
    xt = h.reshape(-1, D)
    N = xt.shape[0]
    g_prob = jax.nn.softmax((xt @ w_group + b_group).astype(jnp.float32), axis=-1)
    g_p, g_idx = lax.top_k(g_prob, 1)
    e_logits = (xt @ w_expert + b_expert).astype(jnp.float32).reshape(N, N_GROUPS, EXP_PER_GROUP)
    e_logits = jnp.take_along_axis(e_logits, jnp.broadcast_to(g_idx[:, :, None], (N, 1, EXP_PER_GROUP)), axis=1)[:, 0]
    e_p, e_idx = lax.top_k(jax.nn.softmax(e_logits, axis=-1), TOPK_IN_GROUP)
    w = g_p * e_p / jnp.sum(e_p, axis=-1, keepdims=True)
    eid = g_idx * EXP_PER_GROUP + e_idx
    A = N * TOPK_IN_GROUP
    flat_e = eid.reshape(-1)
    flat_t = jnp.repeat(jnp.arange(N, dtype=jnp.int32), TOPK_IN_GROUP)
    flat_w = w.reshape(-1)
    order = jnp.argsort(flat_e)
    se = flat_e[order]
    counts = jnp.zeros(N_EXPERTS, jnp.int32).at[flat_e].add(1)
    starts = jnp.cumsum(counts) - counts
    pcounts = (counts + MOE_BLOCK - 1) // MOE_BLOCK * MOE_BLOCK
    pends = jnp.cumsum(pcounts)
    pstarts = pends - pcounts
    dest = pstarts[se] + jnp.arange(A) - starts[se]
    P = A + N_EXPERTS * MOE_BLOCK
    n_blk = P // MOE_BLOCK
    buf_t = jnp.zeros(P, jnp.int32).at[dest].set(flat_t[order])
    buf_w = jnp.zeros(P, jnp.float32).at[dest].set(flat_w[order])
    blk_e = jnp.minimum(jnp.searchsorted(pends, jnp.arange(n_blk) * MOE_BLOCK, side='right'), N_EXPERTS - 1)
    xs = xt[buf_t].reshape(n_blk, MOE_BLOCK, D)

    def expert_block(args):
        xb, e = args
        return (jax.nn.silu(xb @ w_gate[e]) * (xb @ w_up[e])) @ w_down[e]

    ys = lax.map(expert_block, (xs, blk_e)).reshape(P, D)
    out = jnp.zeros((N, D), jnp.float32).at[buf_t].add(ys.astype(jnp.float32) * buf_w[:, None])
    return out.astype(h.dtype).reshape(Bsz, L, D)


def kernel(x, c, ada_w, ada_b, norm1_g, w_in, ssm_lam_re, ssm_lam_im, ssm_log_dt, ssm_b_re, ssm_b_im, ssm_c_re, ssm_c_im, ssm_d, ssm_w_glu, ssm_w_o, nsa_cmp_pos, nsa_cmp_w1, nsa_cmp_w2, nsa_w_o, w_out, norm2_g, moe_w_group, moe_b_group, moe_w_expert, moe_b_expert, moe_w_gate, moe_w_up, moe_w_down, final_g):
    cs = jax.nn.silu(c)
    for l in range(DEPTH):
        mod = cs @ ada_w[l] + ada_b[l]
        sh1, sc1, g1, sh2, sc2, g2 = jnp.split(mod[:, None, :], 6, axis=-1)
        h = rms_norm(x, norm1_g[l]) * (1.0 + sc1) + sh1
        x = x + g1 * token_mixer(h, w_in[l], ssm_lam_re[l], ssm_lam_im[l], ssm_log_dt[l], ssm_b_re[l],
                                 ssm_b_im[l], ssm_c_re[l], ssm_c_im[l], ssm_d[l], ssm_w_glu[l], ssm_w_o[l],
                                 nsa_cmp_pos[l], nsa_cmp_w1[l], nsa_cmp_w2[l], nsa_w_o[l], w_out[l])
        Bsz, L, D = x.shape
        x = moe_layer_pallas(x.reshape(Bsz * L, D), norm2_g[l], sc2, sh2, g2, moe_w_group[l], moe_b_group[l],
                             moe_w_expert[l], moe_b_expert[l], moe_w_gate[l], moe_w_up[l], moe_w_down[l],
                             final_g, l == DEPTH - 1, Bsz, L).reshape(Bsz, L, D)
    return x
```

```python
import functools
import math

import jax
import jax.numpy as jnp
from jax import lax
from jax.experimental import pallas as pl
from jax.experimental.pallas import tpu as pltpu

D_MODEL = 1024
DEPTH = 2
SSM_GROUP = 16
SSM_STATE = 64
SSM_WIDTH = D_MODEL // 2
SSM_GROUPS = SSM_WIDTH // SSM_GROUP
HEAD_DIM = 64
N_HEADS = D_MODEL // HEAD_DIM
N_KV = N_HEADS // 4
HPG = N_HEADS // N_KV
ATTN_WIDTH = N_HEADS * HEAD_DIM
KV_WIDTH = N_KV * HEAD_DIM
CMP_BLOCK = 32
CMP_STRIDE = 16
CMP_HIDDEN = 4 * HEAD_DIM
SLC_BLOCK = 64
SLC_TOPK = 16
WINDOW = 512
Q_CHUNK = 64
FORCED_SCORE = 1e4
N_GROUPS = 4
EXP_PER_GROUP = 8
N_EXPERTS = N_GROUPS * EXP_PER_GROUP
TOPK_IN_GROUP = 2
D_EXPERT = D_MODEL // 4
MOE_BLOCK = 128
EPS = 1e-6
NEG = -1e30

VMEM_LIMIT_BYTES = 48 * 1024 * 1024


def _mm_kernel(a_ref, b_ref, o_ref):
    a = a_ref[...].astype(jnp.bfloat16)
    b = b_ref[...].astype(jnp.bfloat16)
    o_ref[...] = jnp.dot(a, b, preferred_element_type=jnp.float32).astype(o_ref.dtype)


def pmm(a, b, tm=512, tn=512, out_dtype=jnp.float32, interpret=False):
    M, K = a.shape
    _, N = b.shape
    tm = min(tm, M)
    tn = min(tn, N)
    assert M % tm == 0 and N % tn == 0, (M, N, tm, tn)
    return pl.pallas_call(
        _mm_kernel,
        interpret=interpret,
        grid=(M // tm, N // tn),
        in_specs=[pl.BlockSpec((tm, K), lambda i, j: (i, 0)),
                  pl.BlockSpec((K, tn), lambda i, j: (0, j))],
        out_specs=pl.BlockSpec((tm, tn), lambda i, j: (i, j)),
        out_shape=jax.ShapeDtypeStruct((M, N), out_dtype),
        compiler_params=pltpu.CompilerParams(
            dimension_semantics=("parallel", "parallel"),
            vmem_limit_bytes=VMEM_LIMIT_BYTES),
    )(a, b)


MASK_FILL = -1e30
MAX_FLOOR = -1e29
SLC_STEP = 1024
SLC_STEP_BLOCKS = SLC_STEP // SLC_BLOCK
WIN_SPAN = WINDOW + 2 * Q_CHUNK
LANES = 128
V_ROWS = HEAD_DIM + 16


def _nsa_kernel(qT_ref, kc_ref, vcT_ref, ks_ref, vsT_ref, kw_ref, vwT_ref, g_ref, ovT_ref, o_ref, bias_ref, st_ref,
                *, n_cmp, n_slc, n_sel):
    f32 = jnp.float32
    bf16 = jnp.bfloat16
    c = pl.program_id(2)
    q = qT_ref[0, 0, 0]
    ncols = q.shape[1]
    ncp = kc_ref.shape[2]
    lane = lax.broadcasted_iota(jnp.int32, (1, ncols), 1)
    tpos = c * Q_CHUNK + (lane & (Q_CHUNK - 1))

    s = jnp.dot(kc_ref[0, 0], q, preferred_element_type=f32)
    n_idx = lax.broadcasted_iota(jnp.int32, (ncp, 1), 0)
    vis = (n_idx * CMP_STRIDE + (CMP_BLOCK - 1) <= tpos) & (n_idx < n_cmp)
    s = jnp.where(vis, s, MASK_FILL)
    m = jnp.max(s, axis=0, keepdims=True)
    p = jnp.where(vis, jnp.exp(s - m), 0.0)
    l = jnp.sum(p, axis=0, keepdims=True)
    p = p / jnp.where(l > 0.0, l, 1.0)
    p_hi = p.astype(bf16)
    o_cmp = jnp.dot(vcT_ref[0, 0], p_hi, preferred_element_type=f32)

    p_lo = (p - p_hi.astype(f32)).astype(bf16)
    ov = ovT_ref[...]
    imp4 = (jnp.dot(ov, p_hi, preferred_element_type=f32)
            + jnp.dot(ov, p_lo, preferred_element_type=f32))
    z = imp4[:, :LANES] + imp4[:, LANES:]
    imp = z + pltpu.roll(z, Q_CHUNK, axis=1)

    base = pl.multiple_of((c // 2) * LANES, LANES)
    st = jnp.dot(kw_ref[0, 0, pl.ds(base, WIN_SPAN), :], q, preferred_element_type=f32)
    kpos = base - WINDOW + lax.broadcasted_iota(jnp.int32, (WIN_SPAN, 1), 0)
    d = tpos - kpos
    st = jnp.where((d >= 0) & (d < WINDOW) & (kpos >= 0), st, MASK_FILL)
    p_w = jnp.exp(st - jnp.max(st, axis=0, keepdims=True))
    pv_w = jnp.dot(vwT_ref[0, 0, :, pl.ds(base, WIN_SPAN)], p_w.astype(bf16), preferred_element_type=f32)
    o_win = pv_w[:HEAD_DIM] / pv_w[HEAD_DIM:HEAD_DIM + 1]

    dbase = pl.multiple_of((c // 2) * LANES, LANES)
    st = jnp.dot(ks_ref[0, 0, pl.ds(dbase, LANES), :HEAD_DIM], q, preferred_element_type=f32)
    kpos = dbase + lax.broadcasted_iota(jnp.int32, (LANES, 1), 0)
    st = jnp.where((kpos >= c * SLC_BLOCK) & (kpos <= tpos), st, MASK_FILL)
    m_d = jnp.max(st, axis=0, keepdims=True)
    pv_d = jnp.dot(vsT_ref[0, 0, :, pl.ds(dbase, LANES)], jnp.exp(st - m_d).astype(bf16),
                   preferred_element_type=f32)

    jidx = lax.broadcasted_iota(jnp.int32, (n_slc, LANES), 0)
    forced = (jidx == 0) | (jidx == c) | (jidx == c - 1)
    work = jnp.where(forced | (jidx > c), -2.0, imp)
    sel = forced
    for _ in range(n_sel - 3):
        mx = jnp.max(work, axis=0, keepdims=True)
        first = jnp.min(jnp.where(work == mx, jidx, n_slc), axis=0, keepdims=True)
        hit = jidx == first
        sel = sel | hit
        work = jnp.where(hit, -2.0, work)
    early = (jnp.zeros_like(jidx) + c) < n_sel
    sel = (sel | early) & (jidx < c)
    bias = jnp.where(sel, 0.0, MASK_FILL)
    bias_ref[...] = jnp.concatenate([bias, bias], axis=1).astype(bf16)

    zeros_tail = jnp.zeros((LANES - HEAD_DIM - SLC_STEP_BLOCKS, ncols), bf16)

    def scores(step):
        rows = bias_ref[pl.ds(pl.multiple_of(step * SLC_STEP_BLOCKS, SLC_STEP_BLOCKS), SLC_STEP_BLOCKS), :]
        rhs = jnp.concatenate([q, rows, zeros_tail], axis=0)
        off = pl.multiple_of(step * SLC_STEP, SLC_STEP)
        st = jnp.dot(ks_ref[0, 0, pl.ds(off, SLC_STEP), :], rhs, preferred_element_type=f32)
        st_ref[step & 1] = st
        return jnp.max(st, axis=0, keepdims=True)

    def accumulate(step, mx, m, acc):
        m_new = jnp.maximum(m, mx)
        off = pl.multiple_of(step * SLC_STEP, SLC_STEP)
        pv = jnp.dot(vsT_ref[0, 0, :, pl.ds(off, SLC_STEP)], jnp.exp(st_ref[step & 1] - m_new).astype(bf16),
                     preferred_element_type=f32)
        return m_new, jnp.exp(m - m_new) * acc + pv

    n_steps = jnp.maximum((c + SLC_STEP_BLOCKS - 1) // SLC_STEP_BLOCKS, 1)

    def body(i, carry):
        mx, m, acc = carry
        m, acc = accumulate(i, mx, m, acc)
        return scores(i + 1), m, acc

    carry = (scores(0), jnp.full((1, ncols), MAX_FLOOR, f32), jnp.zeros((V_ROWS, ncols), f32))
    mx, m_s, acc = lax.fori_loop(0, n_steps - 1, body, carry)
    m_s, acc = accumulate(n_steps - 1, mx, m_s, acc)
    m_t = jnp.maximum(m_s, m_d)
    tot = jnp.exp(m_s - m_t) * acc + jnp.exp(m_d - m_t) * pv_d
    o_slc = tot[:HEAD_DIM] / tot[HEAD_DIM:HEAD_DIM + 1]

    g = g_ref[0, 0, 0]
    o = g[0:1] * o_cmp + g[1:2] * o_slc + g[2:3] * o_win
    o_ref[0, 0, 0] = o.astype(o_ref.dtype)


def nsa_attention_pallas(q2, kvs, gates2, cmp_pos, cmp_w1, cmp_w2, B, L, interpret=False):
    f32, bf16 = jnp.float32, jnp.bfloat16
    G, H, hd = N_KV, HPG, HEAD_DIM
    C = L // Q_CHUNK
    n16 = L // CMP_STRIDE
    n_cmp = (L - CMP_BLOCK) // CMP_STRIDE + 1
    ncp = -(-n_cmp // LANES) * LANES
    n_slc = L // SLC_BLOCK
    n_sel = min(SLC_TOPK, n_slc)
    k_cmp, v_cmp, k_slc, v_slc, k_win, v_win = kvs

    def compress(kv2, pos, w1, w2):
        x = kv2.reshape(B, n16, CMP_STRIDE, G, hd).transpose(0, 3, 1, 2, 4).reshape(B * G * n16, CMP_STRIDE * hd)
        half = CMP_STRIDE * hd
        wcat = jnp.concatenate([w1[:half], w1[half:]], axis=1)
        ab = pmm(x, wcat, interpret=interpret).reshape(B, G, n16, 2 * CMP_HIDDEN)
        posb = pmm(jnp.pad(pos.reshape(1, -1), ((0, 7), (0, 0))), w1, interpret=interpret)[0]
        hid = ab[:, :, :-1, :CMP_HIDDEN] + ab[:, :, 1:, CMP_HIDDEN:] + posb
        hid = jnp.pad(jax.nn.gelu(hid), ((0, 0), (0, 0), (0, ncp - n_cmp), (0, 0)))
        out = pmm(hid.reshape(B * G * ncp, CMP_HIDDEN), w2, interpret=interpret)
        return out.reshape(B, G, ncp, hd)

    kc = compress(k_cmp, cmp_pos[0], cmp_w1[0], cmp_w2[0]).astype(bf16)
    vcT = compress(v_cmp, cmp_pos[1], cmp_w1[1], cmp_w2[1]).astype(bf16).transpose(0, 1, 3, 2)

    def key_major(a):
        return a.astype(bf16).reshape(B, L, G, hd).transpose(0, 2, 1, 3)

    def dim_major(a):
        return a.astype(bf16).reshape(B, L, G, hd).transpose(0, 2, 3, 1)

    def with_ones_row(vT):
        n = vT.shape[-1]
        extra = jnp.zeros((B, G, V_ROWS - hd, n), bf16).at[:, :, 0, :].set(1.0)
        return jnp.concatenate([vT, extra], axis=2)

    blk_onehot = jax.nn.one_hot((jnp.arange(L) // SLC_BLOCK) % SLC_STEP_BLOCKS, LANES - hd, dtype=bf16)
    ks = jnp.concatenate([key_major(k_slc), jnp.broadcast_to(blk_onehot, (B, G, L, LANES - hd))], axis=-1)
    vsT = with_ones_row(dim_major(v_slc))
    kw = jnp.pad(key_major(k_win), ((0, 0), (0, 0), (WINDOW, 0), (0, 0)))
    vwT = with_ones_row(jnp.pad(dim_major(v_win), ((0, 0), (0, 0), (0, 0), (WINDOW, 0))))
    scale = HEAD_DIM ** -0.5
    qT = (q2.astype(f32) * scale).astype(bf16).reshape(B, C, Q_CHUNK, G, H, hd)
    qT = qT.transpose(0, 3, 1, 5, 4, 2).reshape(B, G, C, hd, H * Q_CHUNK)
    gT = gates2.astype(f32).reshape(B, C, Q_CHUNK, 3, G, H).transpose(0, 4, 1, 3, 5, 2).reshape(B, G, C, 3, H * Q_CHUNK)
    c_start = jnp.arange(ncp)[None, :] * CMP_STRIDE
    s_start = jnp.arange(n_slc)[:, None] * SLC_BLOCK
    ovT = ((c_start < s_start + SLC_BLOCK) & (c_start + CMP_BLOCK > s_start)
           & (jnp.arange(ncp)[None, :] < n_cmp)).astype(bf16)

    ncols = H * Q_CHUNK
    bg = lambda b, g, c: (b, g, 0, 0)
    bgc = lambda b, g, c: (b, g, c, 0, 0)
    oT = pl.pallas_call(
        functools.partial(_nsa_kernel, n_cmp=n_cmp, n_slc=n_slc, n_sel=n_sel),
        grid=(B, G, C),
        in_specs=[pl.BlockSpec((1, 1, 1, hd, ncols), bgc),
                  pl.BlockSpec((1, 1, ncp, hd), bg),
                  pl.BlockSpec((1, 1, hd, ncp), bg),
                  pl.BlockSpec((1, 1, L, LANES), bg),
                  pl.BlockSpec((1, 1, V_ROWS, L), bg),
                  pl.BlockSpec((1, 1, L + WINDOW, hd), bg),
                  pl.BlockSpec((1, 1, V_ROWS, L + WINDOW), bg),
                  pl.BlockSpec((1, 1, 1, 3, ncols), bgc),
                  pl.BlockSpec((n_slc, ncp), lambda b, g, c: (0, 0))],
        out_specs=pl.BlockSpec((1, 1, 1, hd, ncols), bgc),
        out_shape=jax.ShapeDtypeStruct((B, G, C, hd, ncols), bf16),
        scratch_shapes=[pltpu.VMEM((n_slc, ncols), bf16), pltpu.VMEM((2, SLC_STEP, ncols), f32)],
        compiler_params=pltpu.CompilerParams(
            dimension_semantics=("parallel", "parallel", "arbitrary"),
            vmem_limit_bytes=VMEM_LIMIT_BYTES),
        interpret=interpret,
        name="nsa_attention",
    )(qT, kc, vcT, ks, vsT, kw, vwT, gT, ovT)
    o = oT.reshape(B, G, C, hd, H, Q_CHUNK).transpose(0, 2, 5, 1, 4, 3)
    return o.reshape(B * L, G * H * hd)


S5_CHUNK = 32


def _s5_state_kernel(x_ref, ws_ref, s_ref):
    s_ref[0] = jnp.dot(x_ref[0].astype(jnp.bfloat16), ws_ref[0], preferred_element_type=jnp.float32)


def _s5_scan_kernel(sre_ref, sim_ref, are_ref, aim_ref, hre_ref, him_ref, *, chunks_per_batch):
    a_re = are_ref[...]
    a_im = aim_ref[...]
    n_batch = sre_ref.shape[0] // chunks_per_batch

    def step(c, carry):
        new = []
        for b in range(n_batch):
            h_re, h_im = carry[b]
            row = b * chunks_per_batch + c
            hre_ref[row] = h_re
            him_ref[row] = h_im
            new.append((a_re * h_re - a_im * h_im + sre_ref[row],
                        a_re * h_im + a_im * h_re + sim_ref[row]))
        return tuple(new)

    zero = jnp.zeros(a_re.shape, jnp.float32)
    lax.fori_loop(0, chunks_per_batch, step, tuple((zero, zero) for _ in range(n_batch)))


def _s5_out_kernel(x_ref, hp_ref, m_ref, wo_ref, d_ref, y_ref):
    x = x_ref[0]
    y = jnp.dot(x.astype(jnp.bfloat16), m_ref[0], preferred_element_type=jnp.float32)
    y = y + jnp.dot(hp_ref[0].astype(jnp.bfloat16), wo_ref[0], preferred_element_type=jnp.float32)
    y_ref[0] = y + x * d_ref[0]


def s5_scan_pallas(u2, lam_re, lam_im, log_dt, b_re, b_im, c_re, c_im, d_skip, B, L, interpret=False):
    f32, bf16 = jnp.float32, jnp.bfloat16
    G, P, H, T = SSM_GROUPS, SSM_STATE, SSM_GROUP, S5_CHUNK
    hi = lax.Precision.HIGHEST
    NC = B * L // T
    dt = jnp.exp(log_dt.astype(f32))[:, None]
    lr, li = lam_re.astype(f32), lam_im.astype(f32)
    mag = jnp.exp(lr * dt)
    ab_re, ab_im = mag * jnp.cos(li * dt), mag * jnp.sin(li * dt)
    den = lr * lr + li * li
    nr = ab_re - 1.0
    cr = (nr * lr + ab_im * li) / den
    cim = (ab_im * lr - nr * li) / den
    br, bim = b_re.astype(f32), b_im.astype(f32)
    bb_re = cr[..., None] * br - cim[..., None] * bim
    bb_im = cr[..., None] * bim + cim[..., None] * br
    k = jnp.arange(T + 1, dtype=f32)[:, None, None]
    pw_mag = jnp.exp(k * (lr * dt))
    pw_re = pw_mag * jnp.cos(k * (li * dt))
    pw_im = pw_mag * jnp.sin(k * (li * dt))
    cre, cimg = c_re.astype(f32), c_im.astype(f32)
    ab_r = pw_re[:T, :, :, None] * bb_re - pw_im[:T, :, :, None] * bb_im
    ab_i = pw_re[:T, :, :, None] * bb_im + pw_im[:T, :, :, None] * bb_re
    kern = (jnp.einsum('gop,tgpi->tgoi', cre, ab_r, precision=hi)
            - jnp.einsum('gop,tgpi->tgoi', cimg, ab_i, precision=hi))
    tt = jnp.arange(T)
    lag = tt[None, :] - tt[:, None]
    m = jnp.where((lag >= 0)[:, :, None, None, None], kern[jnp.clip(lag, 0, T - 1)], 0.0)
    m = m.transpose(2, 0, 4, 1, 3).reshape(G, T * H, T * H).astype(bf16)
    ws_re = ab_r[::-1].transpose(1, 0, 3, 2).reshape(G, T * H, P)
    ws_im = ab_i[::-1].transpose(1, 0, 3, 2).reshape(G, T * H, P)
    ws = jnp.concatenate([ws_re, ws_im], axis=-1).astype(bf16)
    ca_re = cre[None] * pw_re[1:, :, None, :] - cimg[None] * pw_im[1:, :, None, :]
    ca_im = cre[None] * pw_im[1:, :, None, :] + cimg[None] * pw_re[1:, :, None, :]
    wo = jnp.concatenate([ca_re.transpose(1, 3, 0, 2).reshape(G, P, T * H),
                          -ca_im.transpose(1, 3, 0, 2).reshape(G, P, T * H)], axis=1).astype(bf16)
    dtile = jnp.tile(d_skip.astype(f32), (1, T)).reshape(G, 1, T * H)

    xg = u2.astype(f32).reshape(NC, T, G, H).transpose(2, 0, 1, 3).reshape(G, NC, T * H)
    cparams = pltpu.CompilerParams(dimension_semantics=("parallel",), vmem_limit_bytes=VMEM_LIMIT_BYTES)
    s = pl.pallas_call(
        _s5_state_kernel, grid=(G,),
        in_specs=[pl.BlockSpec((1, NC, T * H), lambda g: (g, 0, 0)),
                  pl.BlockSpec((1, T * H, 2 * P), lambda g: (g, 0, 0))],
        out_specs=pl.BlockSpec((1, NC, 2 * P), lambda g: (g, 0, 0)),
        out_shape=jax.ShapeDtypeStruct((G, NC, 2 * P), f32),
        compiler_params=cparams, interpret=interpret, name="s5_chunk_state",
    )(xg, ws)
    sp = s.reshape(G, NC, 2, P).transpose(1, 2, 0, 3).reshape(NC, 2, G // 2, 2 * P)
    a_re = pw_re[T].reshape(G // 2, 2 * P)
    a_im = pw_im[T].reshape(G // 2, 2 * P)
    full = lambda shape: pl.BlockSpec(shape, lambda i: (0,) * len(shape))
    st_shape = (NC, G // 2, 2 * P)
    h_re, h_im = pl.pallas_call(
        functools.partial(_s5_scan_kernel, chunks_per_batch=L // T), grid=(1,),
        in_specs=[full(st_shape), full(st_shape), full((G // 2, 2 * P)), full((G // 2, 2 * P))],
        out_specs=[full(st_shape), full(st_shape)],
        out_shape=[jax.ShapeDtypeStruct(st_shape, f32)] * 2,
        compiler_params=pltpu.CompilerParams(dimension_semantics=("arbitrary",), vmem_limit_bytes=VMEM_LIMIT_BYTES),
        interpret=interpret, name="s5_state_scan",
    )(sp[:, 0], sp[:, 1], a_re, a_im)
    hp = jnp.stack([h_re, h_im], axis=1).reshape(NC, 2, G, P).transpose(2, 0, 1, 3).reshape(G, NC, 2 * P)
    y = pl.pallas_call(
        _s5_out_kernel, grid=(G,),
        in_specs=[pl.BlockSpec((1, NC, T * H), lambda g: (g, 0, 0)),
                  pl.BlockSpec((1, NC, 2 * P), lambda g: (g, 0, 0)),
                  pl.BlockSpec((1, T * H, T * H), lambda g: (g, 0, 0)),
                  pl.BlockSpec((1, 2 * P, T * H), lambda g: (g, 0, 0)),
                  pl.BlockSpec((1, 1, T * H), lambda g: (g, 0, 0))],
        out_specs=pl.BlockSpec((1, NC, T * H), lambda g: (g, 0, 0)),
        out_shape=jax.ShapeDtypeStruct((G, NC, T * H), f32),
        compiler_params=cparams, interpret=interpret, name="s5_chunk_out",
    )(xg, hp, m, wo, dtile)
    return y.reshape(G, NC, T, H).transpose(1, 2, 0, 3).reshape(B * L, G * H)


MOE_TILE = 512
MOE_SEG = 32
MOE_SLOTS = TOPK_IN_GROUP * MOE_TILE + N_EXPERTS * MOE_SEG
ROUTE_E1, ROUTE_E2, ROUTE_W1, ROUTE_W2 = 0, 1, 2, 3


def _split_bf16(a):
    hi = a.astype(jnp.bfloat16)
    return hi, (a - hi.astype(jnp.float32)).astype(jnp.bfloat16)


def _moe_route_kernel(x_ref, g_ref, sc_ref, sh_ref, wr_ref, br_ref, h_ref, route_ref, routeT_ref, cnt_ref):
    f32 = jnp.float32
    x = x_ref[...]
    y = x * lax.rsqrt(jnp.mean(x * x, axis=-1, keepdims=True) + EPS)
    h = y * g_ref[...] * (1.0 + sc_ref[0]) + sh_ref[0]
    h_ref[...] = h.astype(h_ref.dtype)
    h_hi, h_lo = _split_bf16(h)
    w_hi, w_lo = _split_bf16(wr_ref[...])
    lg = (jnp.dot(h_hi, w_hi, preferred_element_type=f32) + jnp.dot(h_hi, w_lo, preferred_element_type=f32)
          + jnp.dot(h_lo, w_hi, preferred_element_type=f32) + br_ref[...])
    lane = lax.broadcasted_iota(jnp.int32, lg.shape, 1)
    ninf = -jnp.inf
    gmask = lane < N_GROUPS
    gmax = jnp.max(jnp.where(gmask, lg, ninf), axis=1, keepdims=True)
    gsum = jnp.sum(jnp.where(gmask, jnp.exp(lg - gmax), 0.0), axis=1, keepdims=True)
    g_idx = jnp.min(jnp.where(gmask & (lg == gmax), lane, LANES), axis=1, keepdims=True)
    g_p = 1.0 / gsum
    lo = N_GROUPS + EXP_PER_GROUP * g_idx
    emask = (lane >= lo) & (lane < lo + EXP_PER_GROUP)
    el = jnp.where(emask, lg, ninf)
    emax = jnp.max(el, axis=1, keepdims=True)
    i1 = jnp.min(jnp.where(el == emax, lane, LANES), axis=1, keepdims=True)
    el2 = jnp.where(lane == i1, ninf, el)
    emax2 = jnp.max(el2, axis=1, keepdims=True)
    i2 = jnp.min(jnp.where(el2 == emax2, lane, LANES), axis=1, keepdims=True)
    r = jnp.exp(emax2 - emax)
    w1 = g_p / (1.0 + r)
    w2 = g_p * r / (1.0 + r)
    e1 = (i1 - N_GROUPS).astype(f32)
    e2 = (i2 - N_GROUPS).astype(f32)
    route = jnp.where(lane == ROUTE_E1, e1, jnp.where(lane == ROUTE_E2, e2,
                      jnp.where(lane == ROUTE_W1, w1, jnp.where(lane == ROUTE_W2, w2, 0.0))))
    route_ref[...] = route
    routeT_ref[...] = route.T[:8, :]
    cnt = jnp.sum(jnp.where((lane == i1) | (lane == i2), 1.0, 0.0), axis=0, keepdims=True)
    cnt_ref[0] = jnp.broadcast_to(cnt, (8, LANES))


def _moe_ffn_kernel(seg_start_ref, seg_nblk_ref, h_ref, route_ref, routeT_ref, lrow_ref, lcol_ref, wgu_ref, wd_ref,
                    x_ref, gate_ref, fg_ref, o_ref, xs_ref, yw_ref, ws_ref, *, final_norm):
    f32, bf16 = jnp.float32, jnp.bfloat16
    t = pl.program_id(0)
    e = pl.program_id(1)
    tm = h_ref.shape[0]

    @pl.when(e == 0)
    def _dispatch():
        rT = routeT_ref[...]
        e1r, e2r = rT[ROUTE_E1:ROUTE_E1 + 1], rT[ROUTE_E2:ROUTE_E2 + 1]
        w1r, w2r = rT[ROUTE_W1:ROUTE_W1 + 1], rT[ROUTE_W2:ROUTE_W2 + 1]
        eio = lax.broadcasted_iota(jnp.int32, (N_EXPERTS, tm), 0).astype(f32)
        oh1 = eio == e1r
        oh2 = eio == e2r
        oh = jnp.where(oh1 | oh2, 1.0, 0.0).astype(bf16)
        before = (lax.broadcasted_iota(jnp.int32, (tm, tm), 0)
                  < lax.broadcasted_iota(jnp.int32, (tm, tm), 1))
        rank = jnp.dot(oh, jnp.where(before, 1.0, 0.0).astype(bf16), preferred_element_type=f32)
        pos = lcol_ref[0][:, :1] + rank
        d1 = jnp.sum(jnp.where(oh1, pos, 0.0), axis=0, keepdims=True)
        d2 = jnp.sum(jnp.where(oh2, pos, 0.0), axis=0, keepdims=True)
        slot = lax.broadcasted_iota(jnp.int32, (MOE_SLOTS, tm), 0).astype(f32)
        m1 = slot == d1
        m2 = slot == d2
        perm = jnp.where(m1 | m2, 1.0, 0.0).astype(bf16)
        xs_ref[...] = jnp.dot(perm, h_ref[...], preferred_element_type=f32).astype(bf16)
        ws_ref[...] = jnp.sum(jnp.where(m1, w1r, 0.0) + jnp.where(m2, w2r, 0.0), axis=1, keepdims=True)
        yw_ref[...] = jnp.zeros(yw_ref.shape, yw_ref.dtype)

    start = seg_start_ref[t * N_EXPERTS + e]
    nblk = seg_nblk_ref[t * N_EXPERTS + e]

    def block(i, _):
        r0 = pl.multiple_of(start + i * MOE_SEG, MOE_SEG)
        xb = xs_ref[pl.ds(r0, MOE_SEG), :]
        gu = jnp.dot(xb, wgu_ref[0], preferred_element_type=f32)
        a = jax.nn.silu(gu[:, :D_EXPERT]) * gu[:, D_EXPERT:]
        y = jnp.dot(a.astype(bf16), wd_ref[0], preferred_element_type=f32)
        yw_ref[pl.ds(r0, MOE_SEG), :] = (y * ws_ref[pl.ds(r0, MOE_SEG), :]).astype(bf16)
        return 0

    lax.fori_loop(0, nblk, block, 0)

    @pl.when(e == N_EXPERTS - 1)
    def _combine():
        r = route_ref[...]
        e1c, e2c = r[:, ROUTE_E1:ROUTE_E1 + 1], r[:, ROUTE_E2:ROUTE_E2 + 1]
        lane = lax.broadcasted_iota(jnp.int32, (tm, LANES), 1).astype(f32)
        oh1 = lane == e1c
        oh2 = lane == e2c
        oh = jnp.where(oh1 | oh2, 1.0, 0.0).astype(bf16)
        after = (lax.broadcasted_iota(jnp.int32, (tm, tm), 1)
                 < lax.broadcasted_iota(jnp.int32, (tm, tm), 0))
        rank = jnp.dot(jnp.where(after, 1.0, 0.0).astype(bf16), oh, preferred_element_type=f32)
        pos = lrow_ref[0][:1, :] + rank
        d1 = jnp.sum(jnp.where(oh1, pos, 0.0), axis=1, keepdims=True)
        d2 = jnp.sum(jnp.where(oh2, pos, 0.0), axis=1, keepdims=True)
        slot = lax.broadcasted_iota(jnp.int32, (tm, MOE_SLOTS), 1).astype(f32)
        comb = jnp.where((slot == d1) | (slot == d2), 1.0, 0.0).astype(bf16)
        moe = jnp.dot(comb, yw_ref[...], preferred_element_type=f32)
        xn = x_ref[...] + gate_ref[0] * moe
        if final_norm:
            xn = xn * lax.rsqrt(jnp.mean(xn * xn, axis=-1, keepdims=True) + EPS) * fg_ref[...]
        o_ref[...] = xn


def moe_layer_pallas(x2, norm_g, sc2, sh2, g2, w_group, b_group, w_expert, b_expert, w_gate, w_up, w_down,
                     final_g, final_norm, B, L, interpret=False):
    f32, bf16 = jnp.float32, jnp.bfloat16
    N, D = x2.shape
    tm = MOE_TILE
    n_tiles = N // tm
    tpb = L // tm
    wr = jnp.zeros((D, LANES), f32).at[:, :N_GROUPS].set(w_group).at[:, N_GROUPS:N_GROUPS + N_EXPERTS].set(w_expert)
    br = jnp.zeros((1, LANES), f32).at[0, :N_GROUPS].set(b_group).at[0, N_GROUPS:N_GROUPS + N_EXPERTS].set(b_expert)
    row = lambda v: v.reshape(1, D).astype(f32)
    per_batch = lambda v: v.reshape(B, 1, D).astype(f32)
    cp = pltpu.CompilerParams(dimension_semantics=("parallel",), vmem_limit_bytes=VMEM_LIMIT_BYTES)
    h, route, routeT, cnt = pl.pallas_call(
        _moe_route_kernel, grid=(n_tiles,),
        in_specs=[pl.BlockSpec((tm, D), lambda i: (i, 0)),
                  pl.BlockSpec((1, D), lambda i: (0, 0)),
                  pl.BlockSpec((1, 1, D), lambda i: (i // tpb, 0, 0)),
                  pl.BlockSpec((1, 1, D), lambda i: (i // tpb, 0, 0)),
                  pl.BlockSpec((D, LANES), lambda i: (0, 0)),
                  pl.BlockSpec((1, LANES), lambda i: (0, 0))],
        out_specs=[pl.BlockSpec((tm, D), lambda i: (i, 0)),
                   pl.BlockSpec((tm, LANES), lambda i: (i, 0)),
                   pl.BlockSpec((8, tm), lambda i: (0, i)),
                   pl.BlockSpec((1, 8, LANES), lambda i: (i, 0, 0))],
        out_shape=[jax.ShapeDtypeStruct((N, D), bf16), jax.ShapeDtypeStruct((N, LANES), f32),
                   jax.ShapeDtypeStruct((8, N), f32), jax.ShapeDtypeStruct((n_tiles, 8, LANES), f32)],
        compiler_params=cp, interpret=interpret, name="moe_route",
    )(x2, row(norm_g), per_batch(sc2), per_batch(sh2), wr, br)
    counts = cnt[:, 0, N_GROUPS:N_GROUPS + N_EXPERTS].astype(jnp.int32)
    nblk = (counts + MOE_SEG - 1) // MOE_SEG
    starts = (jnp.cumsum(nblk, axis=1) - nblk) * MOE_SEG
    starts_f = starts.astype(f32)
    lrow = jnp.zeros((n_tiles, 8, LANES), f32).at[:, :, :N_EXPERTS].set(starts_f[:, None, :])
    lcol = jnp.broadcast_to(starts_f[:, :, None], (n_tiles, N_EXPERTS, LANES))
    wgu = jnp.concatenate([w_gate, w_up], axis=-1).astype(bf16)
    wd = w_down.astype(bf16)
    tile = lambda t, e, *_: (t, 0)
    out = pl.pallas_call(
        functools.partial(_moe_ffn_kernel, final_norm=final_norm),
        grid_spec=pltpu.PrefetchScalarGridSpec(
            num_scalar_prefetch=2, grid=(n_tiles, N_EXPERTS),
            in_specs=[pl.BlockSpec((tm, D), tile),
                      pl.BlockSpec((tm, LANES), tile),
                      pl.BlockSpec((8, tm), lambda t, e, *_: (0, t)),
                      pl.BlockSpec((1, 8, LANES), lambda t, e, *_: (t, 0, 0)),
                      pl.BlockSpec((1, N_EXPERTS, LANES), lambda t, e, *_: (t, 0, 0)),
                      pl.BlockSpec((1, D, 2 * D_EXPERT), lambda t, e, *_: (e, 0, 0)),
                      pl.BlockSpec((1, D_EXPERT, D), lambda t, e, *_: (e, 0, 0)),
                      pl.BlockSpec((tm, D), tile),
                      pl.BlockSpec((1, 1, D), lambda t, e, *_: (t // tpb, 0, 0)),
                      pl.BlockSpec((1, D), lambda t, e, *_: (0, 0))],
            out_specs=pl.BlockSpec((tm, D), tile),
            scratch_shapes=[pltpu.VMEM((MOE_SLOTS, D), bf16), pltpu.VMEM((MOE_SLOTS, D), bf16),
                            pltpu.VMEM((MOE_SLOTS, 1), f32)]),
        out_shape=jax.ShapeDtypeStruct((N, D), f32),
        compiler_params=pltpu.CompilerParams(dimension_semantics=("parallel", "arbitrary"),
                                             vmem_limit_bytes=VMEM_LIMIT_BYTES),
        interpret=interpret, name="moe_ffn",
    )(starts.reshape(-1), nblk.reshape(-1), h, route, routeT, lrow, lcol, wgu, wd, x2, per_batch(g2), row(final_g))
    return out


def rms_norm(x, g):
    xf = x.astype(jnp.float32)
    y = xf * lax.rsqrt(jnp.mean(xf * xf, axis=-1, keepdims=True) + EPS)
    return (y * g.astype(jnp.float32)).astype(x.dtype)


def masked_softmax(s, mask):
    p = jax.nn.softmax(jnp.where(mask, s.astype(jnp.float32), NEG), axis=-1)
    return p * mask


def s5_mixer(u, lam_re, lam_im, log_dt, b_re, b_im, c_re, c_im, d_skip, w_glu):
    f32 = jnp.float32
    Bsz, L, _ = u.shape
    ug = u.astype(f32).reshape(Bsz, L, SSM_GROUPS, SSM_GROUP)
    dt = jnp.exp(log_dt.astype(f32))[:, None]
    lr, li = lam_re.astype(f32), lam_im.astype(f32)
    mag = jnp.exp(lr * dt)
    ab_re, ab_im = mag * jnp.cos(li * dt), mag * jnp.sin(li * dt)
    den = lr * lr + li * li
    nr = ab_re - 1.0
    cr = (nr * lr + ab_im * li) / den
    cim = (ab_im * lr - nr * li) / den
    br, bim = b_re.astype(f32), b_im.astype(f32)
    bb_re = cr[..., None] * br - cim[..., None] * bim
    bb_im = cr[..., None] * bim + cim[..., None] * br
    bu_re = jnp.einsum('blgh,gph->blgp', ug, bb_re)
    bu_im = jnp.einsum('blgh,gph->blgp', ug, bb_im)
    a_re = jnp.broadcast_to(ab_re, bu_re.shape)
    a_im = jnp.broadcast_to(ab_im, bu_im.shape)

    def combine(e1, e2):
        a1r, a1i, b1r, b1i = e1
        a2r, a2i, b2r, b2i = e2
        return (a2r * a1r - a2i * a1i, a2r * a1i + a2i * a1r,
                a2r * b1r - a2i * b1i + b2r, a2r * b1i + a2i * b1r + b2i)

    _, _, s_re, s_im = lax.associative_scan(combine, (a_re, a_im, bu_re, bu_im), axis=1)
    y = (jnp.einsum('blgp,ghp->blgh', s_re, c_re.astype(f32))
         - jnp.einsum('blgp,ghp->blgh', s_im, c_im.astype(f32))
         + d_skip.astype(f32) * ug)
    y = jax.nn.gelu(y.reshape(Bsz, L, SSM_WIDTH))
    y = y * jax.nn.sigmoid(pmm(y.reshape(Bsz * L, SSM_WIDTH), w_glu).reshape(Bsz, L, SSM_WIDTH))
    return y.astype(u.dtype)


def compress_kv(kv, pos, w1, w2):
    Bsz, L, G, hd = kv.shape
    n_cmp = (L - CMP_BLOCK) // CMP_STRIDE + 1
    idx = jnp.arange(n_cmp)[:, None] * CMP_STRIDE + jnp.arange(CMP_BLOCK)[None, :]
    blocks = kv[:, idx] + pos[:, None, :]
    blocks = jnp.swapaxes(blocks, 2, 3).reshape(Bsz, n_cmp, G, CMP_BLOCK * hd)
    return jax.nn.gelu(blocks @ w1) @ w2


def nsa_attention(q, k_cmp, v_cmp, k_slc, v_slc, k_win, v_win, gates, cmp_pos, cmp_w1, cmp_w2):
    Bsz, L = q.shape[:2]
    n_cmp = (L - CMP_BLOCK) // CMP_STRIDE + 1
    n_slc = L // SLC_BLOCK
    n_sel = min(SLC_TOPK, n_slc)
    n_chunks = L // Q_CHUNK
    scale = HEAD_DIM ** -0.5
    kc = compress_kv(k_cmp, cmp_pos[0], cmp_w1[0], cmp_w2[0])
    vc = compress_kv(v_cmp, cmp_pos[1], cmp_w1[1], cmp_w2[1])
    cmp_end = jnp.arange(n_cmp) * CMP_STRIDE + CMP_BLOCK - 1
    c_start = jnp.arange(n_cmp)[:, None] * CMP_STRIDE
    s_start = jnp.arange(n_slc)[None, :] * SLC_BLOCK
    overlap = ((c_start < s_start + SLC_BLOCK) & (c_start + CMP_BLOCK > s_start)).astype(jnp.float32)
    ks_blk = jnp.moveaxis(k_slc.reshape(Bsz, n_slc, SLC_BLOCK, N_KV, HEAD_DIM), 3, 1)
    vs_blk = jnp.moveaxis(v_slc.reshape(Bsz, n_slc, SLC_BLOCK, N_KV, HEAD_DIM), 3, 1)
    kw = jnp.pad(k_win, ((0, 0), (WINDOW, 0), (0, 0), (0, 0)))
    vw = jnp.pad(v_win, ((0, 0), (WINDOW, 0), (0, 0), (0, 0)))
    bi = jnp.arange(Bsz)[:, None, None, None]
    gi = jnp.arange(N_KV)[None, :, None, None]
    blk = jnp.arange(n_slc)

    def chunk(args):
        c_idx, qc, gc = args
        t = c_idx * Q_CHUNK + jnp.arange(Q_CHUNK)
        s = jnp.einsum('bqghd,bngd->bqghn', qc, kc) * scale
        m = cmp_end[None, :] <= t[:, None]
        p_cmp = masked_softmax(s, m[None, :, None, None, :])
        o_cmp = jnp.einsum('bqghn,bngd->bqghd', p_cmp.astype(vc.dtype), vc)
        imp = jnp.einsum('bqghn,ns->bqgs', p_cmp, overlap)
        cur = t[:, None] // SLC_BLOCK
        forced = (blk[None, :] == 0) | (blk[None, :] == cur) | (blk[None, :] == cur - 1)
        future = blk[None, :] * SLC_BLOCK > t[:, None]
        imp = jnp.where(forced[None, :, None, :], FORCED_SCORE, imp)
        imp = jnp.where(future[None, :, None, :], -1.0, imp)
        _, sel = lax.top_k(imp, n_sel)
        sel = jnp.transpose(sel, (0, 2, 1, 3))
        ks = ks_blk[bi, gi, sel].reshape(Bsz, N_KV, Q_CHUNK, n_sel * SLC_BLOCK, HEAD_DIM)
        vs = vs_blk[bi, gi, sel].reshape(Bsz, N_KV, Q_CHUNK, n_sel * SLC_BLOCK, HEAD_DIM)
        kpos = (sel[..., None] * SLC_BLOCK + jnp.arange(SLC_BLOCK)).reshape(Bsz, N_KV, Q_CHUNK, n_sel * SLC_BLOCK)
        m = jnp.transpose(kpos <= t[None, None, :, None], (0, 2, 1, 3))[:, :, :, None, :]
        s = jnp.einsum('bqghd,bgqkd->bqghk', qc, ks) * scale
        p = masked_softmax(s, m)
        o_slc = jnp.einsum('bqghk,bgqkd->bqghd', p.astype(vs.dtype), vs)
        start = c_idx * Q_CHUNK
        kwc = lax.dynamic_slice_in_dim(kw, start, WINDOW + Q_CHUNK, axis=1)
        vwc = lax.dynamic_slice_in_dim(vw, start, WINDOW + Q_CHUNK, axis=1)
        wpos = start - WINDOW + jnp.arange(WINDOW + Q_CHUNK)
        d = t[:, None] - wpos[None, :]
        m = (d >= 0) & (d < WINDOW) & (wpos[None, :] >= 0)
        s = jnp.einsum('bqghd,bkgd->bqghk', qc, kwc) * scale
        p = masked_softmax(s, m[None, :, None, None, :])
        o_win = jnp.einsum('bqghk,bkgd->bqghd', p.astype(vwc.dtype), vwc)
        return (gc[:, :, 0, :, :, None] * o_cmp + gc[:, :, 1, :, :, None] * o_slc
                + gc[:, :, 2, :, :, None] * o_win)

    q_ch = jnp.moveaxis(q.reshape(Bsz, n_chunks, Q_CHUNK, N_KV, HPG, HEAD_DIM), 1, 0)
    g_ch = jnp.moveaxis(gates.reshape(Bsz, n_chunks, Q_CHUNK, 3, N_KV, HPG), 1, 0)
    o = lax.map(chunk, (jnp.arange(n_chunks), q_ch, g_ch))
    return jnp.moveaxis(o, 0, 1).reshape(Bsz, L, ATTN_WIDTH)


def token_mixer(h, w_in, ssm_lam_re, ssm_lam_im, ssm_log_dt, ssm_b_re, ssm_b_im, ssm_c_re, ssm_c_im,
                ssm_d, ssm_w_glu, ssm_w_o, nsa_cmp_pos, nsa_cmp_w1, nsa_cmp_w2, nsa_w_o, w_out):
    Bsz, L, D = h.shape
    N = Bsz * L
    o1 = ATTN_WIDTH
    o2 = o1 + 6 * KV_WIDTH
    o3 = o2 + 3 * N_HEADS
    o4 = o3 + SSM_WIDTH
    h2 = h.reshape(N, D)
    q = pmm(h2, w_in[:, :o1]).reshape(Bsz, L, -1)
    kv = pmm(h2, w_in[:, o1:o2]).reshape(Bsz, L, -1)
    wg = jnp.pad(w_in[:, o2:o3], ((0, 0), (0, 128 - 3 * N_HEADS)))
    nsa_g = pmm(h2, wg)[:, :3 * N_HEADS].reshape(Bsz, L, -1)
    u = pmm(h2, w_in[:, o3:o4]).reshape(Bsz, L, -1)
    merge_g = pmm(h2, w_in[:, o4:]).reshape(Bsz, L, -1)
    kvs = [t.reshape(N, KV_WIDTH) for t in jnp.split(kv, 6, axis=-1)]
    attn = nsa_attention_pallas(q.reshape(N, ATTN_WIDTH), kvs, jax.nn.sigmoid(nsa_g).reshape(N, 3 * N_HEADS),
                                nsa_cmp_pos, nsa_cmp_w1, nsa_cmp_w2, Bsz, L)
    y = s5_scan_pallas(u.reshape(N, SSM_WIDTH), ssm_lam_re, ssm_lam_im, ssm_log_dt, ssm_b_re, ssm_b_im,
                       ssm_c_re, ssm_c_im, ssm_d, Bsz, L)
    y = jax.nn.gelu(y)
    ssm = (y * jax.nn.sigmoid(pmm(y, ssm_w_glu))).reshape(Bsz, L, SSM_WIDTH)
    g_attn, g_ssm = jnp.split(jax.nn.sigmoid(merge_g), 2, axis=-1)
    merged = (g_attn * pmm(attn.reshape(N, -1), nsa_w_o).reshape(Bsz, L, D)
              + g_ssm * pmm(ssm.reshape(N, -1), ssm_w_o).reshape(Bsz, L, D))
    return pmm(merged.reshape(N, D), w_out).reshape(Bsz, L, D)


def hier_moe(h, w_group, b_group, w_expert, b_expert, w_gate, w_up, w_down):
    Bsz, L, D = h.shape
    xt = h.reshape(-1, D)
    N = xt.shape[0]
    g_prob = jax.nn.softmax((xt @ w_group + b_group).astype(jnp.float32), axis=-1)
    g_p, g_idx = lax.top_k(g_prob, 1)
    e_logits = (xt @ w_expert + b_expert).astype(jnp.float32).reshape(N, N_GROUPS, EXP_PER_GROUP)
    e_logits = jnp.take_along_axis(e_logits, jnp.broadcast_to(g_idx[:, :, None], (N, 1, EXP_PER_GROUP)), axis=1)[:, 0]
    e_p, e_idx = lax.top_k(jax.nn.softmax(e_logits, axis=-1), TOPK_IN_GROUP)
    w = g_p * e_p / jnp.sum(e_p, axis=-1, keepdims=True)
    eid = g_idx * EXP_PER_GROUP + e_idx
    A = N * TOPK_IN_GROUP
    flat_e = eid.reshape(-1)
    flat_t = jnp.repeat(jnp.arange(N, dtype=jnp.int32), TOPK_IN_GROUP)
    flat_w = w.reshape(-1)
    order = jnp.argsort(flat_e)
    se = flat_e[order]
    counts = jnp.zeros(N_EXPERTS, jnp.int32).at[flat_e].add(1)
    starts = jnp.cumsum(counts) - counts
    pcounts = (counts + MOE_BLOCK - 1) // MOE_BLOCK * MOE_BLOCK
    pends = jnp.cumsum(pcounts)
    pstarts = pends - pcounts
    dest = pstarts[se] + jnp.arange(A) - starts[se]
    P = A + N_EXPERTS * MOE_BLOCK
    n_blk = P // MOE_BLOCK
    buf_t = jnp.zeros(P, jnp.int32).at[dest].set(flat_t[order])
    buf_w = jnp.zeros(P, jnp.float32).at[dest].set(flat_w[order])
    blk_e = jnp.minimum(jnp.searchsorted(pends, jnp.arange(n_blk) * MOE_BLOCK, side='right'), N_EXPERTS - 1)
    xs = xt[buf_t].reshape(n_blk, MOE_BLOCK, D)

    def expert_block(args):
        xb, e = args
        return (jax.nn.silu(xb @ w_gate[e]) * (xb @ w_up[e])) @ w_down[e]

    ys = lax.map(expert_block, (xs, blk_e)).reshape(P, D)
    out = jnp.zeros((N, D), jnp.float32).at[buf_t].add(ys.astype(jnp.float32) * buf_w[:, None])
    return out.astype(h.dtype).reshape(Bsz, L, D)


def kernel(x, c, ada_w, ada_b, norm1_g, w_in, ssm_lam_re, ssm_lam_im, ssm_log_dt, ssm_b_re, ssm_b_im, ssm_c_re, ssm_c_im, ssm_d, ssm_w_glu, ssm_w_o, nsa_cmp_pos, nsa_cmp_w1, nsa_cmp_w2, nsa_w_o, w_out, norm2_g, moe_w_group, moe_b_group, moe_w_expert, moe_b_expert, moe_w_gate, moe_w_up, moe_w_down, final_g):
    cs = jax.nn.silu(c)
    for l in range(DEPTH):
        mod = cs @ ada_w[l] + ada_b[l]
        sh1, sc1, g1, sh2, sc2, g2 = jnp.split(mod[:, None, :], 6, axis=-1)
        h = rms_norm(x, norm1_g[l]) * (1.0 + sc1) + sh1
        x = x + g1 * token_mixer(h, w_in[l], ssm_lam_re[l], ssm_lam_im[l], ssm_log_dt[l], ssm_b_re[l],
                                 ssm_b_im[l], ssm_c_re[l], ssm_c_im[l], ssm_d[l], ssm_w_glu[l], ssm_w_o[l],
                                 nsa_cmp_pos[l], nsa_cmp_w1[l], nsa_cmp_w2[l], nsa_w_o[l], w_out[l])
        Bsz, L, D = x.shape
        x = moe_layer_pallas(x.reshape(Bsz * L, D), norm2_g[l], sc2, sh2, g2, moe_w_group[l], moe_b_group[l],
                             moe_w_expert[l], moe_b_expert[l], moe_w_gate[l], moe_w_up[l], moe_w_down[l],
                             final_g, l == DEPTH - 1, Bsz, L).reshape(Bsz, L, D)
    return x
```

```python
import functools
import math

import jax
import jax.numpy as jnp
from jax import lax
from jax.experimental import pallas as pl
from jax.experimental.pallas import tpu as pltpu

D_MODEL = 1024
DEPTH = 2
SSM_GROUP = 16
SSM_STATE = 64
SSM_WIDTH = D_MODEL // 2
SSM_GROUPS = SSM_WIDTH // SSM_GROUP
HEAD_DIM = 64
N_HEADS = D_MODEL // HEAD_DIM
N_KV = N_HEADS // 4
HPG = N_HEADS // N_KV
ATTN_WIDTH = N_HEADS * HEAD_DIM
KV_WIDTH = N_KV * HEAD_DIM
CMP_BLOCK = 32
CMP_STRIDE = 16
CMP_HIDDEN = 4 * HEAD_DIM
SLC_BLOCK = 64
SLC_TOPK = 16
WINDOW = 512
Q_CHUNK = 64
FORCED_SCORE = 1e4
N_GROUPS = 4
EXP_PER_GROUP = 8
N_EXPERTS = N_GROUPS * EXP_PER_GROUP
TOPK_IN_GROUP = 2
D_EXPERT = D_MODEL // 4
MOE_BLOCK = 128
EPS = 1e-6
NEG = -1e30

VMEM_LIMIT_BYTES = 48 * 1024 * 1024


def _mm_kernel(a_ref, b_ref, o_ref):
    a = a_ref[...].astype(jnp.bfloat16)
    b = b_ref[...].astype(jnp.bfloat16)
    o_ref[...] = jnp.dot(a, b, preferred_element_type=jnp.float32).astype(o_ref.dtype)


def pmm(a, b, tm=512, tn=512, out_dtype=jnp.float32, interpret=False):
    M, K = a.shape
    _, N = b.shape
    tm = min(tm, M)
    tn = min(tn, N)
    assert M % tm == 0 and N % tn == 0, (M, N, tm, tn)
    return pl.pallas_call(
        _mm_kernel,
        interpret=interpret,
        grid=(M // tm, N // tn),
        in_specs=[pl.BlockSpec((tm, K), lambda i, j: (i, 0)),
                  pl.BlockSpec((K, tn), lambda i, j: (0, j))],
        out_specs=pl.BlockSpec((tm, tn), lambda i, j: (i, j)),
        out_shape=jax.ShapeDtypeStruct((M, N), out_dtype),
        compiler_params=pltpu.CompilerParams(
            dimension_semantics=("parallel", "parallel"),
            vmem_limit_bytes=VMEM_LIMIT_BYTES),
    )(a, b)


MASK_FILL = -1e30
MAX_FLOOR = -1e29
SLC_STEP = 1024
SLC_STEP_BLOCKS = SLC_STEP // SLC_BLOCK
WIN_SPAN = WINDOW + 2 * Q_CHUNK
LANES = 128
V_ROWS = HEAD_DIM + 16


def _nsa_kernel(qT_ref, kc_ref, vcT_ref, ks_ref, vsT_ref, kw_ref, vwT_ref, g_ref, ovT_ref, o_ref, bias_ref, st_ref,
                *, n_cmp, n_slc, n_sel):
    f32 = jnp.float32
    bf16 = jnp.bfloat16
    c = pl.program_id(2)
    q = qT_ref[0, 0, 0]
    ncols = q.shape[1]
    ncp = kc_ref.shape[2]
    lane = lax.broadcasted_iota(jnp.int32, (1, ncols), 1)
    tpos = c * Q_CHUNK + (lane & (Q_CHUNK - 1))

    s = jnp.dot(kc_ref[0, 0], q, preferred_element_type=f32)
    n_idx = lax.broadcasted_iota(jnp.int32, (ncp, 1), 0)
    vis = (n_idx * CMP_STRIDE + (CMP_BLOCK - 1) <= tpos) & (n_idx < n_cmp)
    s = jnp.where(vis, s, MASK_FILL)
    m = jnp.max(s, axis=0, keepdims=True)
    p = jnp.where(vis, jnp.exp(s - m), 0.0)
    l = jnp.sum(p, axis=0, keepdims=True)
    p = p / jnp.where(l > 0.0, l, 1.0)
    p_hi = p.astype(bf16)
    o_cmp = jnp.dot(vcT_ref[0, 0], p_hi, preferred_element_type=f32)

    p_lo = (p - p_hi.astype(f32)).astype(bf16)
    ov = ovT_ref[...]
    imp4 = (jnp.dot(ov, p_hi, preferred_element_type=f32)
            + jnp.dot(ov, p_lo, preferred_element_type=f32))
    z = imp4[:, :LANES] + imp4[:, LANES:]
    imp = z + pltpu.roll(z, Q_CHUNK, axis=1)

    base = pl.multiple_of((c // 2) * LANES, LANES)
    st = jnp.dot(kw_ref[0, 0, pl.ds(base, WIN_SPAN), :], q, preferred_element_type=f32)
    kpos = base - WINDOW + lax.broadcasted_iota(jnp.int32, (WIN_SPAN, 1), 0)
    d = tpos - kpos
    st = jnp.where((d >= 0) & (d < WINDOW) & (kpos >= 0), st, MASK_FILL)
    p_w = jnp.exp(st - jnp.max(st, axis=0, keepdims=True))
    pv_w = jnp.dot(vwT_ref[0, 0, :, pl.ds(base, WIN_SPAN)], p_w.astype(bf16), preferred_element_type=f32)
    o_win = pv_w[:HEAD_DIM] / pv_w[HEAD_DIM:HEAD_DIM + 1]

    dbase = pl.multiple_of((c // 2) * LANES, LANES)
    st = jnp.dot(ks_ref[0, 0, pl.ds(dbase, LANES), :HEAD_DIM], q, preferred_element_type=f32)
    kpos = dbase + lax.broadcasted_iota(jnp.int32, (LANES, 1), 0)
    st = jnp.where((kpos >= c * SLC_BLOCK) & (kpos <= tpos), st, MASK_FILL)
    m_d = jnp.max(st, axis=0, keepdims=True)
    pv_d = jnp.dot(vsT_ref[0, 0, :, pl.ds(dbase, LANES)], jnp.exp(st - m_d).astype(bf16),
                   preferred_element_type=f32)

    jidx = lax.broadcasted_iota(jnp.int32, (n_slc, LANES), 0)
    forced = (jidx == 0) | (jidx == c) | (jidx == c - 1)
    work = jnp.where(forced | (jidx > c), -2.0, imp)
    sel = forced
    for _ in range(n_sel - 3):
        mx = jnp.max(work, axis=0, keepdims=True)
        first = jnp.min(jnp.where(work == mx, jidx, n_slc), axis=0, keepdims=True)
        hit = jidx == first
        sel = sel | hit
        work = jnp.where(hit, -2.0, work)
    early = (jnp.zeros_like(jidx) + c) < n_sel
    sel = (sel | early) & (jidx < c)
    bias = jnp.where(sel, 0.0, MASK_FILL)
    bias_ref[...] = jnp.concatenate([bias, bias], axis=1).astype(bf16)

    zeros_tail = jnp.zeros((LANES - HEAD_DIM - SLC_STEP_BLOCKS, ncols), bf16)

    def scores(step):
        rows = bias_ref[pl.ds(pl.multiple_of(step * SLC_STEP_BLOCKS, SLC_STEP_BLOCKS), SLC_STEP_BLOCKS), :]
        rhs = jnp.concatenate([q, rows, zeros_tail], axis=0)
        off = pl.multiple_of(step * SLC_STEP, SLC_STEP)
        st = jnp.dot(ks_ref[0, 0, pl.ds(off, SLC_STEP), :], rhs, preferred_element_type=f32)
        st_ref[step & 1] = st
        return jnp.max(st, axis=0, keepdims=True)

    def accumulate(step, mx, m, acc):
        m_new = jnp.maximum(m, mx)
        off = pl.multiple_of(step * SLC_STEP, SLC_STEP)
        pv = jnp.dot(vsT_ref[0, 0, :, pl.ds(off, SLC_STEP)], jnp.exp(st_ref[step & 1] - m_new).astype(bf16),
                     preferred_element_type=f32)
        return m_new, jnp.exp(m - m_new) * acc + pv

    n_steps = jnp.maximum((c + SLC_STEP_BLOCKS - 1) // SLC_STEP_BLOCKS, 1)

    def body(i, carry):
        mx, m, acc = carry
        m, acc = accumulate(i, mx, m, acc)
        return scores(i + 1), m, acc

    carry = (scores(0), jnp.full((1, ncols), MAX_FLOOR, f32), jnp.zeros((V_ROWS, ncols), f32))
    mx, m_s, acc = lax.fori_loop(0, n_steps - 1, body, carry)
    m_s, acc = accumulate(n_steps - 1, mx, m_s, acc)
    m_t = jnp.maximum(m_s, m_d)
    tot = jnp.exp(m_s - m_t) * acc + jnp.exp(m_d - m_t) * pv_d
    o_slc = tot[:HEAD_DIM] / tot[HEAD_DIM:HEAD_DIM + 1]

    g = g_ref[0, 0, 0]
    o = g[0:1] * o_cmp + g[1:2] * o_slc + g[2:3] * o_win
    o_ref[0, 0, 0] = o.astype(o_ref.dtype)


def nsa_attention_pallas(q2, kvs, gates2, cmp_pos, cmp_w1, cmp_w2, B, L, interpret=False):
    f32, bf16 = jnp.float32, jnp.bfloat16
    G, H, hd = N_KV, HPG, HEAD_DIM
    C = L // Q_CHUNK
    n16 = L // CMP_STRIDE
    n_cmp = (L - CMP_BLOCK) // CMP_STRIDE + 1
    ncp = -(-n_cmp // LANES) * LANES
    n_slc = L // SLC_BLOCK
    n_sel = min(SLC_TOPK, n_slc)
    k_cmp, v_cmp, k_slc, v_slc, k_win, v_win = kvs

    def compress(kv2, pos, w1, w2):
        x = kv2.reshape(B, n16, CMP_STRIDE, G, hd).transpose(0, 3, 1, 2, 4).reshape(B * G * n16, CMP_STRIDE * hd)
        half = CMP_STRIDE * hd
        wcat = jnp.concatenate([w1[:half], w1[half:]], axis=1)
        ab = pmm(x, wcat, interpret=interpret).reshape(B, G, n16, 2 * CMP_HIDDEN)
        posb = pmm(jnp.pad(pos.reshape(1, -1), ((0, 7), (0, 0))), w1, interpret=interpret)[0]
        hid = ab[:, :, :-1, :CMP_HIDDEN] + ab[:, :, 1:, CMP_HIDDEN:] + posb
        hid = jnp.pad(jax.nn.gelu(hid), ((0, 0), (0, 0), (0, ncp - n_cmp), (0, 0)))
        out = pmm(hid.reshape(B * G * ncp, CMP_HIDDEN), w2, interpret=interpret)
        return out.reshape(B, G, ncp, hd)

    kc = compress(k_cmp, cmp_pos[0], cmp_w1[0], cmp_w2[0]).astype(bf16)
    vcT = compress(v_cmp, cmp_pos[1], cmp_w1[1], cmp_w2[1]).astype(bf16).transpose(0, 1, 3, 2)

    def key_major(a):
        return a.astype(bf16).reshape(B, L, G, hd).transpose(0, 2, 1, 3)

    def dim_major(a):
        return a.astype(bf16).reshape(B, L, G, hd).transpose(0, 2, 3, 1)

    def with_ones_row(vT):
        n = vT.shape[-1]
        extra = jnp.zeros((B, G, V_ROWS - hd, n), bf16).at[:, :, 0, :].set(1.0)
        return jnp.concatenate([vT, extra], axis=2)

    blk_onehot = jax.nn.one_hot((jnp.arange(L) // SLC_BLOCK) % SLC_STEP_BLOCKS, LANES - hd, dtype=bf16)
    ks = jnp.concatenate([key_major(k_slc), jnp.broadcast_to(blk_onehot, (B, G, L, LANES - hd))], axis=-1)
    vsT = with_ones_row(dim_major(v_slc))
    kw = jnp.pad(key_major(k_win), ((0, 0), (0, 0), (WINDOW, 0), (0, 0)))
    vwT = with_ones_row(jnp.pad(dim_major(v_win), ((0, 0), (0, 0), (0, 0), (WINDOW, 0))))
    qT = q2.astype(bf16).reshape(B, C, Q_CHUNK, G, H, hd)
    qT = qT.transpose(0, 3, 1, 5, 4, 2).reshape(B, G, C, hd, H * Q_CHUNK)
    gT = gates2.astype(f32).reshape(B, C, Q_CHUNK, 3, G, H).transpose(0, 4, 1, 3, 5, 2).reshape(B, G, C, 3, H * Q_CHUNK)
    c_start = jnp.arange(ncp)[None, :] * CMP_STRIDE
    s_start = jnp.arange(n_slc)[:, None] * SLC_BLOCK
    ovT = ((c_start < s_start + SLC_BLOCK) & (c_start + CMP_BLOCK > s_start)
           & (jnp.arange(ncp)[None, :] < n_cmp)).astype(bf16)

    ncols = H * Q_CHUNK
    bg = lambda b, g, c: (b, g, 0, 0)
    bgc = lambda b, g, c: (b, g, c, 0, 0)
    oT = pl.pallas_call(
        functools.partial(_nsa_kernel, n_cmp=n_cmp, n_slc=n_slc, n_sel=n_sel),
        grid=(B, G, C),
        in_specs=[pl.BlockSpec((1, 1, 1, hd, ncols), bgc),
                  pl.BlockSpec((1, 1, ncp, hd), bg),
                  pl.BlockSpec((1, 1, hd, ncp), bg),
                  pl.BlockSpec((1, 1, L, LANES), bg),
                  pl.BlockSpec((1, 1, V_ROWS, L), bg),
                  pl.BlockSpec((1, 1, L + WINDOW, hd), bg),
                  pl.BlockSpec((1, 1, V_ROWS, L + WINDOW), bg),
                  pl.BlockSpec((1, 1, 1, 3, ncols), bgc),
                  pl.BlockSpec((n_slc, ncp), lambda b, g, c: (0, 0))],
        out_specs=pl.BlockSpec((1, 1, 1, hd, ncols), bgc),
        out_shape=jax.ShapeDtypeStruct((B, G, C, hd, ncols), bf16),
        scratch_shapes=[pltpu.VMEM((n_slc, ncols), bf16), pltpu.VMEM((2, SLC_STEP, ncols), f32)],
        compiler_params=pltpu.CompilerParams(
            dimension_semantics=("parallel", "parallel", "arbitrary"),
            vmem_limit_bytes=VMEM_LIMIT_BYTES),
        interpret=interpret,
        name="nsa_attention",
    )(qT, kc, vcT, ks, vsT, kw, vwT, gT, ovT)
    o = oT.reshape(B, G, C, hd, H, Q_CHUNK).transpose(0, 2, 5, 1, 4, 3)
    return o.reshape(B * L, G * H * hd)


S5_CHUNK = 32


def _s5_state_kernel(x_ref, ws_ref, s_ref):
    s_ref[0] = jnp.dot(x_ref[0].astype(jnp.bfloat16), ws_ref[0], preferred_element_type=jnp.float32)


def _s5_scan_kernel(sre_ref, sim_ref, are_ref, aim_ref, hre_ref, him_ref, *, chunks_per_batch):
    a_re = are_ref[...]
    a_im = aim_ref[...]
    n_batch = sre_ref.shape[0] // chunks_per_batch

    def step(c, carry):
        new = []
        for b in range(n_batch):
            h_re, h_im = carry[b]
            row = b * chunks_per_batch + c
            hre_ref[row] = h_re
            him_ref[row] = h_im
            new.append((a_re * h_re - a_im * h_im + sre_ref[row],
                        a_re * h_im + a_im * h_re + sim_ref[row]))
        return tuple(new)

    zero = jnp.zeros(a_re.shape, jnp.float32)
    lax.fori_loop(0, chunks_per_batch, step, tuple((zero, zero) for _ in range(n_batch)))


def _s5_out_kernel(x_ref, hp_ref, m_ref, wo_ref, d_ref, y_ref):
    x = x_ref[0]
    y = jnp.dot(x.astype(jnp.bfloat16), m_ref[0], preferred_element_type=jnp.float32)
    y = y + jnp.dot(hp_ref[0].astype(jnp.bfloat16), wo_ref[0], preferred_element_type=jnp.float32)
    y_ref[0] = y + x * d_ref[0]


def s5_scan_pallas(u2, lam_re, lam_im, log_dt, b_re, b_im, c_re, c_im, d_skip, B, L, interpret=False):
    f32, bf16 = jnp.float32, jnp.bfloat16
    G, P, H, T = SSM_GROUPS, SSM_STATE, SSM_GROUP, S5_CHUNK
    hi = lax.Precision.HIGHEST
    NC = B * L // T
    dt = jnp.exp(log_dt.astype(f32))[:, None]
    lr, li = lam_re.astype(f32), lam_im.astype(f32)
    mag = jnp.exp(lr * dt)
    ab_re, ab_im = mag * jnp.cos(li * dt), mag * jnp.sin(li * dt)
    den = lr * lr + li * li
    nr = ab_re - 1.0
    cr = (nr * lr + ab_im * li) / den
    cim = (ab_im * lr - nr * li) / den
    br, bim = b_re.astype(f32), b_im.astype(f32)
    bb_re = cr[..., None] * br - cim[..., None] * bim
    bb_im = cr[..., None] * bim + cim[..., None] * br
    k = jnp.arange(T + 1, dtype=f32)[:, None, None]
    pw_mag = jnp.exp(k * (lr * dt))
    pw_re = pw_mag * jnp.cos(k * (li * dt))
    pw_im = pw_mag * jnp.sin(k * (li * dt))
    cre, cimg = c_re.astype(f32), c_im.astype(f32)
    ab_r = pw_re[:T, :, :, None] * bb_re - pw_im[:T, :, :, None] * bb_im
    ab_i = pw_re[:T, :, :, None] * bb_im + pw_im[:T, :, :, None] * bb_re
    kern = (jnp.einsum('gop,tgpi->tgoi', cre, ab_r, precision=hi)
            - jnp.einsum('gop,tgpi->tgoi', cimg, ab_i, precision=hi))
    tt = jnp.arange(T)
    lag = tt[None, :] - tt[:, None]
    m = jnp.where((lag >= 0)[:, :, None, None, None], kern[jnp.clip(lag, 0, T - 1)], 0.0)
    m = m.transpose(2, 0, 4, 1, 3).reshape(G, T * H, T * H).astype(bf16)
    ws_re = ab_r[::-1].transpose(1, 0, 3, 2).reshape(G, T * H, P)
    ws_im = ab_i[::-1].transpose(1, 0, 3, 2).reshape(G, T * H, P)
    ws = jnp.concatenate([ws_re, ws_im], axis=-1).astype(bf16)
    ca_re = cre[None] * pw_re[1:, :, None, :] - cimg[None] * pw_im[1:, :, None, :]
    ca_im = cre[None] * pw_im[1:, :, None, :] + cimg[None] * pw_re[1:, :, None, :]
    wo = jnp.concatenate([ca_re.transpose(1, 3, 0, 2).reshape(G, P, T * H),
                          -ca_im.transpose(1, 3, 0, 2).reshape(G, P, T * H)], axis=1).astype(bf16)
    dtile = jnp.tile(d_skip.astype(f32), (1, T)).reshape(G, 1, T * H)

    xg = u2.astype(f32).reshape(NC, T, G, H).transpose(2, 0, 1, 3).reshape(G, NC, T * H)
    cparams = pltpu.CompilerParams(dimension_semantics=("parallel",), vmem_limit_bytes=VMEM_LIMIT_BYTES)
    s = pl.pallas_call(
        _s5_state_kernel, grid=(G,),
        in_specs=[pl.BlockSpec((1, NC, T * H), lambda g: (g, 0, 0)),
                  pl.BlockSpec((1, T * H, 2 * P), lambda g: (g, 0, 0))],
        out_specs=pl.BlockSpec((1, NC, 2 * P), lambda g: (g, 0, 0)),
        out_shape=jax.ShapeDtypeStruct((G, NC, 2 * P), f32),
        compiler_params=cparams, interpret=interpret, name="s5_chunk_state",
    )(xg, ws)
    sp = s.reshape(G, NC, 2, P).transpose(1, 2, 0, 3).reshape(NC, 2, G // 2, 2 * P)
    a_re = pw_re[T].reshape(G // 2, 2 * P)
    a_im = pw_im[T].reshape(G // 2, 2 * P)
    full = lambda shape: pl.BlockSpec(shape, lambda i: (0,) * len(shape))
    st_shape = (NC, G // 2, 2 * P)
    h_re, h_im = pl.pallas_call(
        functools.partial(_s5_scan_kernel, chunks_per_batch=L // T), grid=(1,),
        in_specs=[full(st_shape), full(st_shape), full((G // 2, 2 * P)), full((G // 2, 2 * P))],
        out_specs=[full(st_shape), full(st_shape)],
        out_shape=[jax.ShapeDtypeStruct(st_shape, f32)] * 2,
        compiler_params=pltpu.CompilerParams(dimension_semantics=("arbitrary",), vmem_limit_bytes=VMEM_LIMIT_BYTES),
        interpret=interpret, name="s5_state_scan",
    )(sp[:, 0], sp[:, 1], a_re, a_im)
    hp = jnp.stack([h_re, h_im], axis=1).reshape(NC, 2, G, P).transpose(2, 0, 1, 3).reshape(G, NC, 2 * P)
    y = pl.pallas_call(
        _s5_out_kernel, grid=(G,),
        in_specs=[pl.BlockSpec((1, NC, T * H), lambda g: (g, 0, 0)),
                  pl.BlockSpec((1, NC, 2 * P), lambda g: (g, 0, 0)),
                  pl.BlockSpec((1, T * H, T * H), lambda g: (g, 0, 0)),
                  pl.BlockSpec((1, 2 * P, T * H), lambda g: (g, 0, 0)),
                  pl.BlockSpec((1, 1, T * H), lambda g: (g, 0, 0))],
        out_specs=pl.BlockSpec((1, NC, T * H), lambda g: (g, 0, 0)),
        out_shape=jax.ShapeDtypeStruct((G, NC, T * H), f32),
        compiler_params=cparams, interpret=interpret, name="s5_chunk_out",
    )(xg, hp, m, wo, dtile)
    return y.reshape(G, NC, T, H).transpose(1, 2, 0, 3).reshape(B * L, G * H)


MOE_TILE = 512
MOE_SEG = 32
MOE_SLOTS = TOPK_IN_GROUP * MOE_TILE + N_EXPERTS * MOE_SEG
ROUTE_E1, ROUTE_E2, ROUTE_W1, ROUTE_W2 = 0, 1, 2, 3


def _split_bf16(a):
    hi = a.astype(jnp.bfloat16)
    return hi, (a - hi.astype(jnp.float32)).astype(jnp.bfloat16)


def _moe_route_kernel(x_ref, g_ref, sc_ref, sh_ref, wr_ref, br_ref, h_ref, route_ref, routeT_ref, cnt_ref):
    f32 = jnp.float32
    x = x_ref[...]
    y = x * lax.rsqrt(jnp.mean(x * x, axis=-1, keepdims=True) + EPS)
    h = y * g_ref[...] * (1.0 + sc_ref[0]) + sh_ref[0]
    h_ref[...] = h.astype(h_ref.dtype)
    h_hi, h_lo = _split_bf16(h)
    w_hi, w_lo = _split_bf16(wr_ref[...])
    lg = (jnp.dot(h_hi, w_hi, preferred_element_type=f32) + jnp.dot(h_hi, w_lo, preferred_element_type=f32)
          + jnp.dot(h_lo, w_hi, preferred_element_type=f32) + br_ref[...])
    lane = lax.broadcasted_iota(jnp.int32, lg.shape, 1)
    ninf = -jnp.inf
    gmask = lane < N_GROUPS
    gmax = jnp.max(jnp.where(gmask, lg, ninf), axis=1, keepdims=True)
    gsum = jnp.sum(jnp.where(gmask, jnp.exp(lg - gmax), 0.0), axis=1, keepdims=True)
    g_idx = jnp.min(jnp.where(gmask & (lg == gmax), lane, LANES), axis=1, keepdims=True)
    g_p = 1.0 / gsum
    lo = N_GROUPS + EXP_PER_GROUP * g_idx
    emask = (lane >= lo) & (lane < lo + EXP_PER_GROUP)
    el = jnp.where(emask, lg, ninf)
    emax = jnp.max(el, axis=1, keepdims=True)
    i1 = jnp.min(jnp.where(el == emax, lane, LANES), axis=1, keepdims=True)
    el2 = jnp.where(lane == i1, ninf, el)
    emax2 = jnp.max(el2, axis=1, keepdims=True)
    i2 = jnp.min(jnp.where(el2 == emax2, lane, LANES), axis=1, keepdims=True)
    r = jnp.exp(emax2 - emax)
    w1 = g_p / (1.0 + r)
    w2 = g_p * r / (1.0 + r)
    e1 = (i1 - N_GROUPS).astype(f32)
    e2 = (i2 - N_GROUPS).astype(f32)
    route = jnp.where(lane == ROUTE_E1, e1, jnp.where(lane == ROUTE_E2, e2,
                      jnp.where(lane == ROUTE_W1, w1, jnp.where(lane == ROUTE_W2, w2, 0.0))))
    route_ref[...] = route
    routeT_ref[...] = route.T[:8, :]
    cnt = jnp.sum(jnp.where((lane == i1) | (lane == i2), 1.0, 0.0), axis=0, keepdims=True)
    cnt_ref[0] = jnp.broadcast_to(cnt, (8, LANES))


def _moe_ffn_kernel(seg_start_ref, seg_nblk_ref, h_ref, route_ref, routeT_ref, lrow_ref, lcol_ref, wgu_ref, wd_ref,
                    x_ref, gate_ref, fg_ref, o_ref, xs_ref, yw_ref, ws_ref, *, final_norm):
    f32, bf16 = jnp.float32, jnp.bfloat16
    t = pl.program_id(0)
    e = pl.program_id(1)
    tm = h_ref.shape[0]

    @pl.when(e == 0)
    def _dispatch():
        rT = routeT_ref[...]
        e1r, e2r = rT[ROUTE_E1:ROUTE_E1 + 1], rT[ROUTE_E2:ROUTE_E2 + 1]
        w1r, w2r = rT[ROUTE_W1:ROUTE_W1 + 1], rT[ROUTE_W2:ROUTE_W2 + 1]
        eio = lax.broadcasted_iota(jnp.int32, (N_EXPERTS, tm), 0).astype(f32)
        oh1 = eio == e1r
        oh2 = eio == e2r
        oh = jnp.where(oh1 | oh2, 1.0, 0.0).astype(bf16)
        before = (lax.broadcasted_iota(jnp.int32, (tm, tm), 0)
                  < lax.broadcasted_iota(jnp.int32, (tm, tm), 1))
        rank = jnp.dot(oh, jnp.where(before, 1.0, 0.0).astype(bf16), preferred_element_type=f32)
        pos = lcol_ref[0][:, :1] + rank
        d1 = jnp.sum(jnp.where(oh1, pos, 0.0), axis=0, keepdims=True)
        d2 = jnp.sum(jnp.where(oh2, pos, 0.0), axis=0, keepdims=True)
        slot = lax.broadcasted_iota(jnp.int32, (MOE_SLOTS, tm), 0).astype(f32)
        m1 = slot == d1
        m2 = slot == d2
        perm = jnp.where(m1 | m2, 1.0, 0.0).astype(bf16)
        xs_ref[...] = jnp.dot(perm, h_ref[...], preferred_element_type=f32).astype(bf16)
        ws_ref[...] = jnp.sum(jnp.where(m1, w1r, 0.0) + jnp.where(m2, w2r, 0.0), axis=1, keepdims=True)
        yw_ref[...] = jnp.zeros(yw_ref.shape, yw_ref.dtype)

    start = seg_start_ref[t * N_EXPERTS + e]
    nblk = seg_nblk_ref[t * N_EXPERTS + e]

    def block(i, _):
        r0 = pl.multiple_of(start + i * MOE_SEG, MOE_SEG)
        xb = xs_ref[pl.ds(r0, MOE_SEG), :]
        gu = jnp.dot(xb, wgu_ref[0], preferred_element_type=f32)
        a = jax.nn.silu(gu[:, :D_EXPERT]) * gu[:, D_EXPERT:]
        y = jnp.dot(a.astype(bf16), wd_ref[0], preferred_element_type=f32)
        yw_ref[pl.ds(r0, MOE_SEG), :] = (y * ws_ref[pl.ds(r0, MOE_SEG), :]).astype(bf16)
        return 0

    lax.fori_loop(0, nblk, block, 0)

    @pl.when(e == N_EXPERTS - 1)
    def _combine():
        r = route_ref[...]
        e1c, e2c = r[:, ROUTE_E1:ROUTE_E1 + 1], r[:, ROUTE_E2:ROUTE_E2 + 1]
        lane = lax.broadcasted_iota(jnp.int32, (tm, LANES), 1).astype(f32)
        oh1 = lane == e1c
        oh2 = lane == e2c
        oh = jnp.where(oh1 | oh2, 1.0, 0.0).astype(bf16)
        after = (lax.broadcasted_iota(jnp.int32, (tm, tm), 1)
                 < lax.broadcasted_iota(jnp.int32, (tm, tm), 0))
        rank = jnp.dot(jnp.where(after, 1.0, 0.0).astype(bf16), oh, preferred_element_type=f32)
        pos = lrow_ref[0][:1, :] + rank
        d1 = jnp.sum(jnp.where(oh1, pos, 0.0), axis=1, keepdims=True)
        d2 = jnp.sum(jnp.where(oh2, pos, 0.0), axis=1, keepdims=True)
        slot = lax.broadcasted_iota(jnp.int32, (tm, MOE_SLOTS), 1).astype(f32)
        comb = jnp.where((slot == d1) | (slot == d2), 1.0, 0.0).astype(bf16)
        moe = jnp.dot(comb, yw_ref[...], preferred_element_type=f32)
        xn = x_ref[...] + gate_ref[0] * moe
        if final_norm:
            xn = xn * lax.rsqrt(jnp.mean(xn * xn, axis=-1, keepdims=True) + EPS) * fg_ref[...]
        o_ref[...] = xn


def moe_layer_pallas(x2, norm_g, sc2, sh2, g2, w_group, b_group, w_expert, b_expert, w_gate, w_up, w_down,
                     final_g, final_norm, B, L, interpret=False):
    f32, bf16 = jnp.float32, jnp.bfloat16
    N, D = x2.shape
    tm = MOE_TILE
    n_tiles = N // tm
    tpb = L // tm
    wr = jnp.zeros((D, LANES), f32).at[:, :N_GROUPS].set(w_group).at[:, N_GROUPS:N_GROUPS + N_EXPERTS].set(w_expert)
    br = jnp.zeros((1, LANES), f32).at[0, :N_GROUPS].set(b_group).at[0, N_GROUPS:N_GROUPS + N_EXPERTS].set(b_expert)
    row = lambda v: v.reshape(1, D).astype(f32)
    per_batch = lambda v: v.reshape(B, 1, D).astype(f32)
    cp = pltpu.CompilerParams(dimension_semantics=("parallel",), vmem_limit_bytes=VMEM_LIMIT_BYTES)
    h, route, routeT, cnt = pl.pallas_call(
        _moe_route_kernel, grid=(n_tiles,),
        in_specs=[pl.BlockSpec((tm, D), lambda i: (i, 0)),
                  pl.BlockSpec((1, D), lambda i: (0, 0)),
                  pl.BlockSpec((1, 1, D), lambda i: (i // tpb, 0, 0)),
                  pl.BlockSpec((1, 1, D), lambda i: (i // tpb, 0, 0)),
                  pl.BlockSpec((D, LANES), lambda i: (0, 0)),
                  pl.BlockSpec((1, LANES), lambda i: (0, 0))],
        out_specs=[pl.BlockSpec((tm, D), lambda i: (i, 0)),
                   pl.BlockSpec((tm, LANES), lambda i: (i, 0)),
                   pl.BlockSpec((8, tm), lambda i: (0, i)),
                   pl.BlockSpec((1, 8, LANES), lambda i: (i, 0, 0))],
        out_shape=[jax.ShapeDtypeStruct((N, D), bf16), jax.ShapeDtypeStruct((N, LANES), f32),
                   jax.ShapeDtypeStruct((8, N), f32), jax.ShapeDtypeStruct((n_tiles, 8, LANES), f32)],
        compiler_params=cp, interpret=interpret, name="moe_route",
    )(x2, row(norm_g), per_batch(sc2), per_batch(sh2), wr, br)
    counts = cnt[:, 0, N_GROUPS:N_GROUPS + N_EXPERTS].astype(jnp.int32)
    nblk = (counts + MOE_SEG - 1) // MOE_SEG
    starts = (jnp.cumsum(nblk, axis=1) - nblk) * MOE_SEG
    starts_f = starts.astype(f32)
    lrow = jnp.zeros((n_tiles, 8, LANES), f32).at[:, :, :N_EXPERTS].set(starts_f[:, None, :])
    lcol = jnp.broadcast_to(starts_f[:, :, None], (n_tiles, N_EXPERTS, LANES))
    wgu = jnp.concatenate([w_gate, w_up], axis=-1).astype(bf16)
    wd = w_down.astype(bf16)
    tile = lambda t, e, *_: (t, 0)
    out = pl.pallas_call(
        functools.partial(_moe_ffn_kernel, final_norm=final_norm),
        grid_spec=pltpu.PrefetchScalarGridSpec(
            num_scalar_prefetch=2, grid=(n_tiles, N_EXPERTS),
            in_specs=[pl.BlockSpec((tm, D), tile),
                      pl.BlockSpec((tm, LANES), tile),
                      pl.BlockSpec((8, tm), lambda t, e, *_: (0, t)),
                      pl.BlockSpec((1, 8, LANES), lambda t, e, *_: (t, 0, 0)),
                      pl.BlockSpec((1, N_EXPERTS, LANES), lambda t, e, *_: (t, 0, 0)),
                      pl.BlockSpec((1, D, 2 * D_EXPERT), lambda t, e, *_: (e, 0, 0)),
                      pl.BlockSpec((1, D_EXPERT, D), lambda t, e, *_: (e, 0, 0)),
                      pl.BlockSpec((tm, D), tile),
                      pl.BlockSpec((1, 1, D), lambda t, e, *_: (t // tpb, 0, 0)),
                      pl.BlockSpec((1, D), lambda t, e, *_: (0, 0))],
            out_specs=pl.BlockSpec((tm, D), tile),
            scratch_shapes=[pltpu.VMEM((MOE_SLOTS, D), bf16), pltpu.VMEM((MOE_SLOTS, D), bf16),
                            pltpu.VMEM((MOE_SLOTS, 1), f32)]),
        out_shape=jax.ShapeDtypeStruct((N, D), f32),
        compiler_params=pltpu.CompilerParams(dimension_semantics=("parallel", "arbitrary"),
                                             vmem_limit_bytes=VMEM_LIMIT_BYTES),
        interpret=interpret, name="moe_ffn",
    )(starts.reshape(-1), nblk.reshape(-1), h, route, routeT, lrow, lcol, wgu, wd, x2, per_batch(g2), row(final_g))
    return out


ROW_TILE = 256
GATE_LANES = 128


def _in_proj_kernel(x_ref, g_ref, sc_ref, sh_ref, wq_ref, wkv_ref, wg_ref, wu_ref, wm_ref,
                    q_ref, kv_ref, ng_ref, u_ref, mg_ref):
    f32 = jnp.float32
    x = x_ref[...]
    y = x * lax.rsqrt(jnp.mean(x * x, axis=-1, keepdims=True) + EPS)
    h = (y * g_ref[...] * (1.0 + sc_ref[0]) + sh_ref[0]).astype(jnp.bfloat16)
    q = jnp.dot(h, wq_ref[...], preferred_element_type=f32)
    q_ref[...] = (q * HEAD_DIM ** -0.5).astype(q_ref.dtype)
    kv_ref[...] = jnp.dot(h, wkv_ref[...], preferred_element_type=f32).astype(kv_ref.dtype)
    ng_ref[...] = jax.nn.sigmoid(jnp.dot(h, wg_ref[...], preferred_element_type=f32))
    u_ref[...] = jnp.dot(h, wu_ref[...], preferred_element_type=f32)
    mg_ref[...] = jax.nn.sigmoid(jnp.dot(h, wm_ref[...], preferred_element_type=f32)).astype(mg_ref.dtype)


def in_proj_pallas(x2, norm_g, sc1, sh1, w_in, B, L, interpret=False):
    f32, bf16 = jnp.float32, jnp.bfloat16
    N, D = x2.shape
    tm = ROW_TILE
    tpb = L // tm
    o1 = ATTN_WIDTH
    o2 = o1 + 6 * KV_WIDTH
    o3 = o2 + 3 * N_HEADS
    o4 = o3 + SSM_WIDTH
    wb = w_in.astype(bf16)
    wq, wkv, wu, wm = wb[:, :o1], wb[:, o1:o2], wb[:, o3:o4], wb[:, o4:]
    wg = jnp.pad(wb[:, o2:o3], ((0, 0), (0, GATE_LANES - 3 * N_HEADS)))
    const = lambda shape: pl.BlockSpec(shape, lambda i: (0,) * len(shape))
    rows = lambda w: pl.BlockSpec((tm, w), lambda i: (i, 0))
    per_batch = pl.BlockSpec((1, 1, D), lambda i: (i // tpb, 0, 0))
    return pl.pallas_call(
        _in_proj_kernel, grid=(N // tm,),
        in_specs=[rows(D), const((1, D)), per_batch, per_batch,
                  const(wq.shape), const(wkv.shape), const(wg.shape), const(wu.shape), const(wm.shape)],
        out_specs=[rows(o1), rows(o2 - o1), rows(GATE_LANES), rows(SSM_WIDTH), rows(2 * D)],
        out_shape=[jax.ShapeDtypeStruct((N, o1), bf16), jax.ShapeDtypeStruct((N, o2 - o1), bf16),
                   jax.ShapeDtypeStruct((N, GATE_LANES), f32), jax.ShapeDtypeStruct((N, SSM_WIDTH), f32),
                   jax.ShapeDtypeStruct((N, 2 * D), bf16)],
        compiler_params=pltpu.CompilerParams(dimension_semantics=("parallel",), vmem_limit_bytes=VMEM_LIMIT_BYTES),
        interpret=interpret, name="in_proj",
    )(x2, norm_g.reshape(1, D).astype(f32), sc1.reshape(B, 1, D).astype(f32), sh1.reshape(B, 1, D).astype(f32),
      wq, wkv, wg, wu, wm)


def _merge_kernel(attn_ref, y_ref, mg_ref, x_ref, g1_ref, wglu_ref, wso_ref, wao_ref, wout_ref, o_ref):
    f32, bf16 = jnp.float32, jnp.bfloat16
    D = x_ref.shape[1]
    y = jax.nn.gelu(y_ref[...])
    ssm = y * jax.nn.sigmoid(jnp.dot(y.astype(bf16), wglu_ref[...], preferred_element_type=f32))
    a = jnp.dot(attn_ref[...], wao_ref[...], preferred_element_type=f32)
    s = jnp.dot(ssm.astype(bf16), wso_ref[...], preferred_element_type=f32)
    mg = mg_ref[...].astype(f32)
    merged = mg[:, :D] * a + mg[:, D:] * s
    out = jnp.dot(merged.astype(bf16), wout_ref[...], preferred_element_type=f32)
    o_ref[...] = x_ref[...] + g1_ref[0] * out


def merge_pallas(attn, y, mg, x2, g1, w_glu, ssm_w_o, nsa_w_o, w_out, B, L, interpret=False):
    bf16 = jnp.bfloat16
    N, D = x2.shape
    tm = ROW_TILE
    tpb = L // tm
    const = lambda shape: pl.BlockSpec(shape, lambda i: (0,) * len(shape))
    rows = lambda w: pl.BlockSpec((tm, w), lambda i: (i, 0))
    return pl.pallas_call(
        _merge_kernel, grid=(N // tm,),
        in_specs=[rows(ATTN_WIDTH), rows(SSM_WIDTH), rows(2 * D), rows(D),
                  pl.BlockSpec((1, 1, D), lambda i: (i // tpb, 0, 0)),
                  const(w_glu.shape), const(ssm_w_o.shape), const(nsa_w_o.shape), const(w_out.shape)],
        out_specs=rows(D),
        out_shape=jax.ShapeDtypeStruct((N, D), jnp.float32),
        compiler_params=pltpu.CompilerParams(dimension_semantics=("parallel",), vmem_limit_bytes=VMEM_LIMIT_BYTES),
        interpret=interpret, name="mixer_merge",
    )(attn, y, mg, x2, g1.reshape(B, 1, D).astype(jnp.float32),
      w_glu.astype(bf16), ssm_w_o.astype(bf16), nsa_w_o.astype(bf16), w_out.astype(bf16))


def kernel(x, c, ada_w, ada_b, norm1_g, w_in, ssm_lam_re, ssm_lam_im, ssm_log_dt, ssm_b_re, ssm_b_im, ssm_c_re, ssm_c_im, ssm_d, ssm_w_glu, ssm_w_o, nsa_cmp_pos, nsa_cmp_w1, nsa_cmp_w2, nsa_w_o, w_out, norm2_g, moe_w_group, moe_b_group, moe_w_expert, moe_b_expert, moe_w_gate, moe_w_up, moe_w_down, final_g):
    B, L, D = x.shape
    N = B * L
    x2 = x.reshape(N, D)
    cs = jnp.pad(jax.nn.silu(c), ((0, 8 - B), (0, 0)))
    for l in range(DEPTH):
        mod = pmm(cs, ada_w[l])[:B] + ada_b[l]
        sh1, sc1, g1, sh2, sc2, g2 = jnp.split(mod, 6, axis=-1)
        q, kv, ng, u, mg = in_proj_pallas(x2, norm1_g[l], sc1, sh1, w_in[l], B, L)
        kvs = [kv[:, i * KV_WIDTH:(i + 1) * KV_WIDTH] for i in range(6)]
        attn = nsa_attention_pallas(q, kvs, ng[:, :3 * N_HEADS], nsa_cmp_pos[l], nsa_cmp_w1[l], nsa_cmp_w2[l], B, L)
        y = s5_scan_pallas(u, ssm_lam_re[l], ssm_lam_im[l], ssm_log_dt[l], ssm_b_re[l], ssm_b_im[l],
                           ssm_c_re[l], ssm_c_im[l], ssm_d[l], B, L)
        x2 = merge_pallas(attn, y, mg, x2, g1, ssm_w_glu[l], ssm_w_o[l], nsa_w_o[l], w_out[l], B, L)
        x2 = moe_layer_pallas(x2, norm2_g[l], sc2, sh2, g2, moe_w_group[l], moe_b_group[l],
                              moe_w_expert[l], moe_b_expert[l], moe_w_gate[l], moe_w_up[l], moe_w_down[l],
                              final_g, l == DEPTH - 1, B, L)
    return x2.reshape(B, L, D)
```

```python
import functools
import math

import jax
import jax.numpy as jnp
from jax import lax
from jax.experimental import pallas as pl
from jax.experimental.pallas import tpu as pltpu

D_MODEL = 1024
DEPTH = 2
SSM_GROUP = 16
SSM_STATE = 64
SSM_WIDTH = D_MODEL // 2
SSM_GROUPS = SSM_WIDTH // SSM_GROUP
HEAD_DIM = 64
N_HEADS = D_MODEL // HEAD_DIM
N_KV = N_HEADS // 4
HPG = N_HEADS // N_KV
ATTN_WIDTH = N_HEADS * HEAD_DIM
KV_WIDTH = N_KV * HEAD_DIM
CMP_BLOCK = 32
CMP_STRIDE = 16
CMP_HIDDEN = 4 * HEAD_DIM
SLC_BLOCK = 64
SLC_TOPK = 16
WINDOW = 512
Q_CHUNK = 64
FORCED_SCORE = 1e4
N_GROUPS = 4
EXP_PER_GROUP = 8
N_EXPERTS = N_GROUPS * EXP_PER_GROUP
TOPK_IN_GROUP = 2
D_EXPERT = D_MODEL // 4
MOE_BLOCK = 128
EPS = 1e-6
NEG = -1e30

VMEM_LIMIT_BYTES = 48 * 1024 * 1024


def _mm_kernel(a_ref, b_ref, o_ref):
    a = a_ref[...].astype(jnp.bfloat16)
    b = b_ref[...].astype(jnp.bfloat16)
    o_ref[...] = jnp.dot(a, b, preferred_element_type=jnp.float32).astype(o_ref.dtype)


def pmm(a, b, tm=512, tn=512, out_dtype=jnp.float32, interpret=False):
    M, K = a.shape
    _, N = b.shape
    tm = min(tm, M)
    tn = min(tn, N)
    assert M % tm == 0 and N % tn == 0, (M, N, tm, tn)
    return pl.pallas_call(
        _mm_kernel,
        interpret=interpret,
        grid=(M // tm, N // tn),
        in_specs=[pl.BlockSpec((tm, K), lambda i, j: (i, 0)),
                  pl.BlockSpec((K, tn), lambda i, j: (0, j))],
        out_specs=pl.BlockSpec((tm, tn), lambda i, j: (i, j)),
        out_shape=jax.ShapeDtypeStruct((M, N), out_dtype),
        compiler_params=pltpu.CompilerParams(
            dimension_semantics=("parallel", "parallel"),
            vmem_limit_bytes=VMEM_LIMIT_BYTES),
    )(a, b)


MASK_FILL = -1e30
MAX_FLOOR = -1e29
SLC_STEP = 1024
SLC_STEP_BLOCKS = SLC_STEP // SLC_BLOCK
WIN_SPAN = WINDOW + 2 * Q_CHUNK
LANES = 128
V_ROWS = HEAD_DIM + 16
NSA_CHUNKS = 2


def _nsa_kernel(qT_ref, kc_ref, vcT_ref, ks_ref, vsT_ref, kw_ref, vwT_ref, g_ref, ovT_ref, o_ref, bias_ref, st_ref,
                *, n_cmp, n_slc, n_sel):
    f32 = jnp.float32
    bf16 = jnp.bfloat16
    c0 = pl.program_id(2) * NSA_CHUNKS
    c_last = c0 + NSA_CHUNKS - 1
    q = jnp.concatenate([qT_ref[0, 0, j] for j in range(NSA_CHUNKS)], axis=1)
    ncols = q.shape[1]
    chunk_cols = ncols // NSA_CHUNKS
    ncp = kc_ref.shape[2]
    lane = lax.broadcasted_iota(jnp.int32, (1, ncols), 1)
    ccol = c0 + lane // chunk_cols
    tpos = ccol * Q_CHUNK + (lane & (Q_CHUNK - 1))

    s = jnp.dot(kc_ref[0, 0], q, preferred_element_type=f32)
    n_idx = lax.broadcasted_iota(jnp.int32, (ncp, 1), 0)
    vis = (n_idx * CMP_STRIDE + (CMP_BLOCK - 1) <= tpos) & (n_idx < n_cmp)
    s = jnp.where(vis, s, MASK_FILL)
    m = jnp.max(s, axis=0, keepdims=True)
    p = jnp.where(vis, jnp.exp(s - m), 0.0)
    l = jnp.sum(p, axis=0, keepdims=True)
    p = p / jnp.where(l > 0.0, l, 1.0)
    p_hi = p.astype(bf16)
    o_cmp = jnp.dot(vcT_ref[0, 0], p_hi, preferred_element_type=f32)

    p_lo = (p - p_hi.astype(f32)).astype(bf16)
    ov = ovT_ref[...]
    imp4 = (jnp.dot(ov, p_hi, preferred_element_type=f32)
            + jnp.dot(ov, p_lo, preferred_element_type=f32))
    lane128 = lax.broadcasted_iota(jnp.int32, (1, LANES), 1)
    first_half = lane128 < Q_CHUNK

    def head_sum(j):
        z = imp4[:, j * chunk_cols:j * chunk_cols + LANES] + imp4[:, j * chunk_cols + LANES:(j + 1) * chunk_cols]
        return z + pltpu.roll(z, Q_CHUNK, axis=1)

    imp = jnp.where(first_half, head_sum(0), head_sum(1))
    c = c0 + jnp.where(first_half, 0, 1)

    base = pl.multiple_of((c0 // 2) * LANES, LANES)
    st = jnp.dot(kw_ref[0, 0, pl.ds(base, WIN_SPAN), :], q, preferred_element_type=f32)
    kpos = base - WINDOW + lax.broadcasted_iota(jnp.int32, (WIN_SPAN, 1), 0)
    d = tpos - kpos
    st = jnp.where((d >= 0) & (d < WINDOW) & (kpos >= 0), st, MASK_FILL)
    p_w = jnp.exp(st - jnp.max(st, axis=0, keepdims=True))
    pv_w = jnp.dot(vwT_ref[0, 0, :, pl.ds(base, WIN_SPAN)], p_w.astype(bf16), preferred_element_type=f32)
    o_win = pv_w[:HEAD_DIM] / pv_w[HEAD_DIM:HEAD_DIM + 1]

    dbase = pl.multiple_of((c0 // 2) * LANES, LANES)
    st = jnp.dot(ks_ref[0, 0, pl.ds(dbase, LANES), :HEAD_DIM], q, preferred_element_type=f32)
    kpos = dbase + lax.broadcasted_iota(jnp.int32, (LANES, 1), 0)
    st = jnp.where((kpos >= ccol * SLC_BLOCK) & (kpos <= tpos), st, MASK_FILL)
    m_d = jnp.max(st, axis=0, keepdims=True)
    pv_d = jnp.dot(vsT_ref[0, 0, :, pl.ds(dbase, LANES)], jnp.exp(st - m_d).astype(bf16),
                   preferred_element_type=f32)

    jidx = lax.broadcasted_iota(jnp.int32, (n_slc, LANES), 0)
    forced = (jidx == 0) | (jidx == c) | (jidx == c - 1)
    work = jnp.where(forced | (jidx > c), -2.0, imp)
    sel = forced
    for _ in range(n_sel - 3):
        mx = jnp.max(work, axis=0, keepdims=True)
        first = jnp.min(jnp.where(work == mx, jidx, n_slc), axis=0, keepdims=True)
        hit = jidx == first
        sel = sel | hit
        work = jnp.where(hit, -2.0, work)
    early = (jnp.zeros_like(jidx) + c) < n_sel
    sel = (sel | early) & (jidx < c)
    bias = jnp.where(sel, 0.0, MASK_FILL)
    swapped = pltpu.roll(bias, Q_CHUNK, axis=1)
    b0 = jnp.where(first_half, bias, swapped)
    b1 = jnp.where(first_half, swapped, bias)
    bias_ref[...] = jnp.concatenate([b0, b0, b1, b1], axis=1).astype(bf16)

    zeros_tail = jnp.zeros((LANES - HEAD_DIM - SLC_STEP_BLOCKS, ncols), bf16)

    def scores(step):
        rows = bias_ref[pl.ds(pl.multiple_of(step * SLC_STEP_BLOCKS, SLC_STEP_BLOCKS), SLC_STEP_BLOCKS), :]
        rhs = jnp.concatenate([q, rows, zeros_tail], axis=0)
        off = pl.multiple_of(step * SLC_STEP, SLC_STEP)
        st = jnp.dot(ks_ref[0, 0, pl.ds(off, SLC_STEP), :], rhs, preferred_element_type=f32)
        st_ref[step & 1] = st
        return jnp.max(st, axis=0, keepdims=True)

    def accumulate(step, mx, m, acc):
        m_new = jnp.maximum(m, mx)
        off = pl.multiple_of(step * SLC_STEP, SLC_STEP)
        pv = jnp.dot(vsT_ref[0, 0, :, pl.ds(off, SLC_STEP)], jnp.exp(st_ref[step & 1] - m_new).astype(bf16),
                     preferred_element_type=f32)
        return m_new, jnp.exp(m - m_new) * acc + pv

    n_steps = jnp.maximum((c_last + SLC_STEP_BLOCKS - 1) // SLC_STEP_BLOCKS, 1)

    def body(i, carry):
        mx, m, acc = carry
        m, acc = accumulate(i, mx, m, acc)
        return scores(i + 1), m, acc

    carry = (scores(0), jnp.full((1, ncols), MAX_FLOOR, f32), jnp.zeros((V_ROWS, ncols), f32))
    mx, m_s, acc = lax.fori_loop(0, n_steps - 1, body, carry)
    m_s, acc = accumulate(n_steps - 1, mx, m_s, acc)
    m_t = jnp.maximum(m_s, m_d)
    tot = jnp.exp(m_s - m_t) * acc + jnp.exp(m_d - m_t) * pv_d
    o_slc = tot[:HEAD_DIM] / tot[HEAD_DIM:HEAD_DIM + 1]

    g = jnp.concatenate([g_ref[0, 0, j] for j in range(NSA_CHUNKS)], axis=1)
    o = g[0:1] * o_cmp + g[1:2] * o_slc + g[2:3] * o_win
    for j in range(NSA_CHUNKS):
        o_ref[0, 0, j] = o[:, j * chunk_cols:(j + 1) * chunk_cols].astype(o_ref.dtype)


def nsa_attention_pallas(q2, kvs, gates2, cmp_pos, cmp_w1, cmp_w2, B, L, interpret=False):
    f32, bf16 = jnp.float32, jnp.bfloat16
    G, H, hd = N_KV, HPG, HEAD_DIM
    C = L // Q_CHUNK
    n16 = L // CMP_STRIDE
    n_cmp = (L - CMP_BLOCK) // CMP_STRIDE + 1
    ncp = -(-n_cmp // LANES) * LANES
    n_slc = L // SLC_BLOCK
    n_sel = min(SLC_TOPK, n_slc)
    k_cmp, v_cmp, k_slc, v_slc, k_win, v_win = kvs

    def compress(kv2, pos, w1, w2):
        x = kv2.reshape(B, n16, CMP_STRIDE, G, hd).transpose(0, 3, 1, 2, 4).reshape(B * G * n16, CMP_STRIDE * hd)
        half = CMP_STRIDE * hd
        wcat = jnp.concatenate([w1[:half], w1[half:]], axis=1)
        ab = pmm(x, wcat, interpret=interpret).reshape(B, G, n16, 2 * CMP_HIDDEN)
        posb = pmm(jnp.pad(pos.reshape(1, -1), ((0, 7), (0, 0))), w1, interpret=interpret)[0]
        hid = ab[:, :, :-1, :CMP_HIDDEN] + ab[:, :, 1:, CMP_HIDDEN:] + posb
        hid = jnp.pad(jax.nn.gelu(hid), ((0, 0), (0, 0), (0, ncp - n_cmp), (0, 0)))
        out = pmm(hid.reshape(B * G * ncp, CMP_HIDDEN), w2, interpret=interpret)
        return out.reshape(B, G, ncp, hd)

    kc = compress(k_cmp, cmp_pos[0], cmp_w1[0], cmp_w2[0]).astype(bf16)
    vcT = compress(v_cmp, cmp_pos[1], cmp_w1[1], cmp_w2[1]).astype(bf16).transpose(0, 1, 3, 2)

    def key_major(a):
        return a.astype(bf16).reshape(B, L, G, hd).transpose(0, 2, 1, 3)

    def dim_major(a):
        return a.astype(bf16).reshape(B, L, G, hd).transpose(0, 2, 3, 1)

    def with_ones_row(vT):
        n = vT.shape[-1]
        extra = jnp.zeros((B, G, V_ROWS - hd, n), bf16).at[:, :, 0, :].set(1.0)
        return jnp.concatenate([vT, extra], axis=2)

    blk_onehot = jax.nn.one_hot((jnp.arange(L) // SLC_BLOCK) % SLC_STEP_BLOCKS, LANES - hd, dtype=bf16)
    ks = jnp.concatenate([key_major(k_slc), jnp.broadcast_to(blk_onehot, (B, G, L, LANES - hd))], axis=-1)
    vsT = with_ones_row(dim_major(v_slc))
    kw = jnp.pad(key_major(k_win), ((0, 0), (0, 0), (WINDOW, 0), (0, 0)))
    vwT = with_ones_row(jnp.pad(dim_major(v_win), ((0, 0), (0, 0), (0, 0), (WINDOW, 0))))
    qT = q2.astype(bf16).reshape(B, C, Q_CHUNK, G, H, hd)
    qT = qT.transpose(0, 3, 1, 5, 4, 2).reshape(B, G, C, hd, H * Q_CHUNK)
    gT = gates2.astype(f32).reshape(B, C, Q_CHUNK, 3, G, H).transpose(0, 4, 1, 3, 5, 2).reshape(B, G, C, 3, H * Q_CHUNK)
    c_start = jnp.arange(ncp)[None, :] * CMP_STRIDE
    s_start = jnp.arange(n_slc)[:, None] * SLC_BLOCK
    ovT = ((c_start < s_start + SLC_BLOCK) & (c_start + CMP_BLOCK > s_start)
           & (jnp.arange(ncp)[None, :] < n_cmp)).astype(bf16)

    ncols = H * Q_CHUNK
    bg = lambda b, g, c: (b, g, 0, 0)
    bgc = lambda b, g, c: (b, g, c, 0, 0)
    oT = pl.pallas_call(
        functools.partial(_nsa_kernel, n_cmp=n_cmp, n_slc=n_slc, n_sel=n_sel),
        grid=(B, G, C // NSA_CHUNKS),
        in_specs=[pl.BlockSpec((1, 1, NSA_CHUNKS, hd, ncols), bgc),
                  pl.BlockSpec((1, 1, ncp, hd), bg),
                  pl.BlockSpec((1, 1, hd, ncp), bg),
                  pl.BlockSpec((1, 1, L, LANES), bg),
                  pl.BlockSpec((1, 1, V_ROWS, L), bg),
                  pl.BlockSpec((1, 1, L + WINDOW, hd), bg),
                  pl.BlockSpec((1, 1, V_ROWS, L + WINDOW), bg),
                  pl.BlockSpec((1, 1, NSA_CHUNKS, 3, ncols), bgc),
                  pl.BlockSpec((n_slc, ncp), lambda b, g, c: (0, 0))],
        out_specs=pl.BlockSpec((1, 1, NSA_CHUNKS, hd, ncols), bgc),
        out_shape=jax.ShapeDtypeStruct((B, G, C, hd, ncols), bf16),
        scratch_shapes=[pltpu.VMEM((n_slc, NSA_CHUNKS * ncols), bf16),
                        pltpu.VMEM((2, SLC_STEP, NSA_CHUNKS * ncols), f32)],
        compiler_params=pltpu.CompilerParams(
            dimension_semantics=("parallel", "parallel", "arbitrary"),
            vmem_limit_bytes=VMEM_LIMIT_BYTES),
        interpret=interpret,
        name="nsa_attention",
    )(qT, kc, vcT, ks, vsT, kw, vwT, gT, ovT)
    o = oT.reshape(B, G, C, hd, H, Q_CHUNK).transpose(0, 2, 5, 1, 4, 3)
    return o.reshape(B * L, G * H * hd)


S5_CHUNK = 32


def _s5_state_kernel(x_ref, ws_ref, s_ref):
    s_ref[0] = jnp.dot(x_ref[0].astype(jnp.bfloat16), ws_ref[0], preferred_element_type=jnp.float32)


def _s5_scan_kernel(sre_ref, sim_ref, are_ref, aim_ref, hre_ref, him_ref, *, chunks_per_batch):
    a_re = are_ref[...]
    a_im = aim_ref[...]
    n_batch = sre_ref.shape[0] // chunks_per_batch

    def step(c, carry):
        new = []
        for b in range(n_batch):
            h_re, h_im = carry[b]
            row = b * chunks_per_batch + c
            hre_ref[row] = h_re
            him_ref[row] = h_im
            new.append((a_re * h_re - a_im * h_im + sre_ref[row],
                        a_re * h_im + a_im * h_re + sim_ref[row]))
        return tuple(new)

    zero = jnp.zeros(a_re.shape, jnp.float32)
    lax.fori_loop(0, chunks_per_batch, step, tuple((zero, zero) for _ in range(n_batch)))


def _s5_out_kernel(x_ref, hp_ref, m_ref, wo_ref, d_ref, y_ref):
    x = x_ref[0]
    y = jnp.dot(x.astype(jnp.bfloat16), m_ref[0], preferred_element_type=jnp.float32)
    y = y + jnp.dot(hp_ref[0].astype(jnp.bfloat16), wo_ref[0], preferred_element_type=jnp.float32)
    y_ref[0] = y + x * d_ref[0]


def s5_scan_pallas(u2, lam_re, lam_im, log_dt, b_re, b_im, c_re, c_im, d_skip, B, L, interpret=False):
    f32, bf16 = jnp.float32, jnp.bfloat16
    G, P, H, T = SSM_GROUPS, SSM_STATE, SSM_GROUP, S5_CHUNK
    hi = lax.Precision.HIGHEST
    NC = B * L // T
    dt = jnp.exp(log_dt.astype(f32))[:, None]
    lr, li = lam_re.astype(f32), lam_im.astype(f32)
    mag = jnp.exp(lr * dt)
    ab_re, ab_im = mag * jnp.cos(li * dt), mag * jnp.sin(li * dt)
    den = lr * lr + li * li
    nr = ab_re - 1.0
    cr = (nr * lr + ab_im * li) / den
    cim = (ab_im * lr - nr * li) / den
    br, bim = b_re.astype(f32), b_im.astype(f32)
    bb_re = cr[..., None] * br - cim[..., None] * bim
    bb_im = cr[..., None] * bim + cim[..., None] * br
    k = jnp.arange(T + 1, dtype=f32)[:, None, None]
    pw_mag = jnp.exp(k * (lr * dt))
    pw_re = pw_mag * jnp.cos(k * (li * dt))
    pw_im = pw_mag * jnp.sin(k * (li * dt))
    cre, cimg = c_re.astype(f32), c_im.astype(f32)
    ab_r = pw_re[:T, :, :, None] * bb_re - pw_im[:T, :, :, None] * bb_im
    ab_i = pw_re[:T, :, :, None] * bb_im + pw_im[:T, :, :, None] * bb_re
    kern = (jnp.einsum('gop,tgpi->tgoi', cre, ab_r, precision=hi)
            - jnp.einsum('gop,tgpi->tgoi', cimg, ab_i, precision=hi))
    tt = jnp.arange(T)
    lag = tt[None, :] - tt[:, None]
    m = jnp.where((lag >= 0)[:, :, None, None, None], kern[jnp.clip(lag, 0, T - 1)], 0.0)
    m = m.transpose(2, 0, 4, 1, 3).reshape(G, T * H, T * H).astype(bf16)
    ws_re = ab_r[::-1].transpose(1, 0, 3, 2).reshape(G, T * H, P)
    ws_im = ab_i[::-1].transpose(1, 0, 3, 2).reshape(G, T * H, P)
    ws = jnp.concatenate([ws_re, ws_im], axis=-1).astype(bf16)
    ca_re = cre[None] * pw_re[1:, :, None, :] - cimg[None] * pw_im[1:, :, None, :]
    ca_im = cre[None] * pw_im[1:, :, None, :] + cimg[None] * pw_re[1:, :, None, :]
    wo = jnp.concatenate([ca_re.transpose(1, 3, 0, 2).reshape(G, P, T * H),
                          -ca_im.transpose(1, 3, 0, 2).reshape(G, P, T * H)], axis=1).astype(bf16)
    dtile = jnp.tile(d_skip.astype(f32), (1, T)).reshape(G, 1, T * H)

    xg = u2.astype(f32).reshape(NC, T, G, H).transpose(2, 0, 1, 3).reshape(G, NC, T * H)
    cparams = pltpu.CompilerParams(dimension_semantics=("parallel",), vmem_limit_bytes=VMEM_LIMIT_BYTES)
    s = pl.pallas_call(
        _s5_state_kernel, grid=(G,),
        in_specs=[pl.BlockSpec((1, NC, T * H), lambda g: (g, 0, 0)),
                  pl.BlockSpec((1, T * H, 2 * P), lambda g: (g, 0, 0))],
        out_specs=pl.BlockSpec((1, NC, 2 * P), lambda g: (g, 0, 0)),
        out_shape=jax.ShapeDtypeStruct((G, NC, 2 * P), f32),
        compiler_params=cparams, interpret=interpret, name="s5_chunk_state",
    )(xg, ws)
    sp = s.reshape(G, NC, 2, P).transpose(1, 2, 0, 3).reshape(NC, 2, G // 2, 2 * P)
    a_re = pw_re[T].reshape(G // 2, 2 * P)
    a_im = pw_im[T].reshape(G // 2, 2 * P)
    full = lambda shape: pl.BlockSpec(shape, lambda i: (0,) * len(shape))
    st_shape = (NC, G // 2, 2 * P)
    h_re, h_im = pl.pallas_call(
        functools.partial(_s5_scan_kernel, chunks_per_batch=L // T), grid=(1,),
        in_specs=[full(st_shape), full(st_shape), full((G // 2, 2 * P)), full((G // 2, 2 * P))],
        out_specs=[full(st_shape), full(st_shape)],
        out_shape=[jax.ShapeDtypeStruct(st_shape, f32)] * 2,
        compiler_params=pltpu.CompilerParams(dimension_semantics=("arbitrary",), vmem_limit_bytes=VMEM_LIMIT_BYTES),
        interpret=interpret, name="s5_state_scan",
    )(sp[:, 0], sp[:, 1], a_re, a_im)
    hp = jnp.stack([h_re, h_im], axis=1).reshape(NC, 2, G, P).transpose(2, 0, 1, 3).reshape(G, NC, 2 * P)
    y = pl.pallas_call(
        _s5_out_kernel, grid=(G,),
        in_specs=[pl.BlockSpec((1, NC, T * H), lambda g: (g, 0, 0)),
                  pl.BlockSpec((1, NC, 2 * P), lambda g: (g, 0, 0)),
                  pl.BlockSpec((1, T * H, T * H), lambda g: (g, 0, 0)),
                  pl.BlockSpec((1, 2 * P, T * H), lambda g: (g, 0, 0)),
                  pl.BlockSpec((1, 1, T * H), lambda g: (g, 0, 0))],
        out_specs=pl.BlockSpec((1, NC, T * H), lambda g: (g, 0, 0)),
        out_shape=jax.ShapeDtypeStruct((G, NC, T * H), f32),
        compiler_params=cparams, interpret=interpret, name="s5_chunk_out",
    )(xg, hp, m, wo, dtile)
    return y.reshape(G, NC, T, H).transpose(1, 2, 0, 3).reshape(B * L, G * H)


MOE_TILE = 512
MOE_SEG = 32
MOE_SLOTS = TOPK_IN_GROUP * MOE_TILE + N_EXPERTS * MOE_SEG
ROUTE_E1, ROUTE_E2, ROUTE_W1, ROUTE_W2 = 0, 1, 2, 3


def _split_bf16(a):
    hi = a.astype(jnp.bfloat16)
    return hi, (a - hi.astype(jnp.float32)).astype(jnp.bfloat16)


def _moe_route_kernel(x_ref, g_ref, sc_ref, sh_ref, wr_ref, br_ref, h_ref, route_ref, routeT_ref, cnt_ref):
    f32 = jnp.float32
    x = x_ref[...]
    y = x * lax.rsqrt(jnp.mean(x * x, axis=-1, keepdims=True) + EPS)
    h = y * g_ref[...] * (1.0 + sc_ref[0]) + sh_ref[0]
    h_ref[...] = h.astype(h_ref.dtype)
    h_hi, h_lo = _split_bf16(h)
    w_hi, w_lo = _split_bf16(wr_ref[...])
    lg = (jnp.dot(h_hi, w_hi, preferred_element_type=f32) + jnp.dot(h_hi, w_lo, preferred_element_type=f32)
          + jnp.dot(h_lo, w_hi, preferred_element_type=f32) + br_ref[...])
    lane = lax.broadcasted_iota(jnp.int32, lg.shape, 1)
    ninf = -jnp.inf
    gmask = lane < N_GROUPS
    gmax = jnp.max(jnp.where(gmask, lg, ninf), axis=1, keepdims=True)
    gsum = jnp.sum(jnp.where(gmask, jnp.exp(lg - gmax), 0.0), axis=1, keepdims=True)
    g_idx = jnp.min(jnp.where(gmask & (lg == gmax), lane, LANES), axis=1, keepdims=True)
    g_p = 1.0 / gsum
    lo = N_GROUPS + EXP_PER_GROUP * g_idx
    emask = (lane >= lo) & (lane < lo + EXP_PER_GROUP)
    el = jnp.where(emask, lg, ninf)
    emax = jnp.max(el, axis=1, keepdims=True)
    i1 = jnp.min(jnp.where(el == emax, lane, LANES), axis=1, keepdims=True)
    el2 = jnp.where(lane == i1, ninf, el)
    emax2 = jnp.max(el2, axis=1, keepdims=True)
    i2 = jnp.min(jnp.where(el2 == emax2, lane, LANES), axis=1, keepdims=True)
    r = jnp.exp(emax2 - emax)
    w1 = g_p / (1.0 + r)
    w2 = g_p * r / (1.0 + r)
    e1 = (i1 - N_GROUPS).astype(f32)
    e2 = (i2 - N_GROUPS).astype(f32)
    route = jnp.where(lane == ROUTE_E1, e1, jnp.where(lane == ROUTE_E2, e2,
                      jnp.where(lane == ROUTE_W1, w1, jnp.where(lane == ROUTE_W2, w2, 0.0))))
    route_ref[...] = route
    routeT_ref[...] = route.T[:8, :]
    cnt = jnp.sum(jnp.where((lane == i1) | (lane == i2), 1.0, 0.0), axis=0, keepdims=True)
    cnt_ref[0] = jnp.broadcast_to(cnt, (8, LANES))


def _moe_ffn_kernel(seg_start_ref, seg_nblk_ref, h_ref, route_ref, routeT_ref, lrow_ref, lcol_ref, wgu_ref, wd_ref,
                    x_ref, gate_ref, fg_ref, o_ref, xs_ref, yw_ref, ws_ref, *, final_norm):
    f32, bf16 = jnp.float32, jnp.bfloat16
    t = pl.program_id(0)
    e = pl.program_id(1)
    tm = h_ref.shape[0]

    @pl.when(e == 0)
    def _dispatch():
        rT = routeT_ref[...]
        e1r, e2r = rT[ROUTE_E1:ROUTE_E1 + 1], rT[ROUTE_E2:ROUTE_E2 + 1]
        w1r, w2r = rT[ROUTE_W1:ROUTE_W1 + 1], rT[ROUTE_W2:ROUTE_W2 + 1]
        eio = lax.broadcasted_iota(jnp.int32, (N_EXPERTS, tm), 0).astype(f32)
        oh1 = eio == e1r
        oh2 = eio == e2r
        oh = jnp.where(oh1 | oh2, 1.0, 0.0).astype(bf16)
        before = (lax.broadcasted_iota(jnp.int32, (tm, tm), 0)
                  < lax.broadcasted_iota(jnp.int32, (tm, tm), 1))
        rank = jnp.dot(oh, jnp.where(before, 1.0, 0.0).astype(bf16), preferred_element_type=f32)
        pos = lcol_ref[0][:, :1] + rank
        d1 = jnp.sum(jnp.where(oh1, pos, 0.0), axis=0, keepdims=True)
        d2 = jnp.sum(jnp.where(oh2, pos, 0.0), axis=0, keepdims=True)
        slot = lax.broadcasted_iota(jnp.int32, (MOE_SLOTS, tm), 0).astype(f32)
        m1 = slot == d1
        m2 = slot == d2
        perm = jnp.where(m1 | m2, 1.0, 0.0).astype(bf16)
        xs_ref[...] = jnp.dot(perm, h_ref[...], preferred_element_type=f32).astype(bf16)
        ws_ref[...] = jnp.sum(jnp.where(m1, w1r, 0.0) + jnp.where(m2, w2r, 0.0), axis=1, keepdims=True)
        yw_ref[...] = jnp.zeros(yw_ref.shape, yw_ref.dtype)

    start = seg_start_ref[t * N_EXPERTS + e]
    nblk = seg_nblk_ref[t * N_EXPERTS + e]

    def block(i, _):
        r0 = pl.multiple_of(start + i * MOE_SEG, MOE_SEG)
        xb = xs_ref[pl.ds(r0, MOE_SEG), :]
        gu = jnp.dot(xb, wgu_ref[0], preferred_element_type=f32)
        a = jax.nn.silu(gu[:, :D_EXPERT]) * gu[:, D_EXPERT:]
        y = jnp.dot(a.astype(bf16), wd_ref[0], preferred_element_type=f32)
        yw_ref[pl.ds(r0, MOE_SEG), :] = (y * ws_ref[pl.ds(r0, MOE_SEG), :]).astype(bf16)
        return 0

    lax.fori_loop(0, nblk, block, 0)

    @pl.when(e == N_EXPERTS - 1)
    def _combine():
        r = route_ref[...]
        e1c, e2c = r[:, ROUTE_E1:ROUTE_E1 + 1], r[:, ROUTE_E2:ROUTE_E2 + 1]
        lane = lax.broadcasted_iota(jnp.int32, (tm, LANES), 1).astype(f32)
        oh1 = lane == e1c
        oh2 = lane == e2c
        oh = jnp.where(oh1 | oh2, 1.0, 0.0).astype(bf16)
        after = (lax.broadcasted_iota(jnp.int32, (tm, tm), 1)
                 < lax.broadcasted_iota(jnp.int32, (tm, tm), 0))
        rank = jnp.dot(jnp.where(after, 1.0, 0.0).astype(bf16), oh, preferred_element_type=f32)
        pos = lrow_ref[0][:1, :] + rank
        d1 = jnp.sum(jnp.where(oh1, pos, 0.0), axis=1, keepdims=True)
        d2 = jnp.sum(jnp.where(oh2, pos, 0.0), axis=1, keepdims=True)
        slot = lax.broadcasted_iota(jnp.int32, (tm, MOE_SLOTS), 1).astype(f32)
        comb = jnp.where((slot == d1) | (slot == d2), 1.0, 0.0).astype(bf16)
        moe = jnp.dot(comb, yw_ref[...], preferred_element_type=f32)
        xn = x_ref[...] + gate_ref[0] * moe
        if final_norm:
            xn = xn * lax.rsqrt(jnp.mean(xn * xn, axis=-1, keepdims=True) + EPS) * fg_ref[...]
        o_ref[...] = xn


def moe_layer_pallas(x2, norm_g, sc2, sh2, g2, w_group, b_group, w_expert, b_expert, w_gate, w_up, w_down,
                     final_g, final_norm, B, L, interpret=False):
    f32, bf16 = jnp.float32, jnp.bfloat16
    N, D = x2.shape
    tm = MOE_TILE
    n_tiles = N // tm
    tpb = L // tm
    wr = jnp.zeros((D, LANES), f32).at[:, :N_GROUPS].set(w_group).at[:, N_GROUPS:N_GROUPS + N_EXPERTS].set(w_expert)
    br = jnp.zeros((1, LANES), f32).at[0, :N_GROUPS].set(b_group).at[0, N_GROUPS:N_GROUPS + N_EXPERTS].set(b_expert)
    row = lambda v: v.reshape(1, D).astype(f32)
    per_batch = lambda v: v.reshape(B, 1, D).astype(f32)
    cp = pltpu.CompilerParams(dimension_semantics=("parallel",), vmem_limit_bytes=VMEM_LIMIT_BYTES)
    h, route, routeT, cnt = pl.pallas_call(
        _moe_route_kernel, grid=(n_tiles,),
        in_specs=[pl.BlockSpec((tm, D), lambda i: (i, 0)),
                  pl.BlockSpec((1, D), lambda i: (0, 0)),
                  pl.BlockSpec((1, 1, D), lambda i: (i // tpb, 0, 0)),
                  pl.BlockSpec((1, 1, D), lambda i: (i // tpb, 0, 0)),
                  pl.BlockSpec((D, LANES), lambda i: (0, 0)),
                  pl.BlockSpec((1, LANES), lambda i: (0, 0))],
        out_specs=[pl.BlockSpec((tm, D), lambda i: (i, 0)),
                   pl.BlockSpec((tm, LANES), lambda i: (i, 0)),
                   pl.BlockSpec((8, tm), lambda i: (0, i)),
                   pl.BlockSpec((1, 8, LANES), lambda i: (i, 0, 0))],
        out_shape=[jax.ShapeDtypeStruct((N, D), bf16), jax.ShapeDtypeStruct((N, LANES), f32),
                   jax.ShapeDtypeStruct((8, N), f32), jax.ShapeDtypeStruct((n_tiles, 8, LANES), f32)],
        compiler_params=cp, interpret=interpret, name="moe_route",
    )(x2, row(norm_g), per_batch(sc2), per_batch(sh2), wr, br)
    counts = cnt[:, 0, N_GROUPS:N_GROUPS + N_EXPERTS].astype(jnp.int32)
    nblk = (counts + MOE_SEG - 1) // MOE_SEG
    starts = (jnp.cumsum(nblk, axis=1) - nblk) * MOE_SEG
    starts_f = starts.astype(f32)
    lrow = jnp.zeros((n_tiles, 8, LANES), f32).at[:, :, :N_EXPERTS].set(starts_f[:, None, :])
    lcol = jnp.broadcast_to(starts_f[:, :, None], (n_tiles, N_EXPERTS, LANES))
    wgu = jnp.concatenate([w_gate, w_up], axis=-1).astype(bf16)
    wd = w_down.astype(bf16)
    tile = lambda t, e, *_: (t, 0)
    out = pl.pallas_call(
        functools.partial(_moe_ffn_kernel, final_norm=final_norm),
        grid_spec=pltpu.PrefetchScalarGridSpec(
            num_scalar_prefetch=2, grid=(n_tiles, N_EXPERTS),
            in_specs=[pl.BlockSpec((tm, D), tile),
                      pl.BlockSpec((tm, LANES), tile),
                      pl.BlockSpec((8, tm), lambda t, e, *_: (0, t)),
                      pl.BlockSpec((1, 8, LANES), lambda t, e, *_: (t, 0, 0)),
                      pl.BlockSpec((1, N_EXPERTS, LANES), lambda t, e, *_: (t, 0, 0)),
                      pl.BlockSpec((1, D, 2 * D_EXPERT), lambda t, e, *_: (e, 0, 0)),
                      pl.BlockSpec((1, D_EXPERT, D), lambda t, e, *_: (e, 0, 0)),
                      pl.BlockSpec((tm, D), tile),
                      pl.BlockSpec((1, 1, D), lambda t, e, *_: (t // tpb, 0, 0)),
                      pl.BlockSpec((1, D), lambda t, e, *_: (0, 0))],
            out_specs=pl.BlockSpec((tm, D), tile),
            scratch_shapes=[pltpu.VMEM((MOE_SLOTS, D), bf16), pltpu.VMEM((MOE_SLOTS, D), bf16),
                            pltpu.VMEM((MOE_SLOTS, 1), f32)]),
        out_shape=jax.ShapeDtypeStruct((N, D), f32),
        compiler_params=pltpu.CompilerParams(dimension_semantics=("parallel", "arbitrary"),
                                             vmem_limit_bytes=VMEM_LIMIT_BYTES),
        interpret=interpret, name="moe_ffn",
    )(starts.reshape(-1), nblk.reshape(-1), h, route, routeT, lrow, lcol, wgu, wd, x2, per_batch(g2), row(final_g))
    return out


ROW_TILE = 256
GATE_LANES = 128


def _in_proj_kernel(x_ref, g_ref, sc_ref, sh_ref, wq_ref, wkv_ref, wg_ref, wu_ref, wm_ref,
                    q_ref, kv_ref, ng_ref, u_ref, mg_ref):
    f32 = jnp.float32
    x = x_ref[...]
    y = x * lax.rsqrt(jnp.mean(x * x, axis=-1, keepdims=True) + EPS)
    h = (y * g_ref[...] * (1.0 + sc_ref[0]) + sh_ref[0]).astype(jnp.bfloat16)
    q = jnp.dot(h, wq_ref[...], preferred_element_type=f32)
    q_ref[...] = (q * HEAD_DIM ** -0.5).astype(q_ref.dtype)
    kv_ref[...] = jnp.dot(h, wkv_ref[...], preferred_element_type=f32).astype(kv_ref.dtype)
    ng_ref[...] = jax.nn.sigmoid(jnp.dot(h, wg_ref[...], preferred_element_type=f32))
    u_ref[...] = jnp.dot(h, wu_ref[...], preferred_element_type=f32)
    mg_ref[...] = jax.nn.sigmoid(jnp.dot(h, wm_ref[...], preferred_element_type=f32)).astype(mg_ref.dtype)


def in_proj_pallas(x2, norm_g, sc1, sh1, w_in, B, L, interpret=False):
    f32, bf16 = jnp.float32, jnp.bfloat16
    N, D = x2.shape
    tm = ROW_TILE
    tpb = L // tm
    o1 = ATTN_WIDTH
    o2 = o1 + 6 * KV_WIDTH
    o3 = o2 + 3 * N_HEADS
    o4 = o3 + SSM_WIDTH
    wb = w_in.astype(bf16)
    wq, wkv, wu, wm = wb[:, :o1], wb[:, o1:o2], wb[:, o3:o4], wb[:, o4:]
    wg = jnp.pad(wb[:, o2:o3], ((0, 0), (0, GATE_LANES - 3 * N_HEADS)))
    const = lambda shape: pl.BlockSpec(shape, lambda i: (0,) * len(shape))
    rows = lambda w: pl.BlockSpec((tm, w), lambda i: (i, 0))
    per_batch = pl.BlockSpec((1, 1, D), lambda i: (i // tpb, 0, 0))
    return pl.pallas_call(
        _in_proj_kernel, grid=(N // tm,),
        in_specs=[rows(D), const((1, D)), per_batch, per_batch,
                  const(wq.shape), const(wkv.shape), const(wg.shape), const(wu.shape), const(wm.shape)],
        out_specs=[rows(o1), rows(o2 - o1), rows(GATE_LANES), rows(SSM_WIDTH), rows(2 * D)],
        out_shape=[jax.ShapeDtypeStruct((N, o1), bf16), jax.ShapeDtypeStruct((N, o2 - o1), bf16),
                   jax.ShapeDtypeStruct((N, GATE_LANES), f32), jax.ShapeDtypeStruct((N, SSM_WIDTH), f32),
                   jax.ShapeDtypeStruct((N, 2 * D), bf16)],
        compiler_params=pltpu.CompilerParams(dimension_semantics=("parallel",), vmem_limit_bytes=VMEM_LIMIT_BYTES),
        interpret=interpret, name="in_proj",
    )(x2, norm_g.reshape(1, D).astype(f32), sc1.reshape(B, 1, D).astype(f32), sh1.reshape(B, 1, D).astype(f32),
      wq, wkv, wg, wu, wm)


def _merge_kernel(attn_ref, y_ref, mg_ref, x_ref, g1_ref, wglu_ref, wso_ref, wao_ref, wout_ref, o_ref):
    f32, bf16 = jnp.float32, jnp.bfloat16
    D = x_ref.shape[1]
    y = jax.nn.gelu(y_ref[...])
    ssm = y * jax.nn.sigmoid(jnp.dot(y.astype(bf16), wglu_ref[...], preferred_element_type=f32))
    a = jnp.dot(attn_ref[...], wao_ref[...], preferred_element_type=f32)
    s = jnp.dot(ssm.astype(bf16), wso_ref[...], preferred_element_type=f32)
    mg = mg_ref[...].astype(f32)
    merged = mg[:, :D] * a + mg[:, D:] * s
    out = jnp.dot(merged.astype(bf16), wout_ref[...], preferred_element_type=f32)
    o_ref[...] = x_ref[...] + g1_ref[0] * out


def merge_pallas(attn, y, mg, x2, g1, w_glu, ssm_w_o, nsa_w_o, w_out, B, L, interpret=False):
    bf16 = jnp.bfloat16
    N, D = x2.shape
    tm = ROW_TILE
    tpb = L // tm
    const = lambda shape: pl.BlockSpec(shape, lambda i: (0,) * len(shape))
    rows = lambda w: pl.BlockSpec((tm, w), lambda i: (i, 0))
    return pl.pallas_call(
        _merge_kernel, grid=(N // tm,),
        in_specs=[rows(ATTN_WIDTH), rows(SSM_WIDTH), rows(2 * D), rows(D),
                  pl.BlockSpec((1, 1, D), lambda i: (i // tpb, 0, 0)),
                  const(w_glu.shape), const(ssm_w_o.shape), const(nsa_w_o.shape), const(w_out.shape)],
        out_specs=rows(D),
        out_shape=jax.ShapeDtypeStruct((N, D), jnp.float32),
        compiler_params=pltpu.CompilerParams(dimension_semantics=("parallel",), vmem_limit_bytes=VMEM_LIMIT_BYTES),
        interpret=interpret, name="mixer_merge",
    )(attn, y, mg, x2, g1.reshape(B, 1, D).astype(jnp.float32),
      w_glu.astype(bf16), ssm_w_o.astype(bf16), nsa_w_o.astype(bf16), w_out.astype(bf16))


def kernel(x, c, ada_w, ada_b, norm1_g, w_in, ssm_lam_re, ssm_lam_im, ssm_log_dt, ssm_b_re, ssm_b_im, ssm_c_re, ssm_c_im, ssm_d, ssm_w_glu, ssm_w_o, nsa_cmp_pos, nsa_cmp_w1, nsa_cmp_w2, nsa_w_o, w_out, norm2_g, moe_w_group, moe_b_group, moe_w_expert, moe_b_expert, moe_w_gate, moe_w_up, moe_w_down, final_g):
    B, L, D = x.shape
    N = B * L
    x2 = x.reshape(N, D)
    cs = jnp.pad(jax.nn.silu(c), ((0, 8 - B), (0, 0)))
    for l in range(DEPTH):
        mod = pmm(cs, ada_w[l])[:B] + ada_b[l]
        sh1, sc1, g1, sh2, sc2, g2 = jnp.split(mod, 6, axis=-1)
        q, kv, ng, u, mg = in_proj_pallas(x2, norm1_g[l], sc1, sh1, w_in[l], B, L)
        kvs = [kv[:, i * KV_WIDTH:(i + 1) * KV_WIDTH] for i in range(6)]
        attn = nsa_attention_pallas(q, kvs, ng[:, :3 * N_HEADS], nsa_cmp_pos[l], nsa_cmp_w1[l], nsa_cmp_w2[l], B, L)
        y = s5_scan_pallas(u, ssm_lam_re[l], ssm_lam_im[l], ssm_log_dt[l], ssm_b_re[l], ssm_b_im[l],
                           ssm_c_re[l], ssm_c_im[l], ssm_d[l], B, L)
        x2 = merge_pallas(attn, y, mg, x2, g1, ssm_w_glu[l], ssm_w_o[l], nsa_w_o[l], w_out[l], B, L)
        x2 = moe_layer_pallas(x2, norm2_g[l], sc2, sh2, g2, moe_w_group[l], moe_b_group[l],
                              moe_w_expert[l], moe_b_expert[l], moe_w_gate[l], moe_w_up[l], moe_w_down[l],
                              final_g, l == DEPTH - 1, B, L)
    return x2.reshape(B, L, D)
```

```python
import functools
import math

import jax
import jax.numpy as jnp
from jax import lax
from jax.experimental import pallas as pl
from jax.experimental.pallas import tpu as pltpu

D_MODEL = 1024
DEPTH = 2
SSM_GROUP = 16
SSM_STATE = 64
SSM_WIDTH = D_MODEL // 2
SSM_GROUPS = SSM_WIDTH // SSM_GROUP
HEAD_DIM = 64
N_HEADS = D_MODEL // HEAD_DIM
N_KV = N_HEADS // 4
HPG = N_HEADS // N_KV
ATTN_WIDTH = N_HEADS * HEAD_DIM
KV_WIDTH = N_KV * HEAD_DIM
CMP_BLOCK = 32
CMP_STRIDE = 16
CMP_HIDDEN = 4 * HEAD_DIM
SLC_BLOCK = 64
SLC_TOPK = 16
WINDOW = 512
Q_CHUNK = 64
FORCED_SCORE = 1e4
N_GROUPS = 4
EXP_PER_GROUP = 8
N_EXPERTS = N_GROUPS * EXP_PER_GROUP
TOPK_IN_GROUP = 2
D_EXPERT = D_MODEL // 4
MOE_BLOCK = 128
EPS = 1e-6
NEG = -1e30

VMEM_LIMIT_BYTES = 48 * 1024 * 1024


def _mm_kernel(a_ref, b_ref, o_ref):
    a = a_ref[...].astype(jnp.bfloat16)
    b = b_ref[...].astype(jnp.bfloat16)
    o_ref[...] = jnp.dot(a, b, preferred_element_type=jnp.float32).astype(o_ref.dtype)


def pmm(a, b, tm=512, tn=512, out_dtype=jnp.float32, interpret=False):
    M, K = a.shape
    _, N = b.shape
    tm = min(tm, M)
    tn = min(tn, N)
    assert M % tm == 0 and N % tn == 0, (M, N, tm, tn)
    return pl.pallas_call(
        _mm_kernel,
        interpret=interpret,
        grid=(M // tm, N // tn),
        in_specs=[pl.BlockSpec((tm, K), lambda i, j: (i, 0)),
                  pl.BlockSpec((K, tn), lambda i, j: (0, j))],
        out_specs=pl.BlockSpec((tm, tn), lambda i, j: (i, j)),
        out_shape=jax.ShapeDtypeStruct((M, N), out_dtype),
        compiler_params=pltpu.CompilerParams(
            dimension_semantics=("parallel", "parallel"),
            vmem_limit_bytes=VMEM_LIMIT_BYTES),
    )(a, b)


MASK_FILL = -1e30
MAX_FLOOR = -1e29
SLC_STEP = 1024
SLC_STEP_BLOCKS = SLC_STEP // SLC_BLOCK
LANES = 128
V_ROWS = HEAD_DIM + 16
NSA_CHUNKS = 2
WIN_SPAN = WINDOW + NSA_CHUNKS * Q_CHUNK
DIAG_SPAN = NSA_CHUNKS * SLC_BLOCK


def _nsa_kernel(qT_ref, kc_ref, vcT_ref, ks_ref, vsT_ref, kw_ref, vwT_ref, g_ref, ovT_ref, o_ref, bias_ref, st_ref,
                *, n_cmp, n_slc, n_sel):
    f32 = jnp.float32
    bf16 = jnp.bfloat16
    c0 = pl.program_id(2) * NSA_CHUNKS
    c_last = c0 + NSA_CHUNKS - 1
    q = jnp.concatenate([qT_ref[0, 0, j] for j in range(NSA_CHUNKS)], axis=1)
    ncols = q.shape[1]
    chunk_cols = ncols // NSA_CHUNKS
    ncp = kc_ref.shape[2]
    lane = lax.broadcasted_iota(jnp.int32, (1, ncols), 1)
    ccol = c0 + lane // chunk_cols
    tpos = ccol * Q_CHUNK + (lane & (Q_CHUNK - 1))

    s = jnp.dot(kc_ref[0, 0], q, preferred_element_type=f32)
    n_idx = lax.broadcasted_iota(jnp.int32, (ncp, 1), 0)
    vis = (n_idx * CMP_STRIDE + (CMP_BLOCK - 1) <= tpos) & (n_idx < n_cmp)
    s = jnp.where(vis, s, MASK_FILL)
    m = jnp.max(s, axis=0, keepdims=True)
    p = jnp.where(vis, jnp.exp(s - m), 0.0)
    l = jnp.sum(p, axis=0, keepdims=True)
    p = p / jnp.where(l > 0.0, l, 1.0)
    p_hi = p.astype(bf16)
    o_cmp = jnp.dot(vcT_ref[0, 0], p_hi, preferred_element_type=f32)

    p_lo = (p - p_hi.astype(f32)).astype(bf16)
    ov = ovT_ref[...]
    imp4 = (jnp.dot(ov, p_hi, preferred_element_type=f32)
            + jnp.dot(ov, p_lo, preferred_element_type=f32))
    lane128 = lax.broadcasted_iota(jnp.int32, (1, LANES), 1)
    first_half = lane128 < Q_CHUNK

    def head_sum(j):
        z = imp4[:, j * chunk_cols:j * chunk_cols + LANES] + imp4[:, j * chunk_cols + LANES:(j + 1) * chunk_cols]
        return z + pltpu.roll(z, Q_CHUNK, axis=1)

    imp = jnp.concatenate([jnp.where(first_half, head_sum(2 * p), head_sum(2 * p + 1))
                           for p in range(NSA_CHUNKS // 2)], axis=1)
    tk_lanes = NSA_CHUNKS * Q_CHUNK
    c = c0 + lax.broadcasted_iota(jnp.int32, (1, tk_lanes), 1) // Q_CHUNK

    base = pl.multiple_of(c0 * Q_CHUNK, LANES)
    st = jnp.dot(kw_ref[0, 0, pl.ds(base, WIN_SPAN), :], q, preferred_element_type=f32)
    kpos = base - WINDOW + lax.broadcasted_iota(jnp.int32, (WIN_SPAN, 1), 0)
    d = tpos - kpos
    st = jnp.where((d >= 0) & (d < WINDOW) & (kpos >= 0), st, MASK_FILL)
    p_w = jnp.exp(st - jnp.max(st, axis=0, keepdims=True))
    pv_w = jnp.dot(vwT_ref[0, 0, :, pl.ds(base, WIN_SPAN)], p_w.astype(bf16), preferred_element_type=f32)
    o_win = pv_w[:HEAD_DIM] / pv_w[HEAD_DIM:HEAD_DIM + 1]

    dbase = pl.multiple_of(c0 * SLC_BLOCK, LANES)
    st = jnp.dot(ks_ref[0, 0, pl.ds(dbase, DIAG_SPAN), :HEAD_DIM], q, preferred_element_type=f32)
    kpos = dbase + lax.broadcasted_iota(jnp.int32, (DIAG_SPAN, 1), 0)
    st = jnp.where((kpos >= ccol * SLC_BLOCK) & (kpos <= tpos), st, MASK_FILL)
    m_d = jnp.max(st, axis=0, keepdims=True)
    pv_d = jnp.dot(vsT_ref[0, 0, :, pl.ds(dbase, DIAG_SPAN)], jnp.exp(st - m_d).astype(bf16),
                   preferred_element_type=f32)

    jidx = lax.broadcasted_iota(jnp.int32, (n_slc, tk_lanes), 0)
    forced = (jidx == 0) | (jidx == c) | (jidx == c - 1)
    work = jnp.where(forced | (jidx > c), -2.0, imp)
    sel = forced
    for _ in range(n_sel - 3):
        mx = jnp.max(work, axis=0, keepdims=True)
        first = jnp.min(jnp.where(work == mx, jidx, n_slc), axis=0, keepdims=True)
        hit = jidx == first
        sel = sel | hit
        work = jnp.where(hit, -2.0, work)
    early = (jnp.zeros_like(jidx) + c) < n_sel
    sel = (sel | early) & (jidx < c)
    bias = jnp.where(sel, 0.0, MASK_FILL)
    pieces = []
    for p in range(NSA_CHUNKS // 2):
        pair = bias[:, p * LANES:(p + 1) * LANES]
        swapped = pltpu.roll(pair, Q_CHUNK, axis=1)
        even = jnp.where(first_half, pair, swapped)
        odd = jnp.where(first_half, swapped, pair)
        pieces += [even, even, odd, odd]
    bias_ref[...] = jnp.concatenate(pieces, axis=1).astype(bf16)

    zeros_tail = jnp.zeros((LANES - HEAD_DIM - SLC_STEP_BLOCKS, ncols), bf16)

    def scores(step):
        rows = bias_ref[pl.ds(pl.multiple_of(step * SLC_STEP_BLOCKS, SLC_STEP_BLOCKS), SLC_STEP_BLOCKS), :]
        rhs = jnp.concatenate([q, rows, zeros_tail], axis=0)
        off = pl.multiple_of(step * SLC_STEP, SLC_STEP)
        st = jnp.dot(ks_ref[0, 0, pl.ds(off, SLC_STEP), :], rhs, preferred_element_type=f32)
        st_ref[step & 1] = st
        return jnp.max(st, axis=0, keepdims=True)

    def accumulate(step, mx, m, acc):
        m_new = jnp.maximum(m, mx)
        off = pl.multiple_of(step * SLC_STEP, SLC_STEP)
        pv = jnp.dot(vsT_ref[0, 0, :, pl.ds(off, SLC_STEP)], jnp.exp(st_ref[step & 1] - m_new).astype(bf16),
                     preferred_element_type=f32)
        return m_new, jnp.exp(m - m_new) * acc + pv

    n_steps = jnp.maximum((c_last + SLC_STEP_BLOCKS - 1) // SLC_STEP_BLOCKS, 1)

    def body(i, carry):
        mx, m, acc = carry
        m, acc = accumulate(i, mx, m, acc)
        return scores(i + 1), m, acc

    carry = (scores(0), jnp.full((1, ncols), MAX_FLOOR, f32), jnp.zeros((V_ROWS, ncols), f32))
    mx, m_s, acc = lax.fori_loop(0, n_steps - 1, body, carry)
    m_s, acc = accumulate(n_steps - 1, mx, m_s, acc)
    m_t = jnp.maximum(m_s, m_d)
    tot = jnp.exp(m_s - m_t) * acc + jnp.exp(m_d - m_t) * pv_d
    o_slc = tot[:HEAD_DIM] / tot[HEAD_DIM:HEAD_DIM + 1]

    g = jnp.concatenate([g_ref[0, 0, j] for j in range(NSA_CHUNKS)], axis=1)
    o = g[0:1] * o_cmp + g[1:2] * o_slc + g[2:3] * o_win
    for j in range(NSA_CHUNKS):
        o_ref[0, 0, j] = o[:, j * chunk_cols:(j + 1) * chunk_cols].astype(o_ref.dtype)


def nsa_attention_pallas(q2, kvs, gates2, cmp_pos, cmp_w1, cmp_w2, B, L, interpret=False):
    f32, bf16 = jnp.float32, jnp.bfloat16
    G, H, hd = N_KV, HPG, HEAD_DIM
    C = L // Q_CHUNK
    n16 = L // CMP_STRIDE
    n_cmp = (L - CMP_BLOCK) // CMP_STRIDE + 1
    ncp = -(-n_cmp // LANES) * LANES
    n_slc = L // SLC_BLOCK
    n_sel = min(SLC_TOPK, n_slc)
    k_cmp, v_cmp, k_slc, v_slc, k_win, v_win = kvs

    def compress(kv2, pos, w1, w2):
        x = kv2.reshape(B, n16, CMP_STRIDE, G, hd).transpose(0, 3, 1, 2, 4).reshape(B * G * n16, CMP_STRIDE * hd)
        half = CMP_STRIDE * hd
        wcat = jnp.concatenate([w1[:half], w1[half:]], axis=1)
        ab = pmm(x, wcat, interpret=interpret).reshape(B, G, n16, 2 * CMP_HIDDEN)
        posb = pmm(jnp.pad(pos.reshape(1, -1), ((0, 7), (0, 0))), w1, interpret=interpret)[0]
        hid = ab[:, :, :-1, :CMP_HIDDEN] + ab[:, :, 1:, CMP_HIDDEN:] + posb
        hid = jnp.pad(jax.nn.gelu(hid), ((0, 0), (0, 0), (0, ncp - n_cmp), (0, 0)))
        out = pmm(hid.reshape(B * G * ncp, CMP_HIDDEN), w2, interpret=interpret)
        return out.reshape(B, G, ncp, hd)

    kc = compress(k_cmp, cmp_pos[0], cmp_w1[0], cmp_w2[0]).astype(bf16)
    vcT = compress(v_cmp, cmp_pos[1], cmp_w1[1], cmp_w2[1]).astype(bf16).transpose(0, 1, 3, 2)

    def key_major(a):
        return a.astype(bf16).reshape(B, L, G, hd).transpose(0, 2, 1, 3)

    def dim_major(a):
        return a.astype(bf16).reshape(B, L, G, hd).transpose(0, 2, 3, 1)

    def with_ones_row(vT):
        n = vT.shape[-1]
        extra = jnp.zeros((B, G, V_ROWS - hd, n), bf16).at[:, :, 0, :].set(1.0)
        return jnp.concatenate([vT, extra], axis=2)

    blk_onehot = jax.nn.one_hot((jnp.arange(L) // SLC_BLOCK) % SLC_STEP_BLOCKS, LANES - hd, dtype=bf16)
    ks = jnp.concatenate([key_major(k_slc), jnp.broadcast_to(blk_onehot, (B, G, L, LANES - hd))], axis=-1)
    vsT = with_ones_row(dim_major(v_slc))
    kw = jnp.pad(key_major(k_win), ((0, 0), (0, 0), (WINDOW, 0), (0, 0)))
    vwT = with_ones_row(jnp.pad(dim_major(v_win), ((0, 0), (0, 0), (0, 0), (WINDOW, 0))))
    qT = q2.astype(bf16).reshape(B, C, Q_CHUNK, G, H, hd)
    qT = qT.transpose(0, 3, 1, 5, 4, 2).reshape(B, G, C, hd, H * Q_CHUNK)
    gT = gates2.astype(f32).reshape(B, C, Q_CHUNK, 3, G, H).transpose(0, 4, 1, 3, 5, 2).reshape(B, G, C, 3, H * Q_CHUNK)
    c_start = jnp.arange(ncp)[None, :] * CMP_STRIDE
    s_start = jnp.arange(n_slc)[:, None] * SLC_BLOCK
    ovT = ((c_start < s_start + SLC_BLOCK) & (c_start + CMP_BLOCK > s_start)
           & (jnp.arange(ncp)[None, :] < n_cmp)).astype(bf16)

    ncols = H * Q_CHUNK
    bg = lambda b, g, c: (b, g, 0, 0)
    bgc = lambda b, g, c: (b, g, c, 0, 0)
    oT = pl.pallas_call(
        functools.partial(_nsa_kernel, n_cmp=n_cmp, n_slc=n_slc, n_sel=n_sel),
        grid=(B, G, C // NSA_CHUNKS),
        in_specs=[pl.BlockSpec((1, 1, NSA_CHUNKS, hd, ncols), bgc),
                  pl.BlockSpec((1, 1, ncp, hd), bg),
                  pl.BlockSpec((1, 1, hd, ncp), bg),
                  pl.BlockSpec((1, 1, L, LANES), bg),
                  pl.BlockSpec((1, 1, V_ROWS, L), bg),
                  pl.BlockSpec((1, 1, L + WINDOW, hd), bg),
                  pl.BlockSpec((1, 1, V_ROWS, L + WINDOW), bg),
                  pl.BlockSpec((1, 1, NSA_CHUNKS, 3, ncols), bgc),
                  pl.BlockSpec((n_slc, ncp), lambda b, g, c: (0, 0))],
        out_specs=pl.BlockSpec((1, 1, NSA_CHUNKS, hd, ncols), bgc),
        out_shape=jax.ShapeDtypeStruct((B, G, C, hd, ncols), bf16),
        scratch_shapes=[pltpu.VMEM((n_slc, NSA_CHUNKS * ncols), bf16),
                        pltpu.VMEM((2, SLC_STEP, NSA_CHUNKS * ncols), f32)],
        compiler_params=pltpu.CompilerParams(
            dimension_semantics=("parallel", "parallel", "arbitrary"),
            vmem_limit_bytes=VMEM_LIMIT_BYTES),
        interpret=interpret,
        name="nsa_attention",
    )(qT, kc, vcT, ks, vsT, kw, vwT, gT, ovT)
    o = oT.reshape(B, G, C, hd, H, Q_CHUNK).transpose(0, 2, 5, 1, 4, 3)
    return o.reshape(B * L, G * H * hd)


S5_CHUNK = 32


def _s5_state_kernel(x_ref, ws_ref, s_ref):
    s_ref[0] = jnp.dot(x_ref[0].astype(jnp.bfloat16), ws_ref[0], preferred_element_type=jnp.float32)


def _s5_scan_kernel(sre_ref, sim_ref, are_ref, aim_ref, hre_ref, him_ref, *, chunks_per_batch):
    a_re = are_ref[...]
    a_im = aim_ref[...]
    n_batch = sre_ref.shape[0] // chunks_per_batch

    def step(c, carry):
        new = []
        for b in range(n_batch):
            h_re, h_im = carry[b]
            row = b * chunks_per_batch + c
            hre_ref[row] = h_re
            him_ref[row] = h_im
            new.append((a_re * h_re - a_im * h_im + sre_ref[row],
                        a_re * h_im + a_im * h_re + sim_ref[row]))
        return tuple(new)

    zero = jnp.zeros(a_re.shape, jnp.float32)
    lax.fori_loop(0, chunks_per_batch, step, tuple((zero, zero) for _ in range(n_batch)))


def _s5_out_kernel(x_ref, hp_ref, m_ref, wo_ref, d_ref, y_ref):
    x = x_ref[0]
    y = jnp.dot(x.astype(jnp.bfloat16), m_ref[0], preferred_element_type=jnp.float32)
    y = y + jnp.dot(hp_ref[0].astype(jnp.bfloat16), wo_ref[0], preferred_element_type=jnp.float32)
    y_ref[0] = y + x * d_ref[0]


def s5_scan_pallas(u2, lam_re, lam_im, log_dt, b_re, b_im, c_re, c_im, d_skip, B, L, interpret=False):
    f32, bf16 = jnp.float32, jnp.bfloat16
    G, P, H, T = SSM_GROUPS, SSM_STATE, SSM_GROUP, S5_CHUNK
    hi = lax.Precision.HIGHEST
    NC = B * L // T
    dt = jnp.exp(log_dt.astype(f32))[:, None]
    lr, li = lam_re.astype(f32), lam_im.astype(f32)
    mag = jnp.exp(lr * dt)
    ab_re, ab_im = mag * jnp.cos(li * dt), mag * jnp.sin(li * dt)
    den = lr * lr + li * li
    nr = ab_re - 1.0
    cr = (nr * lr + ab_im * li) / den
    cim = (ab_im * lr - nr * li) / den
    br, bim = b_re.astype(f32), b_im.astype(f32)
    bb_re = cr[..., None] * br - cim[..., None] * bim
    bb_im = cr[..., None] * bim + cim[..., None] * br
    k = jnp.arange(T + 1, dtype=f32)[:, None, None]
    pw_mag = jnp.exp(k * (lr * dt))
    pw_re = pw_mag * jnp.cos(k * (li * dt))
    pw_im = pw_mag * jnp.sin(k * (li * dt))
    cre, cimg = c_re.astype(f32), c_im.astype(f32)
    ab_r = pw_re[:T, :, :, None] * bb_re - pw_im[:T, :, :, None] * bb_im
    ab_i = pw_re[:T, :, :, None] * bb_im + pw_im[:T, :, :, None] * bb_re
    kern = (jnp.einsum('gop,tgpi->tgoi', cre, ab_r, precision=hi)
            - jnp.einsum('gop,tgpi->tgoi', cimg, ab_i, precision=hi))
    tt = jnp.arange(T)
    lag = tt[None, :] - tt[:, None]
    m = jnp.where((lag >= 0)[:, :, None, None, None], kern[jnp.clip(lag, 0, T - 1)], 0.0)
    m = m.transpose(2, 0, 4, 1, 3).reshape(G, T * H, T * H).astype(bf16)
    ws_re = ab_r[::-1].transpose(1, 0, 3, 2).reshape(G, T * H, P)
    ws_im = ab_i[::-1].transpose(1, 0, 3, 2).reshape(G, T * H, P)
    ws = jnp.concatenate([ws_re, ws_im], axis=-1).astype(bf16)
    ca_re = cre[None] * pw_re[1:, :, None, :] - cimg[None] * pw_im[1:, :, None, :]
    ca_im = cre[None] * pw_im[1:, :, None, :] + cimg[None] * pw_re[1:, :, None, :]
    wo = jnp.concatenate([ca_re.transpose(1, 3, 0, 2).reshape(G, P, T * H),
                          -ca_im.transpose(1, 3, 0, 2).reshape(G, P, T * H)], axis=1).astype(bf16)
    dtile = jnp.tile(d_skip.astype(f32), (1, T)).reshape(G, 1, T * H)

    xg = u2.astype(f32).reshape(NC, T, G, H).transpose(2, 0, 1, 3).reshape(G, NC, T * H)
    cparams = pltpu.CompilerParams(dimension_semantics=("parallel",), vmem_limit_bytes=VMEM_LIMIT_BYTES)
    s = pl.pallas_call(
        _s5_state_kernel, grid=(G,),
        in_specs=[pl.BlockSpec((1, NC, T * H), lambda g: (g, 0, 0)),
                  pl.BlockSpec((1, T * H, 2 * P), lambda g: (g, 0, 0))],
        out_specs=pl.BlockSpec((1, NC, 2 * P), lambda g: (g, 0, 0)),
        out_shape=jax.ShapeDtypeStruct((G, NC, 2 * P), f32),
        compiler_params=cparams, interpret=interpret, name="s5_chunk_state",
    )(xg, ws)
    sp = s.reshape(G, NC, 2, P).transpose(1, 2, 0, 3).reshape(NC, 2, G // 2, 2 * P)
    a_re = pw_re[T].reshape(G // 2, 2 * P)
    a_im = pw_im[T].reshape(G // 2, 2 * P)
    full = lambda shape: pl.BlockSpec(shape, lambda i: (0,) * len(shape))
    st_shape = (NC, G // 2, 2 * P)
    h_re, h_im = pl.pallas_call(
        functools.partial(_s5_scan_kernel, chunks_per_batch=L // T), grid=(1,),
        in_specs=[full(st_shape), full(st_shape), full((G // 2, 2 * P)), full((G // 2, 2 * P))],
        out_specs=[full(st_shape), full(st_shape)],
        out_shape=[jax.ShapeDtypeStruct(st_shape, f32)] * 2,
        compiler_params=pltpu.CompilerParams(dimension_semantics=("arbitrary",), vmem_limit_bytes=VMEM_LIMIT_BYTES),
        interpret=interpret, name="s5_state_scan",
    )(sp[:, 0], sp[:, 1], a_re, a_im)
    hp = jnp.stack([h_re, h_im], axis=1).reshape(NC, 2, G, P).transpose(2, 0, 1, 3).reshape(G, NC, 2 * P)
    y = pl.pallas_call(
        _s5_out_kernel, grid=(G,),
        in_specs=[pl.BlockSpec((1, NC, T * H), lambda g: (g, 0, 0)),
                  pl.BlockSpec((1, NC, 2 * P), lambda g: (g, 0, 0)),
                  pl.BlockSpec((1, T * H, T * H), lambda g: (g, 0, 0)),
                  pl.BlockSpec((1, 2 * P, T * H), lambda g: (g, 0, 0)),
                  pl.BlockSpec((1, 1, T * H), lambda g: (g, 0, 0))],
        out_specs=pl.BlockSpec((1, NC, T * H), lambda g: (g, 0, 0)),
        out_shape=jax.ShapeDtypeStruct((G, NC, T * H), f32),
        compiler_params=cparams, interpret=interpret, name="s5_chunk_out",
    )(xg, hp, m, wo, dtile)
    return y.reshape(G, NC, T, H).transpose(1, 2, 0, 3).reshape(B * L, G * H)


MOE_TILE = 512
MOE_SEG = 32
MOE_EXPERTS_PER_STEP = 4
MOE_SLOTS = TOPK_IN_GROUP * MOE_TILE + N_EXPERTS * MOE_SEG
ROUTE_E1, ROUTE_E2, ROUTE_W1, ROUTE_W2 = 0, 1, 2, 3


def _split_bf16(a):
    hi = a.astype(jnp.bfloat16)
    return hi, (a - hi.astype(jnp.float32)).astype(jnp.bfloat16)


def _moe_route_kernel(x_ref, g_ref, sc_ref, sh_ref, wr_ref, br_ref, h_ref, route_ref, routeT_ref, cnt_ref):
    f32 = jnp.float32
    x = x_ref[...]
    y = x * lax.rsqrt(jnp.mean(x * x, axis=-1, keepdims=True) + EPS)
    h = y * g_ref[...] * (1.0 + sc_ref[0]) + sh_ref[0]
    h_ref[...] = h.astype(h_ref.dtype)
    h_hi, h_lo = _split_bf16(h)
    w_hi, w_lo = _split_bf16(wr_ref[...])
    lg = (jnp.dot(h_hi, w_hi, preferred_element_type=f32) + jnp.dot(h_hi, w_lo, preferred_element_type=f32)
          + jnp.dot(h_lo, w_hi, preferred_element_type=f32) + br_ref[...])
    lane = lax.broadcasted_iota(jnp.int32, lg.shape, 1)
    ninf = -jnp.inf
    gmask = lane < N_GROUPS
    gmax = jnp.max(jnp.where(gmask, lg, ninf), axis=1, keepdims=True)
    gsum = jnp.sum(jnp.where(gmask, jnp.exp(lg - gmax), 0.0), axis=1, keepdims=True)
    g_idx = jnp.min(jnp.where(gmask & (lg == gmax), lane, LANES), axis=1, keepdims=True)
    g_p = 1.0 / gsum
    lo = N_GROUPS + EXP_PER_GROUP * g_idx
    emask = (lane >= lo) & (lane < lo + EXP_PER_GROUP)
    el = jnp.where(emask, lg, ninf)
    emax = jnp.max(el, axis=1, keepdims=True)
    i1 = jnp.min(jnp.where(el == emax, lane, LANES), axis=1, keepdims=True)
    el2 = jnp.where(lane == i1, ninf, el)
    emax2 = jnp.max(el2, axis=1, keepdims=True)
    i2 = jnp.min(jnp.where(el2 == emax2, lane, LANES), axis=1, keepdims=True)
    r = jnp.exp(emax2 - emax)
    w1 = g_p / (1.0 + r)
    w2 = g_p * r / (1.0 + r)
    e1 = (i1 - N_GROUPS).astype(f32)
    e2 = (i2 - N_GROUPS).astype(f32)
    route = jnp.where(lane == ROUTE_E1, e1, jnp.where(lane == ROUTE_E2, e2,
                      jnp.where(lane == ROUTE_W1, w1, jnp.where(lane == ROUTE_W2, w2, 0.0))))
    route_ref[...] = route
    routeT_ref[...] = route.T[:8, :]
    cnt = jnp.sum(jnp.where((lane == i1) | (lane == i2), 1.0, 0.0), axis=0, keepdims=True)
    cnt_ref[0] = jnp.broadcast_to(cnt, (8, LANES))


def _moe_ffn_kernel(seg_start_ref, seg_nblk_ref, h_ref, route_ref, routeT_ref, lrow_ref, lcol_ref, wgu_ref, wd_ref,
                    x_ref, gate_ref, fg_ref, o_ref, xs_ref, yw_ref, ws_ref, *, final_norm):
    f32, bf16 = jnp.float32, jnp.bfloat16
    t = pl.program_id(0)
    step = pl.program_id(1)
    tm = h_ref.shape[0]

    @pl.when(step == 0)
    def _dispatch():
        rT = routeT_ref[...]
        e1r, e2r = rT[ROUTE_E1:ROUTE_E1 + 1], rT[ROUTE_E2:ROUTE_E2 + 1]
        w1r, w2r = rT[ROUTE_W1:ROUTE_W1 + 1], rT[ROUTE_W2:ROUTE_W2 + 1]
        eio = lax.broadcasted_iota(jnp.int32, (N_EXPERTS, tm), 0).astype(f32)
        oh1 = eio == e1r
        oh2 = eio == e2r
        oh = jnp.where(oh1 | oh2, 1.0, 0.0).astype(bf16)
        before = (lax.broadcasted_iota(jnp.int32, (tm, tm), 0)
                  < lax.broadcasted_iota(jnp.int32, (tm, tm), 1))
        rank = jnp.dot(oh, jnp.where(before, 1.0, 0.0).astype(bf16), preferred_element_type=f32)
        pos = lcol_ref[0][:, :1] + rank
        d1 = jnp.sum(jnp.where(oh1, pos, 0.0), axis=0, keepdims=True)
        d2 = jnp.sum(jnp.where(oh2, pos, 0.0), axis=0, keepdims=True)
        slot = lax.broadcasted_iota(jnp.int32, (MOE_SLOTS, tm), 0).astype(f32)
        m1 = slot == d1
        m2 = slot == d2
        perm = jnp.where(m1 | m2, 1.0, 0.0).astype(bf16)
        xs_ref[...] = jnp.dot(perm, h_ref[...], preferred_element_type=f32).astype(bf16)
        ws_ref[...] = jnp.sum(jnp.where(m1, w1r, 0.0) + jnp.where(m2, w2r, 0.0), axis=1, keepdims=True)
        yw_ref[...] = jnp.zeros(yw_ref.shape, yw_ref.dtype)

    rows = 2 * MOE_SEG

    for j in range(MOE_EXPERTS_PER_STEP):
        e = step * MOE_EXPERTS_PER_STEP + j
        start = seg_start_ref[t * N_EXPERTS + e]
        nblk = seg_nblk_ref[t * N_EXPERTS + e]

        def block(i, _, start=start, j=j):
            r0 = pl.multiple_of(start + i * rows, MOE_SEG)
            xb = xs_ref[pl.ds(r0, rows), :]
            gu = jnp.dot(xb, wgu_ref[j], preferred_element_type=f32)
            a = jax.nn.silu(gu[:, :D_EXPERT]) * gu[:, D_EXPERT:]
            y = jnp.dot(a.astype(bf16), wd_ref[j], preferred_element_type=f32)
            yw_ref[pl.ds(r0, rows), :] = (y * ws_ref[pl.ds(r0, rows), :]).astype(bf16)
            return 0

        lax.fori_loop(0, (nblk + 1) // 2, block, 0)

    @pl.when(step == N_EXPERTS // MOE_EXPERTS_PER_STEP - 1)
    def _combine():
        r = route_ref[...]
        e1c, e2c = r[:, ROUTE_E1:ROUTE_E1 + 1], r[:, ROUTE_E2:ROUTE_E2 + 1]
        lane = lax.broadcasted_iota(jnp.int32, (tm, LANES), 1).astype(f32)
        oh1 = lane == e1c
        oh2 = lane == e2c
        oh = jnp.where(oh1 | oh2, 1.0, 0.0).astype(bf16)
        after = (lax.broadcasted_iota(jnp.int32, (tm, tm), 1)
                 < lax.broadcasted_iota(jnp.int32, (tm, tm), 0))
        rank = jnp.dot(jnp.where(after, 1.0, 0.0).astype(bf16), oh, preferred_element_type=f32)
        pos = lrow_ref[0][:1, :] + rank
        d1 = jnp.sum(jnp.where(oh1, pos, 0.0), axis=1, keepdims=True)
        d2 = jnp.sum(jnp.where(oh2, pos, 0.0), axis=1, keepdims=True)
        slot = lax.broadcasted_iota(jnp.int32, (tm, MOE_SLOTS), 1).astype(f32)
        comb = jnp.where((slot == d1) | (slot == d2), 1.0, 0.0).astype(bf16)
        moe = jnp.dot(comb, yw_ref[...], preferred_element_type=f32)
        xn = x_ref[...] + gate_ref[0] * moe
        if final_norm:
            xn = xn * lax.rsqrt(jnp.mean(xn * xn, axis=-1, keepdims=True) + EPS) * fg_ref[...]
        o_ref[...] = xn


def moe_layer_pallas(x2, norm_g, sc2, sh2, g2, w_group, b_group, w_expert, b_expert, w_gate, w_up, w_down,
                     final_g, final_norm, B, L, interpret=False):
    f32, bf16 = jnp.float32, jnp.bfloat16
    N, D = x2.shape
    tm = MOE_TILE
    n_tiles = N // tm
    tpb = L // tm
    wr = jnp.zeros((D, LANES), f32).at[:, :N_GROUPS].set(w_group).at[:, N_GROUPS:N_GROUPS + N_EXPERTS].set(w_expert)
    br = jnp.zeros((1, LANES), f32).at[0, :N_GROUPS].set(b_group).at[0, N_GROUPS:N_GROUPS + N_EXPERTS].set(b_expert)
    row = lambda v: v.reshape(1, D).astype(f32)
    per_batch = lambda v: v.reshape(B, 1, D).astype(f32)
    cp = pltpu.CompilerParams(dimension_semantics=("parallel",), vmem_limit_bytes=VMEM_LIMIT_BYTES)
    h, route, routeT, cnt = pl.pallas_call(
        _moe_route_kernel, grid=(n_tiles,),
        in_specs=[pl.BlockSpec((tm, D), lambda i: (i, 0)),
                  pl.BlockSpec((1, D), lambda i: (0, 0)),
                  pl.BlockSpec((1, 1, D), lambda i: (i // tpb, 0, 0)),
                  pl.BlockSpec((1, 1, D), lambda i: (i // tpb, 0, 0)),
                  pl.BlockSpec((D, LANES), lambda i: (0, 0)),
                  pl.BlockSpec((1, LANES), lambda i: (0, 0))],
        out_specs=[pl.BlockSpec((tm, D), lambda i: (i, 0)),
                   pl.BlockSpec((tm, LANES), lambda i: (i, 0)),
                   pl.BlockSpec((8, tm), lambda i: (0, i)),
                   pl.BlockSpec((1, 8, LANES), lambda i: (i, 0, 0))],
        out_shape=[jax.ShapeDtypeStruct((N, D), bf16), jax.ShapeDtypeStruct((N, LANES), f32),
                   jax.ShapeDtypeStruct((8, N), f32), jax.ShapeDtypeStruct((n_tiles, 8, LANES), f32)],
        compiler_params=cp, interpret=interpret, name="moe_route",
    )(x2, row(norm_g), per_batch(sc2), per_batch(sh2), wr, br)
    counts = cnt[:, 0, N_GROUPS:N_GROUPS + N_EXPERTS].astype(jnp.int32)
    nblk = (counts + MOE_SEG - 1) // MOE_SEG
    starts = (jnp.cumsum(nblk, axis=1) - nblk) * MOE_SEG
    starts_f = starts.astype(f32)
    lrow = jnp.zeros((n_tiles, 8, LANES), f32).at[:, :, :N_EXPERTS].set(starts_f[:, None, :])
    lcol = jnp.broadcast_to(starts_f[:, :, None], (n_tiles, N_EXPERTS, LANES))
    wgu = jnp.concatenate([w_gate, w_up], axis=-1).astype(bf16)
    wd = w_down.astype(bf16)
    tile = lambda t, e, *_: (t, 0)
    out = pl.pallas_call(
        functools.partial(_moe_ffn_kernel, final_norm=final_norm),
        grid_spec=pltpu.PrefetchScalarGridSpec(
            num_scalar_prefetch=2, grid=(n_tiles, N_EXPERTS // MOE_EXPERTS_PER_STEP),
            in_specs=[pl.BlockSpec((tm, D), tile),
                      pl.BlockSpec((tm, LANES), tile),
                      pl.BlockSpec((8, tm), lambda t, e, *_: (0, t)),
                      pl.BlockSpec((1, 8, LANES), lambda t, e, *_: (t, 0, 0)),
                      pl.BlockSpec((1, N_EXPERTS, LANES), lambda t, e, *_: (t, 0, 0)),
                      pl.BlockSpec((MOE_EXPERTS_PER_STEP, D, 2 * D_EXPERT), lambda t, e, *_: (e, 0, 0)),
                      pl.BlockSpec((MOE_EXPERTS_PER_STEP, D_EXPERT, D), lambda t, e, *_: (e, 0, 0)),
                      pl.BlockSpec((tm, D), tile),
                      pl.BlockSpec((1, 1, D), lambda t, e, *_: (t // tpb, 0, 0)),
                      pl.BlockSpec((1, D), lambda t, e, *_: (0, 0))],
            out_specs=pl.BlockSpec((tm, D), tile),
            scratch_shapes=[pltpu.VMEM((MOE_SLOTS, D), bf16), pltpu.VMEM((MOE_SLOTS, D), bf16),
                            pltpu.VMEM((MOE_SLOTS, 1), f32)]),
        out_shape=jax.ShapeDtypeStruct((N, D), f32),
        compiler_params=pltpu.CompilerParams(dimension_semantics=("parallel", "arbitrary"),
                                             vmem_limit_bytes=VMEM_LIMIT_BYTES),
        interpret=interpret, name="moe_ffn",
    )(starts.reshape(-1), nblk.reshape(-1), h, route, routeT, lrow, lcol, wgu, wd, x2, per_batch(g2), row(final_g))
    return out


ROW_TILE = 256
GATE_LANES = 128


def _in_proj_kernel(x_ref, g_ref, sc_ref, sh_ref, wq_ref, wkv_ref, wg_ref, wu_ref, wm_ref,
                    q_ref, kv_ref, ng_ref, u_ref, mg_ref):
    f32 = jnp.float32
    x = x_ref[...]
    y = x * lax.rsqrt(jnp.mean(x * x, axis=-1, keepdims=True) + EPS)
    h = (y * g_ref[...] * (1.0 + sc_ref[0]) + sh_ref[0]).astype(jnp.bfloat16)
    q = jnp.dot(h, wq_ref[...], preferred_element_type=f32)
    q_ref[...] = (q * HEAD_DIM ** -0.5).astype(q_ref.dtype)
    kv_ref[...] = jnp.dot(h, wkv_ref[...], preferred_element_type=f32).astype(kv_ref.dtype)
    ng_ref[...] = jax.nn.sigmoid(jnp.dot(h, wg_ref[...], preferred_element_type=f32))
    u_ref[...] = jnp.dot(h, wu_ref[...], preferred_element_type=f32)
    mg_ref[...] = jax.nn.sigmoid(jnp.dot(h, wm_ref[...], preferred_element_type=f32)).astype(mg_ref.dtype)


def in_proj_pallas(x2, norm_g, sc1, sh1, w_in, B, L, interpret=False):
    f32, bf16 = jnp.float32, jnp.bfloat16
    N, D = x2.shape
    tm = ROW_TILE
    tpb = L // tm
    o1 = ATTN_WIDTH
    o2 = o1 + 6 * KV_WIDTH
    o3 = o2 + 3 * N_HEADS
    o4 = o3 + SSM_WIDTH
    wb = w_in.astype(bf16)
    wq, wkv, wu, wm = wb[:, :o1], wb[:, o1:o2], wb[:, o3:o4], wb[:, o4:]
    wg = jnp.pad(wb[:, o2:o3], ((0, 0), (0, GATE_LANES - 3 * N_HEADS)))
    const = lambda shape: pl.BlockSpec(shape, lambda i: (0,) * len(shape))
    rows = lambda w: pl.BlockSpec((tm, w), lambda i: (i, 0))
    per_batch = pl.BlockSpec((1, 1, D), lambda i: (i // tpb, 0, 0))
    return pl.pallas_call(
        _in_proj_kernel, grid=(N // tm,),
        in_specs=[rows(D), const((1, D)), per_batch, per_batch,
                  const(wq.shape), const(wkv.shape), const(wg.shape), const(wu.shape), const(wm.shape)],
        out_specs=[rows(o1), rows(o2 - o1), rows(GATE_LANES), rows(SSM_WIDTH), rows(2 * D)],
        out_shape=[jax.ShapeDtypeStruct((N, o1), bf16), jax.ShapeDtypeStruct((N, o2 - o1), bf16),
                   jax.ShapeDtypeStruct((N, GATE_LANES), f32), jax.ShapeDtypeStruct((N, SSM_WIDTH), f32),
                   jax.ShapeDtypeStruct((N, 2 * D), bf16)],
        compiler_params=pltpu.CompilerParams(dimension_semantics=("parallel",), vmem_limit_bytes=VMEM_LIMIT_BYTES),
        interpret=interpret, name="in_proj",
    )(x2, norm_g.reshape(1, D).astype(f32), sc1.reshape(B, 1, D).astype(f32), sh1.reshape(B, 1, D).astype(f32),
      wq, wkv, wg, wu, wm)


def _merge_kernel(attn_ref, y_ref, mg_ref, x_ref, g1_ref, wglu_ref, wso_ref, wao_ref, wout_ref, o_ref):
    f32, bf16 = jnp.float32, jnp.bfloat16
    D = x_ref.shape[1]
    y = jax.nn.gelu(y_ref[...])
    ssm = y * jax.nn.sigmoid(jnp.dot(y.astype(bf16), wglu_ref[...], preferred_element_type=f32))
    a = jnp.dot(attn_ref[...], wao_ref[...], preferred_element_type=f32)
    s = jnp.dot(ssm.astype(bf16), wso_ref[...], preferred_element_type=f32)
    mg = mg_ref[...].astype(f32)
    merged = mg[:, :D] * a + mg[:, D:] * s
    out = jnp.dot(merged.astype(bf16), wout_ref[...], preferred_element_type=f32)
    o_ref[...] = x_ref[...] + g1_ref[0] * out


def merge_pallas(attn, y, mg, x2, g1, w_glu, ssm_w_o, nsa_w_o, w_out, B, L, interpret=False):
    bf16 = jnp.bfloat16
    N, D = x2.shape
    tm = ROW_TILE
    tpb = L // tm
    const = lambda shape: pl.BlockSpec(shape, lambda i: (0,) * len(shape))
    rows = lambda w: pl.BlockSpec((tm, w), lambda i: (i, 0))
    return pl.pallas_call(
        _merge_kernel, grid=(N // tm,),
        in_specs=[rows(ATTN_WIDTH), rows(SSM_WIDTH), rows(2 * D), rows(D),
                  pl.BlockSpec((1, 1, D), lambda i: (i // tpb, 0, 0)),
                  const(w_glu.shape), const(ssm_w_o.shape), const(nsa_w_o.shape), const(w_out.shape)],
        out_specs=rows(D),
        out_shape=jax.ShapeDtypeStruct((N, D), jnp.float32),
        compiler_params=pltpu.CompilerParams(dimension_semantics=("parallel",), vmem_limit_bytes=VMEM_LIMIT_BYTES),
        interpret=interpret, name="mixer_merge",
    )(attn, y, mg, x2, g1.reshape(B, 1, D).astype(jnp.float32),
      w_glu.astype(bf16), ssm_w_o.astype(bf16), nsa_w_o.astype(bf16), w_out.astype(bf16))


def kernel(x, c, ada_w, ada_b, norm1_g, w_in, ssm_lam_re, ssm_lam_im, ssm_log_dt, ssm_b_re, ssm_b_im, ssm_c_re, ssm_c_im, ssm_d, ssm_w_glu, ssm_w_o, nsa_cmp_pos, nsa_cmp_w1, nsa_cmp_w2, nsa_w_o, w_out, norm2_g, moe_w_group, moe_b_group, moe_w_expert, moe_b_expert, moe_w_gate, moe_w_up, moe_w_down, final_g):
    B, L, D = x.shape
    N = B * L
    x2 = x.reshape(N, D)
    cs = jnp.pad(jax.nn.silu(c), ((0, 8 - B), (0, 0)))
    for l in range(DEPTH):
        mod = pmm(cs, ada_w[l])[:B] + ada_b[l]
        sh1, sc1, g1, sh2, sc2, g2 = jnp.split(mod, 6, axis=-1)
        q, kv, ng, u, mg = in_proj_pallas(x2, norm1_g[l], sc1, sh1, w_in[l], B, L)
        kvs = [kv[:, i * KV_WIDTH:(i + 1) * KV_WIDTH] for i in range(6)]
        attn = nsa_attention_pallas(q, kvs, ng[:, :3 * N_HEADS], nsa_cmp_pos[l], nsa_cmp_w1[l], nsa_cmp_w2[l], B, L)
        y = s5_scan_pallas(u, ssm_lam_re[l], ssm_lam_im[l], ssm_log_dt[l], ssm_b_re[l], ssm_b_im[l],
                           ssm_c_re[l], ssm_c_im[l], ssm_d[l], B, L)
        x2 = merge_pallas(attn, y, mg, x2, g1, ssm_w_glu[l], ssm_w_o[l], nsa_w_o[l], w_out[l], B, L)
        x2 = moe_layer_pallas(x2, norm2_g[l], sc2, sh2, g2, moe_w_group[l], moe_b_group[l],
                              moe_w_expert[l], moe_b_expert[l], moe_w_gate[l], moe_w_up[l], moe_w_down[l],
                              final_g, l == DEPTH - 1, B, L)
    return x2.reshape(B, L, D)
```

```python
import functools
import math

import jax
import jax.numpy as jnp
from jax import lax
from jax.experimental import pallas as pl
from jax.experimental.pallas import tpu as pltpu

D_MODEL = 1024
DEPTH = 2
SSM_GROUP = 16
SSM_STATE = 64
SSM_WIDTH = D_MODEL // 2
SSM_GROUPS = SSM_WIDTH // SSM_GROUP
HEAD_DIM = 64
N_HEADS = D_MODEL // HEAD_DIM
N_KV = N_HEADS // 4
HPG = N_HEADS // N_KV
ATTN_WIDTH = N_HEADS * HEAD_DIM
KV_WIDTH = N_KV * HEAD_DIM
CMP_BLOCK = 32
CMP_STRIDE = 16
CMP_HIDDEN = 4 * HEAD_DIM
SLC_BLOCK = 64
SLC_TOPK = 16
WINDOW = 512
Q_CHUNK = 64
FORCED_SCORE = 1e4
N_GROUPS = 4
EXP_PER_GROUP = 8
N_EXPERTS = N_GROUPS * EXP_PER_GROUP
TOPK_IN_GROUP = 2
D_EXPERT = D_MODEL // 4
MOE_BLOCK = 128
EPS = 1e-6
NEG = -1e30

VMEM_LIMIT_BYTES = 48 * 1024 * 1024


def _mm_kernel(a_ref, b_ref, o_ref):
    a = a_ref[...].astype(jnp.bfloat16)
    b = b_ref[...].astype(jnp.bfloat16)
    o_ref[...] = jnp.dot(a, b, preferred_element_type=jnp.float32).astype(o_ref.dtype)


def pmm(a, b, tm=512, tn=512, out_dtype=jnp.float32, interpret=False):
    M, K = a.shape
    _, N = b.shape
    tm = min(tm, M)
    tn = min(tn, N)
    assert M % tm == 0 and N % tn == 0, (M, N, tm, tn)
    return pl.pallas_call(
        _mm_kernel,
        interpret=interpret,
        grid=(M // tm, N // tn),
        in_specs=[pl.BlockSpec((tm, K), lambda i, j: (i, 0)),
                  pl.BlockSpec((K, tn), lambda i, j: (0, j))],
        out_specs=pl.BlockSpec((tm, tn), lambda i, j: (i, j)),
        out_shape=jax.ShapeDtypeStruct((M, N), out_dtype),
        compiler_params=pltpu.CompilerParams(
            dimension_semantics=("parallel", "parallel"),
            vmem_limit_bytes=VMEM_LIMIT_BYTES),
    )(a, b)


MASK_FILL = -1e30
MAX_FLOOR = -1e29
SLC_STEP = 1024
SLC_STEP_BLOCKS = SLC_STEP // SLC_BLOCK
LANES = 128
V_ROWS = HEAD_DIM + 16
NSA_CHUNKS = 4
WIN_SPAN = WINDOW + NSA_CHUNKS * Q_CHUNK
DIAG_SPAN = NSA_CHUNKS * SLC_BLOCK


def _nsa_kernel(qT_ref, kc_ref, vcT_ref, ks_ref, vsT_ref, kw_ref, vwT_ref, g_ref, ovT_ref, o_ref, bias_ref, st_ref,
                *, n_cmp, n_slc, n_sel):
    f32 = jnp.float32
    bf16 = jnp.bfloat16
    c0 = pl.program_id(2) * NSA_CHUNKS
    c_last = c0 + NSA_CHUNKS - 1
    q = jnp.concatenate([qT_ref[0, 0, j] for j in range(NSA_CHUNKS)], axis=1)
    ncols = q.shape[1]
    chunk_cols = ncols // NSA_CHUNKS
    ncp = kc_ref.shape[2]
    lane = lax.broadcasted_iota(jnp.int32, (1, ncols), 1)
    ccol = c0 + lane // chunk_cols
    tpos = ccol * Q_CHUNK + (lane & (Q_CHUNK - 1))

    s = jnp.dot(kc_ref[0, 0], q, preferred_element_type=f32)
    n_idx = lax.broadcasted_iota(jnp.int32, (ncp, 1), 0)
    vis = (n_idx * CMP_STRIDE + (CMP_BLOCK - 1) <= tpos) & (n_idx < n_cmp)
    s = jnp.where(vis, s, MASK_FILL)
    m = jnp.max(s, axis=0, keepdims=True)
    p = jnp.where(vis, jnp.exp(s - m), 0.0)
    l = jnp.sum(p, axis=0, keepdims=True)
    p = p / jnp.where(l > 0.0, l, 1.0)
    p_hi = p.astype(bf16)
    o_cmp = jnp.dot(vcT_ref[0, 0], p_hi, preferred_element_type=f32)

    p_lo = (p - p_hi.astype(f32)).astype(bf16)
    ov = ovT_ref[...]
    imp4 = jnp.dot(ov, jnp.concatenate([p_hi, p_lo], axis=1), preferred_element_type=f32)
    imp4 = imp4[:, :ncols] + imp4[:, ncols:]
    lane128 = lax.broadcasted_iota(jnp.int32, (1, LANES), 1)
    first_half = lane128 < Q_CHUNK

    def head_sum(j):
        z = imp4[:, j * chunk_cols:j * chunk_cols + LANES] + imp4[:, j * chunk_cols + LANES:(j + 1) * chunk_cols]
        return z + pltpu.roll(z, Q_CHUNK, axis=1)

    imp = jnp.concatenate([jnp.where(first_half, head_sum(2 * p), head_sum(2 * p + 1))
                           for p in range(NSA_CHUNKS // 2)], axis=1)
    tk_lanes = NSA_CHUNKS * Q_CHUNK
    c = c0 + lax.broadcasted_iota(jnp.int32, (1, tk_lanes), 1) // Q_CHUNK

    base = pl.multiple_of(c0 * Q_CHUNK, LANES)
    st = jnp.dot(kw_ref[0, 0, pl.ds(base, WIN_SPAN), :], q, preferred_element_type=f32)
    kpos = base - WINDOW + lax.broadcasted_iota(jnp.int32, (WIN_SPAN, 1), 0)
    d = tpos - kpos
    st = jnp.where((d >= 0) & (d < WINDOW) & (kpos >= 0), st, MASK_FILL)
    p_w = jnp.exp(st - jnp.max(st, axis=0, keepdims=True))
    pv_w = jnp.dot(vwT_ref[0, 0, :, pl.ds(base, WIN_SPAN)], p_w.astype(bf16), preferred_element_type=f32)
    o_win = pv_w[:HEAD_DIM] / pv_w[HEAD_DIM:HEAD_DIM + 1]

    dbase = pl.multiple_of(c0 * SLC_BLOCK, LANES)
    st = jnp.dot(ks_ref[0, 0, pl.ds(dbase, DIAG_SPAN), :HEAD_DIM], q, preferred_element_type=f32)
    kpos = dbase + lax.broadcasted_iota(jnp.int32, (DIAG_SPAN, 1), 0)
    st = jnp.where((kpos >= ccol * SLC_BLOCK) & (kpos <= tpos), st, MASK_FILL)
    m_d = jnp.max(st, axis=0, keepdims=True)
    pv_d = jnp.dot(vsT_ref[0, 0, :, pl.ds(dbase, DIAG_SPAN)], jnp.exp(st - m_d).astype(bf16),
                   preferred_element_type=f32)

    jidx = lax.broadcasted_iota(jnp.int32, (n_slc, tk_lanes), 0)
    forced = (jidx == 0) | (jidx == c) | (jidx == c - 1)
    work = jnp.where(forced | (jidx > c), -2.0, imp)
    sel = forced
    for _ in range(n_sel - 3):
        mx = jnp.max(work, axis=0, keepdims=True)
        first = jnp.min(jnp.where(work == mx, jidx, n_slc), axis=0, keepdims=True)
        hit = jidx == first
        sel = sel | hit
        work = jnp.where(hit, -2.0, work)
    early = (jnp.zeros_like(jidx) + c) < n_sel
    sel = (sel | early) & (jidx < c)
    bias = jnp.where(sel, 0.0, MASK_FILL)
    pieces = []
    for p in range(NSA_CHUNKS // 2):
        pair = bias[:, p * LANES:(p + 1) * LANES]
        swapped = pltpu.roll(pair, Q_CHUNK, axis=1)
        even = jnp.where(first_half, pair, swapped)
        odd = jnp.where(first_half, swapped, pair)
        pieces += [even, even, odd, odd]
    bias_ref[...] = jnp.concatenate(pieces, axis=1).astype(bf16)

    zeros_tail = jnp.zeros((LANES - HEAD_DIM - SLC_STEP_BLOCKS, ncols), bf16)

    def scores(step):
        rows = bias_ref[pl.ds(pl.multiple_of(step * SLC_STEP_BLOCKS, SLC_STEP_BLOCKS), SLC_STEP_BLOCKS), :]
        rhs = jnp.concatenate([q, rows, zeros_tail], axis=0)
        off = pl.multiple_of(step * SLC_STEP, SLC_STEP)
        st = jnp.dot(ks_ref[0, 0, pl.ds(off, SLC_STEP), :], rhs, preferred_element_type=f32)
        st_ref[step & 1] = st
        return jnp.max(st, axis=0, keepdims=True)

    def accumulate(step, mx, m, acc):
        m_new = jnp.maximum(m, mx)
        off = pl.multiple_of(step * SLC_STEP, SLC_STEP)
        pv = jnp.dot(vsT_ref[0, 0, :, pl.ds(off, SLC_STEP)], jnp.exp(st_ref[step & 1] - m_new).astype(bf16),
                     preferred_element_type=f32)
        return m_new, jnp.exp(m - m_new) * acc + pv

    n_steps = jnp.maximum((c_last + SLC_STEP_BLOCKS - 1) // SLC_STEP_BLOCKS, 1)

    def body(i, carry):
        mx, m, acc = carry
        m, acc = accumulate(i, mx, m, acc)
        return scores(i + 1), m, acc

    carry = (scores(0), jnp.full((1, ncols), MAX_FLOOR, f32), jnp.zeros((V_ROWS, ncols), f32))
    mx, m_s, acc = lax.fori_loop(0, n_steps - 1, body, carry)
    m_s, acc = accumulate(n_steps - 1, mx, m_s, acc)
    m_t = jnp.maximum(m_s, m_d)
    tot = jnp.exp(m_s - m_t) * acc + jnp.exp(m_d - m_t) * pv_d
    o_slc = tot[:HEAD_DIM] / tot[HEAD_DIM:HEAD_DIM + 1]

    g = jnp.concatenate([g_ref[0, 0, j] for j in range(NSA_CHUNKS)], axis=1)
    o = g[0:1] * o_cmp + g[1:2] * o_slc + g[2:3] * o_win
    for j in range(NSA_CHUNKS):
        o_ref[0, 0, j] = o[:, j * chunk_cols:(j + 1) * chunk_cols].astype(o_ref.dtype)


def nsa_attention_pallas(q2, kvs, gates2, cmp_pos, cmp_w1, cmp_w2, B, L, interpret=False):
    f32, bf16 = jnp.float32, jnp.bfloat16
    G, H, hd = N_KV, HPG, HEAD_DIM
    C = L // Q_CHUNK
    n16 = L // CMP_STRIDE
    n_cmp = (L - CMP_BLOCK) // CMP_STRIDE + 1
    ncp = -(-n_cmp // LANES) * LANES
    n_slc = L // SLC_BLOCK
    n_sel = min(SLC_TOPK, n_slc)
    k_cmp, v_cmp, k_slc, v_slc, k_win, v_win = kvs

    def compress(kv2, pos, w1, w2):
        x = kv2.reshape(B, n16, CMP_STRIDE, G, hd).transpose(0, 3, 1, 2, 4).reshape(B * G * n16, CMP_STRIDE * hd)
        half = CMP_STRIDE * hd
        wcat = jnp.concatenate([w1[:half], w1[half:]], axis=1)
        ab = pmm(x, wcat, interpret=interpret).reshape(B, G, n16, 2 * CMP_HIDDEN)
        posb = pmm(jnp.pad(pos.reshape(1, -1), ((0, 7), (0, 0))), w1, interpret=interpret)[0]
        hid = ab[:, :, :-1, :CMP_HIDDEN] + ab[:, :, 1:, CMP_HIDDEN:] + posb
        hid = jnp.pad(jax.nn.gelu(hid), ((0, 0), (0, 0), (0, ncp - n_cmp), (0, 0)))
        out = pmm(hid.reshape(B * G * ncp, CMP_HIDDEN), w2, interpret=interpret)
        return out.reshape(B, G, ncp, hd)

    kc = compress(k_cmp, cmp_pos[0], cmp_w1[0], cmp_w2[0]).astype(bf16)
    vcT = compress(v_cmp, cmp_pos[1], cmp_w1[1], cmp_w2[1]).astype(bf16).transpose(0, 1, 3, 2)

    def key_major(a):
        return a.astype(bf16).reshape(B, L, G, hd).transpose(0, 2, 1, 3)

    def dim_major(a):
        return a.astype(bf16).reshape(B, L, G, hd).transpose(0, 2, 3, 1)

    def with_ones_row(vT):
        n = vT.shape[-1]
        extra = jnp.zeros((B, G, V_ROWS - hd, n), bf16).at[:, :, 0, :].set(1.0)
        return jnp.concatenate([vT, extra], axis=2)

    blk_onehot = jax.nn.one_hot((jnp.arange(L) // SLC_BLOCK) % SLC_STEP_BLOCKS, LANES - hd, dtype=bf16)
    ks = jnp.concatenate([key_major(k_slc), jnp.broadcast_to(blk_onehot, (B, G, L, LANES - hd))], axis=-1)
    vsT = with_ones_row(dim_major(v_slc))
    kw = jnp.pad(key_major(k_win), ((0, 0), (0, 0), (WINDOW, 0), (0, 0)))
    vwT = with_ones_row(jnp.pad(dim_major(v_win), ((0, 0), (0, 0), (0, 0), (WINDOW, 0))))
    qT = q2.astype(bf16).reshape(B, C, Q_CHUNK, G, H, hd)
    qT = qT.transpose(0, 3, 1, 5, 4, 2).reshape(B, G, C, hd, H * Q_CHUNK)
    gT = gates2.astype(f32).reshape(B, C, Q_CHUNK, 3, G, H).transpose(0, 4, 1, 3, 5, 2).reshape(B, G, C, 3, H * Q_CHUNK)
    c_start = jnp.arange(ncp)[None, :] * CMP_STRIDE
    s_start = jnp.arange(n_slc)[:, None] * SLC_BLOCK
    ovT = ((c_start < s_start + SLC_BLOCK) & (c_start + CMP_BLOCK > s_start)
           & (jnp.arange(ncp)[None, :] < n_cmp)).astype(bf16)

    ncols = H * Q_CHUNK
    bg = lambda b, g, c: (b, g, 0, 0)
    bgc = lambda b, g, c: (b, g, c, 0, 0)
    oT = pl.pallas_call(
        functools.partial(_nsa_kernel, n_cmp=n_cmp, n_slc=n_slc, n_sel=n_sel),
        grid=(B, G, C // NSA_CHUNKS),
        in_specs=[pl.BlockSpec((1, 1, NSA_CHUNKS, hd, ncols), bgc),
                  pl.BlockSpec((1, 1, ncp, hd), bg),
                  pl.BlockSpec((1, 1, hd, ncp), bg),
                  pl.BlockSpec((1, 1, L, LANES), bg),
                  pl.BlockSpec((1, 1, V_ROWS, L), bg),
                  pl.BlockSpec((1, 1, L + WINDOW, hd), bg),
                  pl.BlockSpec((1, 1, V_ROWS, L + WINDOW), bg),
                  pl.BlockSpec((1, 1, NSA_CHUNKS, 3, ncols), bgc),
                  pl.BlockSpec((n_slc, ncp), lambda b, g, c: (0, 0))],
        out_specs=pl.BlockSpec((1, 1, NSA_CHUNKS, hd, ncols), bgc),
        out_shape=jax.ShapeDtypeStruct((B, G, C, hd, ncols), bf16),
        scratch_shapes=[pltpu.VMEM((n_slc, NSA_CHUNKS * ncols), bf16),
                        pltpu.VMEM((2, SLC_STEP, NSA_CHUNKS * ncols), f32)],
        compiler_params=pltpu.CompilerParams(
            dimension_semantics=("parallel", "parallel", "arbitrary"),
            vmem_limit_bytes=VMEM_LIMIT_BYTES),
        interpret=interpret,
        name="nsa_attention",
    )(qT, kc, vcT, ks, vsT, kw, vwT, gT, ovT)
    o = oT.reshape(B, G, C, hd, H, Q_CHUNK).transpose(0, 2, 5, 1, 4, 3)
    return o.reshape(B * L, G * H * hd)


S5_CHUNK = 32


def _s5_state_kernel(x_ref, ws_ref, s_ref):
    s_ref[0] = jnp.dot(x_ref[0], ws_ref[0], preferred_element_type=jnp.float32)


def _s5_scan_kernel(sre_ref, sim_ref, are_ref, aim_ref, hre_ref, him_ref, *, chunks_per_batch):
    a_re = are_ref[...]
    a_im = aim_ref[...]
    n_batch = sre_ref.shape[0] // chunks_per_batch

    def step(c, carry):
        new = []
        for b in range(n_batch):
            h_re, h_im = carry[b]
            row = b * chunks_per_batch + c
            hre_ref[row] = h_re
            him_ref[row] = h_im
            new.append((a_re * h_re - a_im * h_im + sre_ref[row],
                        a_re * h_im + a_im * h_re + sim_ref[row]))
        return tuple(new)

    zero = jnp.zeros(a_re.shape, jnp.float32)
    lax.fori_loop(0, chunks_per_batch, step, tuple((zero, zero) for _ in range(n_batch)))


def _s5_out_kernel(x_ref, hp_ref, m_ref, wo_ref, y_ref):
    y = jnp.dot(x_ref[0], m_ref[0], preferred_element_type=jnp.float32)
    y = y + jnp.dot(hp_ref[0].astype(jnp.bfloat16), wo_ref[0], preferred_element_type=jnp.float32)
    y_ref[0] = y.astype(y_ref.dtype)


def s5_scan_pallas(u2, lam_re, lam_im, log_dt, b_re, b_im, c_re, c_im, B, L, interpret=False):
    f32, bf16 = jnp.float32, jnp.bfloat16
    G, P, H, T = SSM_GROUPS, SSM_STATE, SSM_GROUP, S5_CHUNK
    hi = lax.Precision.HIGHEST
    NC = B * L // T
    dt = jnp.exp(log_dt.astype(f32))[:, None]
    lr, li = lam_re.astype(f32), lam_im.astype(f32)
    mag = jnp.exp(lr * dt)
    ab_re, ab_im = mag * jnp.cos(li * dt), mag * jnp.sin(li * dt)
    den = lr * lr + li * li
    nr = ab_re - 1.0
    cr = (nr * lr + ab_im * li) / den
    cim = (ab_im * lr - nr * li) / den
    br, bim = b_re.astype(f32), b_im.astype(f32)
    bb_re = cr[..., None] * br - cim[..., None] * bim
    bb_im = cr[..., None] * bim + cim[..., None] * br
    k = jnp.arange(T + 1, dtype=f32)[:, None, None]
    pw_mag = jnp.exp(k * (lr * dt))
    pw_re = pw_mag * jnp.cos(k * (li * dt))
    pw_im = pw_mag * jnp.sin(k * (li * dt))
    cre, cimg = c_re.astype(f32), c_im.astype(f32)
    ab_r = pw_re[:T, :, :, None] * bb_re - pw_im[:T, :, :, None] * bb_im
    ab_i = pw_re[:T, :, :, None] * bb_im + pw_im[:T, :, :, None] * bb_re
    kern = (jnp.einsum('gop,tgpi->tgoi', cre, ab_r, precision=hi)
            - jnp.einsum('gop,tgpi->tgoi', cimg, ab_i, precision=hi))
    tt = jnp.arange(T)
    lag = tt[None, :] - tt[:, None]
    m = jnp.where((lag >= 0)[:, :, None, None, None], kern[jnp.clip(lag, 0, T - 1)], 0.0)
    m = m.transpose(2, 0, 4, 1, 3).reshape(G, T * H, T * H).astype(bf16)
    ws_re = ab_r[::-1].transpose(1, 0, 3, 2).reshape(G, T * H, P)
    ws_im = ab_i[::-1].transpose(1, 0, 3, 2).reshape(G, T * H, P)
    ws = jnp.concatenate([ws_re, ws_im], axis=-1).astype(bf16)
    ca_re = cre[None] * pw_re[1:, :, None, :] - cimg[None] * pw_im[1:, :, None, :]
    ca_im = cre[None] * pw_im[1:, :, None, :] + cimg[None] * pw_re[1:, :, None, :]
    wo = jnp.concatenate([ca_re.transpose(1, 3, 0, 2).reshape(G, P, T * H),
                          -ca_im.transpose(1, 3, 0, 2).reshape(G, P, T * H)], axis=1).astype(bf16)

    xg = u2.astype(bf16).reshape(NC, T, G, H).transpose(2, 0, 1, 3).reshape(G, NC, T * H)
    cparams = pltpu.CompilerParams(dimension_semantics=("parallel",), vmem_limit_bytes=VMEM_LIMIT_BYTES)
    s = pl.pallas_call(
        _s5_state_kernel, grid=(G,),
        in_specs=[pl.BlockSpec((1, NC, T * H), lambda g: (g, 0, 0)),
                  pl.BlockSpec((1, T * H, 2 * P), lambda g: (g, 0, 0))],
        out_specs=pl.BlockSpec((1, NC, 2 * P), lambda g: (g, 0, 0)),
        out_shape=jax.ShapeDtypeStruct((G, NC, 2 * P), f32),
        compiler_params=cparams, interpret=interpret, name="s5_chunk_state",
    )(xg, ws)
    sp = s.reshape(G, NC, 2, P).transpose(1, 2, 0, 3).reshape(NC, 2, G // 2, 2 * P)
    a_re = pw_re[T].reshape(G // 2, 2 * P)
    a_im = pw_im[T].reshape(G // 2, 2 * P)
    full = lambda shape: pl.BlockSpec(shape, lambda i: (0,) * len(shape))
    st_shape = (NC, G // 2, 2 * P)
    h_re, h_im = pl.pallas_call(
        functools.partial(_s5_scan_kernel, chunks_per_batch=L // T), grid=(1,),
        in_specs=[full(st_shape), full(st_shape), full((G // 2, 2 * P)), full((G // 2, 2 * P))],
        out_specs=[full(st_shape), full(st_shape)],
        out_shape=[jax.ShapeDtypeStruct(st_shape, f32)] * 2,
        compiler_params=pltpu.CompilerParams(dimension_semantics=("arbitrary",), vmem_limit_bytes=VMEM_LIMIT_BYTES),
        interpret=interpret, name="s5_state_scan",
    )(sp[:, 0], sp[:, 1], a_re, a_im)
    hp = jnp.stack([h_re, h_im], axis=1).reshape(NC, 2, G, P).transpose(2, 0, 1, 3).reshape(G, NC, 2 * P)
    y = pl.pallas_call(
        _s5_out_kernel, grid=(G,),
        in_specs=[pl.BlockSpec((1, NC, T * H), lambda g: (g, 0, 0)),
                  pl.BlockSpec((1, NC, 2 * P), lambda g: (g, 0, 0)),
                  pl.BlockSpec((1, T * H, T * H), lambda g: (g, 0, 0)),
                  pl.BlockSpec((1, 2 * P, T * H), lambda g: (g, 0, 0))],
        out_specs=pl.BlockSpec((1, NC, T * H), lambda g: (g, 0, 0)),
        out_shape=jax.ShapeDtypeStruct((G, NC, T * H), bf16),
        compiler_params=cparams, interpret=interpret, name="s5_chunk_out",
    )(xg, hp, m, wo)
    return y.reshape(G, NC, T, H).transpose(1, 2, 0, 3).reshape(B * L, G * H)


MOE_TILE = 512
MOE_SEG = 32
MOE_EXPERTS_PER_STEP = 4
MOE_SLOTS = TOPK_IN_GROUP * MOE_TILE + N_EXPERTS * MOE_SEG
ROUTE_E1, ROUTE_E2, ROUTE_W1, ROUTE_W2 = 0, 1, 2, 3


def _split_bf16(a):
    hi = a.astype(jnp.bfloat16)
    return hi, (a - hi.astype(jnp.float32)).astype(jnp.bfloat16)


def _moe_route_kernel(x_ref, g_ref, sc_ref, sh_ref, wr_ref, br_ref, h_ref, route_ref, routeT_ref, cnt_ref):
    f32 = jnp.float32
    x = x_ref[...]
    y = x * lax.rsqrt(jnp.mean(x * x, axis=-1, keepdims=True) + EPS)
    h = y * g_ref[...] * (1.0 + sc_ref[0]) + sh_ref[0]
    h_ref[...] = h.astype(h_ref.dtype)
    h_hi, h_lo = _split_bf16(h)
    w_hi, w_lo = _split_bf16(wr_ref[...])
    lg = (jnp.dot(h_hi, w_hi, preferred_element_type=f32) + jnp.dot(h_hi, w_lo, preferred_element_type=f32)
          + jnp.dot(h_lo, w_hi, preferred_element_type=f32) + br_ref[...])
    lane = lax.broadcasted_iota(jnp.int32, lg.shape, 1)
    ninf = -jnp.inf
    gmask = lane < N_GROUPS
    gmax = jnp.max(jnp.where(gmask, lg, ninf), axis=1, keepdims=True)
    gsum = jnp.sum(jnp.where(gmask, jnp.exp(lg - gmax), 0.0), axis=1, keepdims=True)
    g_idx = jnp.min(jnp.where(gmask & (lg == gmax), lane, LANES), axis=1, keepdims=True)
    g_p = 1.0 / gsum
    lo = N_GROUPS + EXP_PER_GROUP * g_idx
    emask = (lane >= lo) & (lane < lo + EXP_PER_GROUP)
    el = jnp.where(emask, lg, ninf)
    emax = jnp.max(el, axis=1, keepdims=True)
    i1 = jnp.min(jnp.where(el == emax, lane, LANES), axis=1, keepdims=True)
    el2 = jnp.where(lane == i1, ninf, el)
    emax2 = jnp.max(el2, axis=1, keepdims=True)
    i2 = jnp.min(jnp.where(el2 == emax2, lane, LANES), axis=1, keepdims=True)
    r = jnp.exp(emax2 - emax)
    w1 = g_p / (1.0 + r)
    w2 = g_p * r / (1.0 + r)
    e1 = (i1 - N_GROUPS).astype(f32)
    e2 = (i2 - N_GROUPS).astype(f32)
    route = jnp.where(lane == ROUTE_E1, e1, jnp.where(lane == ROUTE_E2, e2,
                      jnp.where(lane == ROUTE_W1, w1, jnp.where(lane == ROUTE_W2, w2, 0.0))))
    route_ref[...] = route
    routeT_ref[...] = route.T[:8, :]
    cnt = jnp.sum(jnp.where((lane == i1) | (lane == i2), 1.0, 0.0), axis=0, keepdims=True)
    cnt_ref[0] = jnp.broadcast_to(cnt, (8, LANES))


def _moe_ffn_kernel(seg_start_ref, seg_nblk_ref, h_ref, route_ref, routeT_ref, lrow_ref, lcol_ref, wgu_ref, wd_ref,
                    x_ref, gate_ref, fg_ref, o_ref, xs_ref, yw_ref, ws_ref, *, final_norm):
    f32, bf16 = jnp.float32, jnp.bfloat16
    t = pl.program_id(0)
    step = pl.program_id(1)
    tm = h_ref.shape[0]

    @pl.when(step == 0)
    def _dispatch():
        rT = routeT_ref[...]
        e1r, e2r = rT[ROUTE_E1:ROUTE_E1 + 1], rT[ROUTE_E2:ROUTE_E2 + 1]
        w1r, w2r = rT[ROUTE_W1:ROUTE_W1 + 1], rT[ROUTE_W2:ROUTE_W2 + 1]
        eio = lax.broadcasted_iota(jnp.int32, (N_EXPERTS, tm), 0).astype(f32)
        oh1 = eio == e1r
        oh2 = eio == e2r
        oh = jnp.where(oh1 | oh2, 1.0, 0.0).astype(bf16)
        before = (lax.broadcasted_iota(jnp.int32, (tm, tm), 0)
                  < lax.broadcasted_iota(jnp.int32, (tm, tm), 1))
        rank = jnp.dot(oh, jnp.where(before, 1.0, 0.0).astype(bf16), preferred_element_type=f32)
        pos = lcol_ref[0][:, :1] + rank
        d1 = jnp.sum(jnp.where(oh1, pos, 0.0), axis=0, keepdims=True)
        d2 = jnp.sum(jnp.where(oh2, pos, 0.0), axis=0, keepdims=True)
        slot = lax.broadcasted_iota(jnp.int32, (MOE_SLOTS, tm), 0).astype(f32)
        m1 = slot == d1
        m2 = slot == d2
        perm = jnp.where(m1 | m2, 1.0, 0.0).astype(bf16)
        xs_ref[...] = jnp.dot(perm, h_ref[...], preferred_element_type=f32).astype(bf16)
        ws_ref[...] = jnp.sum(jnp.where(m1, w1r, 0.0) + jnp.where(m2, w2r, 0.0), axis=1, keepdims=True)
        yw_ref[...] = jnp.zeros(yw_ref.shape, yw_ref.dtype)

    rows = 2 * MOE_SEG

    for j in range(MOE_EXPERTS_PER_STEP):
        e = step * MOE_EXPERTS_PER_STEP + j
        start = seg_start_ref[t * N_EXPERTS + e]
        nblk = seg_nblk_ref[t * N_EXPERTS + e]

        def block(i, _, start=start, j=j):
            r0 = pl.multiple_of(start + i * rows, MOE_SEG)
            xb = xs_ref[pl.ds(r0, rows), :]
            gu = jnp.dot(xb, wgu_ref[j], preferred_element_type=f32)
            a = jax.nn.silu(gu[:, :D_EXPERT]) * gu[:, D_EXPERT:]
            y = jnp.dot(a.astype(bf16), wd_ref[j], preferred_element_type=f32)
            yw_ref[pl.ds(r0, rows), :] = (y * ws_ref[pl.ds(r0, rows), :]).astype(bf16)
            return 0

        lax.fori_loop(0, (nblk + 1) // 2, block, 0)

    @pl.when(step == N_EXPERTS // MOE_EXPERTS_PER_STEP - 1)
    def _combine():
        r = route_ref[...]
        e1c, e2c = r[:, ROUTE_E1:ROUTE_E1 + 1], r[:, ROUTE_E2:ROUTE_E2 + 1]
        lane = lax.broadcasted_iota(jnp.int32, (tm, LANES), 1).astype(f32)
        oh1 = lane == e1c
        oh2 = lane == e2c
        oh = jnp.where(oh1 | oh2, 1.0, 0.0).astype(bf16)
        after = (lax.broadcasted_iota(jnp.int32, (tm, tm), 1)
                 < lax.broadcasted_iota(jnp.int32, (tm, tm), 0))
        rank = jnp.dot(jnp.where(after, 1.0, 0.0).astype(bf16), oh, preferred_element_type=f32)
        pos = lrow_ref[0][:1, :] + rank
        d1 = jnp.sum(jnp.where(oh1, pos, 0.0), axis=1, keepdims=True)
        d2 = jnp.sum(jnp.where(oh2, pos, 0.0), axis=1, keepdims=True)
        slot = lax.broadcasted_iota(jnp.int32, (tm, MOE_SLOTS), 1).astype(f32)
        comb = jnp.where((slot == d1) | (slot == d2), 1.0, 0.0).astype(bf16)
        moe = jnp.dot(comb, yw_ref[...], preferred_element_type=f32)
        xn = x_ref[...] + gate_ref[0] * moe
        if final_norm:
            xn = xn * lax.rsqrt(jnp.mean(xn * xn, axis=-1, keepdims=True) + EPS) * fg_ref[...]
        o_ref[...] = xn


def moe_layer_pallas(x2, norm_g, sc2, sh2, g2, w_group, b_group, w_expert, b_expert, w_gate, w_up, w_down,
                     final_g, final_norm, B, L, interpret=False):
    f32, bf16 = jnp.float32, jnp.bfloat16
    N, D = x2.shape
    tm = MOE_TILE
    n_tiles = N // tm
    tpb = L // tm
    wr = jnp.zeros((D, LANES), f32).at[:, :N_GROUPS].set(w_group).at[:, N_GROUPS:N_GROUPS + N_EXPERTS].set(w_expert)
    br = jnp.zeros((1, LANES), f32).at[0, :N_GROUPS].set(b_group).at[0, N_GROUPS:N_GROUPS + N_EXPERTS].set(b_expert)
    row = lambda v: v.reshape(1, D).astype(f32)
    per_batch = lambda v: v.reshape(B, 1, D).astype(f32)
    cp = pltpu.CompilerParams(dimension_semantics=("parallel",), vmem_limit_bytes=VMEM_LIMIT_BYTES)
    h, route, routeT, cnt = pl.pallas_call(
        _moe_route_kernel, grid=(n_tiles,),
        in_specs=[pl.BlockSpec((tm, D), lambda i: (i, 0)),
                  pl.BlockSpec((1, D), lambda i: (0, 0)),
                  pl.BlockSpec((1, 1, D), lambda i: (i // tpb, 0, 0)),
                  pl.BlockSpec((1, 1, D), lambda i: (i // tpb, 0, 0)),
                  pl.BlockSpec((D, LANES), lambda i: (0, 0)),
                  pl.BlockSpec((1, LANES), lambda i: (0, 0))],
        out_specs=[pl.BlockSpec((tm, D), lambda i: (i, 0)),
                   pl.BlockSpec((tm, LANES), lambda i: (i, 0)),
                   pl.BlockSpec((8, tm), lambda i: (0, i)),
                   pl.BlockSpec((1, 8, LANES), lambda i: (i, 0, 0))],
        out_shape=[jax.ShapeDtypeStruct((N, D), bf16), jax.ShapeDtypeStruct((N, LANES), f32),
                   jax.ShapeDtypeStruct((8, N), f32), jax.ShapeDtypeStruct((n_tiles, 8, LANES), f32)],
        compiler_params=cp, interpret=interpret, name="moe_route",
    )(x2, row(norm_g), per_batch(sc2), per_batch(sh2), wr, br)
    counts = cnt[:, 0, N_GROUPS:N_GROUPS + N_EXPERTS].astype(jnp.int32)
    nblk = (counts + MOE_SEG - 1) // MOE_SEG
    starts = (jnp.cumsum(nblk, axis=1) - nblk) * MOE_SEG
    starts_f = starts.astype(f32)
    lrow = jnp.zeros((n_tiles, 8, LANES), f32).at[:, :, :N_EXPERTS].set(starts_f[:, None, :])
    lcol = jnp.broadcast_to(starts_f[:, :, None], (n_tiles, N_EXPERTS, LANES))
    wgu = jnp.concatenate([w_gate, w_up], axis=-1).astype(bf16)
    wd = w_down.astype(bf16)
    tile = lambda t, e, *_: (t, 0)
    out = pl.pallas_call(
        functools.partial(_moe_ffn_kernel, final_norm=final_norm),
        grid_spec=pltpu.PrefetchScalarGridSpec(
            num_scalar_prefetch=2, grid=(n_tiles, N_EXPERTS // MOE_EXPERTS_PER_STEP),
            in_specs=[pl.BlockSpec((tm, D), tile),
                      pl.BlockSpec((tm, LANES), tile),
                      pl.BlockSpec((8, tm), lambda t, e, *_: (0, t)),
                      pl.BlockSpec((1, 8, LANES), lambda t, e, *_: (t, 0, 0)),
                      pl.BlockSpec((1, N_EXPERTS, LANES), lambda t, e, *_: (t, 0, 0)),
                      pl.BlockSpec((MOE_EXPERTS_PER_STEP, D, 2 * D_EXPERT), lambda t, e, *_: (e, 0, 0)),
                      pl.BlockSpec((MOE_EXPERTS_PER_STEP, D_EXPERT, D), lambda t, e, *_: (e, 0, 0)),
                      pl.BlockSpec((tm, D), tile),
                      pl.BlockSpec((1, 1, D), lambda t, e, *_: (t // tpb, 0, 0)),
                      pl.BlockSpec((1, D), lambda t, e, *_: (0, 0))],
            out_specs=pl.BlockSpec((tm, D), tile),
            scratch_shapes=[pltpu.VMEM((MOE_SLOTS, D), bf16), pltpu.VMEM((MOE_SLOTS, D), bf16),
                            pltpu.VMEM((MOE_SLOTS, 1), f32)]),
        out_shape=jax.ShapeDtypeStruct((N, D), f32),
        compiler_params=pltpu.CompilerParams(dimension_semantics=("parallel", "arbitrary"),
                                             vmem_limit_bytes=VMEM_LIMIT_BYTES),
        interpret=interpret, name="moe_ffn",
    )(starts.reshape(-1), nblk.reshape(-1), h, route, routeT, lrow, lcol, wgu, wd, x2, per_batch(g2), row(final_g))
    return out


ROW_TILE = 256
GATE_LANES = 128


def _in_proj_kernel(x_ref, g_ref, sc_ref, sh_ref, wq_ref, wkv_ref, wg_ref, wu_ref, wm_ref,
                    q_ref, kv_ref, ng_ref, u_ref, mg_ref):
    f32 = jnp.float32
    x = x_ref[...]
    y = x * lax.rsqrt(jnp.mean(x * x, axis=-1, keepdims=True) + EPS)
    h = (y * g_ref[...] * (1.0 + sc_ref[0]) + sh_ref[0]).astype(jnp.bfloat16)
    q = jnp.dot(h, wq_ref[...], preferred_element_type=f32)
    q_ref[...] = (q * HEAD_DIM ** -0.5).astype(q_ref.dtype)
    kv_ref[...] = jnp.dot(h, wkv_ref[...], preferred_element_type=f32).astype(kv_ref.dtype)
    ng_ref[...] = jax.nn.sigmoid(jnp.dot(h, wg_ref[...], preferred_element_type=f32))
    u_ref[...] = jnp.dot(h, wu_ref[...], preferred_element_type=f32)
    mg_ref[...] = jax.nn.sigmoid(jnp.dot(h, wm_ref[...], preferred_element_type=f32)).astype(mg_ref.dtype)


def in_proj_pallas(x2, norm_g, sc1, sh1, w_in, B, L, interpret=False):
    f32, bf16 = jnp.float32, jnp.bfloat16
    N, D = x2.shape
    tm = ROW_TILE
    tpb = L // tm
    o1 = ATTN_WIDTH
    o2 = o1 + 6 * KV_WIDTH
    o3 = o2 + 3 * N_HEADS
    o4 = o3 + SSM_WIDTH
    wb = w_in.astype(bf16)
    wq, wkv, wu, wm = wb[:, :o1], wb[:, o1:o2], wb[:, o3:o4], wb[:, o4:]
    wg = jnp.pad(wb[:, o2:o3], ((0, 0), (0, GATE_LANES - 3 * N_HEADS)))
    const = lambda shape: pl.BlockSpec(shape, lambda i: (0,) * len(shape))
    rows = lambda w: pl.BlockSpec((tm, w), lambda i: (i, 0))
    per_batch = pl.BlockSpec((1, 1, D), lambda i: (i // tpb, 0, 0))
    return pl.pallas_call(
        _in_proj_kernel, grid=(N // tm,),
        in_specs=[rows(D), const((1, D)), per_batch, per_batch,
                  const(wq.shape), const(wkv.shape), const(wg.shape), const(wu.shape), const(wm.shape)],
        out_specs=[rows(o1), rows(o2 - o1), rows(GATE_LANES), rows(SSM_WIDTH), rows(2 * D)],
        out_shape=[jax.ShapeDtypeStruct((N, o1), bf16), jax.ShapeDtypeStruct((N, o2 - o1), bf16),
                   jax.ShapeDtypeStruct((N, GATE_LANES), f32), jax.ShapeDtypeStruct((N, SSM_WIDTH), f32),
                   jax.ShapeDtypeStruct((N, 2 * D), bf16)],
        compiler_params=pltpu.CompilerParams(dimension_semantics=("parallel",), vmem_limit_bytes=VMEM_LIMIT_BYTES),
        interpret=interpret, name="in_proj",
    )(x2, norm_g.reshape(1, D).astype(f32), sc1.reshape(B, 1, D).astype(f32), sh1.reshape(B, 1, D).astype(f32),
      wq, wkv, wg, wu, wm)


def _merge_kernel(attn_ref, y_ref, u_ref, d_ref, mg_ref, x_ref, g1_ref, wglu_ref, wso_ref, wao_ref, wout_ref, o_ref):
    f32, bf16 = jnp.float32, jnp.bfloat16
    D = x_ref.shape[1]
    y = jax.nn.gelu(y_ref[...].astype(f32) + d_ref[...] * u_ref[...])
    ssm = y * jax.nn.sigmoid(jnp.dot(y.astype(bf16), wglu_ref[...], preferred_element_type=f32))
    a = jnp.dot(attn_ref[...], wao_ref[...], preferred_element_type=f32)
    s = jnp.dot(ssm.astype(bf16), wso_ref[...], preferred_element_type=f32)
    mg = mg_ref[...].astype(f32)
    merged = mg[:, :D] * a + mg[:, D:] * s
    out = jnp.dot(merged.astype(bf16), wout_ref[...], preferred_element_type=f32)
    o_ref[...] = x_ref[...] + g1_ref[0] * out


def merge_pallas(attn, y, u, d_skip, mg, x2, g1, w_glu, ssm_w_o, nsa_w_o, w_out, B, L, interpret=False):
    bf16 = jnp.bfloat16
    N, D = x2.shape
    tm = ROW_TILE
    tpb = L // tm
    const = lambda shape: pl.BlockSpec(shape, lambda i: (0,) * len(shape))
    rows = lambda w: pl.BlockSpec((tm, w), lambda i: (i, 0))
    return pl.pallas_call(
        _merge_kernel, grid=(N // tm,),
        in_specs=[rows(ATTN_WIDTH), rows(SSM_WIDTH), rows(SSM_WIDTH), const((1, SSM_WIDTH)), rows(2 * D), rows(D),
                  pl.BlockSpec((1, 1, D), lambda i: (i // tpb, 0, 0)),
                  const(w_glu.shape), const(ssm_w_o.shape), const(nsa_w_o.shape), const(w_out.shape)],
        out_specs=rows(D),
        out_shape=jax.ShapeDtypeStruct((N, D), jnp.float32),
        compiler_params=pltpu.CompilerParams(dimension_semantics=("parallel",), vmem_limit_bytes=VMEM_LIMIT_BYTES),
        interpret=interpret, name="mixer_merge",
    )(attn, y, u, d_skip.reshape(1, SSM_WIDTH).astype(jnp.float32), mg, x2, g1.reshape(B, 1, D).astype(jnp.float32),
      w_glu.astype(bf16), ssm_w_o.astype(bf16), nsa_w_o.astype(bf16), w_out.astype(bf16))


def kernel(x, c, ada_w, ada_b, norm1_g, w_in, ssm_lam_re, ssm_lam_im, ssm_log_dt, ssm_b_re, ssm_b_im, ssm_c_re, ssm_c_im, ssm_d, ssm_w_glu, ssm_w_o, nsa_cmp_pos, nsa_cmp_w1, nsa_cmp_w2, nsa_w_o, w_out, norm2_g, moe_w_group, moe_b_group, moe_w_expert, moe_b_expert, moe_w_gate, moe_w_up, moe_w_down, final_g):
    B, L, D = x.shape
    N = B * L
    x2 = x.reshape(N, D)
    cs = jnp.pad(jax.nn.silu(c), ((0, 8 - B), (0, 0)))
    for l in range(DEPTH):
        mod = pmm(cs, ada_w[l])[:B] + ada_b[l]
        sh1, sc1, g1, sh2, sc2, g2 = jnp.split(mod, 6, axis=-1)
        q, kv, ng, u, mg = in_proj_pallas(x2, norm1_g[l], sc1, sh1, w_in[l], B, L)
        kvs = [kv[:, i * KV_WIDTH:(i + 1) * KV_WIDTH] for i in range(6)]
        attn = nsa_attention_pallas(q, kvs, ng[:, :3 * N_HEADS], nsa_cmp_pos[l], nsa_cmp_w1[l], nsa_cmp_w2[l], B, L)
        y = s5_scan_pallas(u, ssm_lam_re[l], ssm_lam_im[l], ssm_log_dt[l], ssm_b_re[l], ssm_b_im[l],
                           ssm_c_re[l], ssm_c_im[l], B, L)
        x2 = merge_pallas(attn, y, u, ssm_d[l], mg, x2, g1, ssm_w_glu[l], ssm_w_o[l], nsa_w_o[l], w_out[l], B, L)
        x2 = moe_layer_pallas(x2, norm2_g[l], sc2, sh2, g2, moe_w_group[l], moe_b_group[l],
                              moe_w_expert[l], moe_b_expert[l], moe_w_gate[l], moe_w_up[l], moe_w_down[l],
                              final_g, l == DEPTH - 1, B, L)
    return x2.reshape(B, L, D)
```

```python
import functools
import math

import jax
import jax.numpy as jnp
from jax import lax
from jax.experimental import pallas as pl
from jax.experimental.pallas import tpu as pltpu

D_MODEL = 1024
DEPTH = 2
SSM_GROUP = 16
SSM_STATE = 64
SSM_WIDTH = D_MODEL // 2
SSM_GROUPS = SSM_WIDTH // SSM_GROUP
HEAD_DIM = 64
N_HEADS = D_MODEL // HEAD_DIM
N_KV = N_HEADS // 4
HPG = N_HEADS // N_KV
ATTN_WIDTH = N_HEADS * HEAD_DIM
KV_WIDTH = N_KV * HEAD_DIM
CMP_BLOCK = 32
CMP_STRIDE = 16
CMP_HIDDEN = 4 * HEAD_DIM
SLC_BLOCK = 64
SLC_TOPK = 16
WINDOW = 512
Q_CHUNK = 64
FORCED_SCORE = 1e4
N_GROUPS = 4
EXP_PER_GROUP = 8
N_EXPERTS = N_GROUPS * EXP_PER_GROUP
TOPK_IN_GROUP = 2
D_EXPERT = D_MODEL // 4
MOE_BLOCK = 128
EPS = 1e-6
NEG = -1e30

VMEM_LIMIT_BYTES = 48 * 1024 * 1024


def _mm_kernel(a_ref, b_ref, o_ref):
    a = a_ref[...].astype(jnp.bfloat16)
    b = b_ref[...].astype(jnp.bfloat16)
    o_ref[...] = jnp.dot(a, b, preferred_element_type=jnp.float32).astype(o_ref.dtype)


def pmm(a, b, tm=512, tn=512, out_dtype=jnp.float32, interpret=False):
    M, K = a.shape
    _, N = b.shape
    tm = min(tm, M)
    tn = min(tn, N)
    assert M % tm == 0 and N % tn == 0, (M, N, tm, tn)
    return pl.pallas_call(
        _mm_kernel,
        interpret=interpret,
        grid=(M // tm, N // tn),
        in_specs=[pl.BlockSpec((tm, K), lambda i, j: (i, 0)),
                  pl.BlockSpec((K, tn), lambda i, j: (0, j))],
        out_specs=pl.BlockSpec((tm, tn), lambda i, j: (i, j)),
        out_shape=jax.ShapeDtypeStruct((M, N), out_dtype),
        compiler_params=pltpu.CompilerParams(
            dimension_semantics=("parallel", "parallel"),
            vmem_limit_bytes=VMEM_LIMIT_BYTES),
    )(a, b)


MASK_FILL = -1e30
MAX_FLOOR = -1e29
SLC_STEP = 1024
SLC_STEP_BLOCKS = SLC_STEP // SLC_BLOCK
LANES = 128
V_ROWS = HEAD_DIM + 16
NSA_CHUNKS = 4
WIN_SPAN = WINDOW + NSA_CHUNKS * Q_CHUNK
DIAG_SPAN = NSA_CHUNKS * SLC_BLOCK


def _nsa_kernel(qT_ref, kc_ref, vcT_ref, ks_ref, vsT_ref, kw_ref, vwT_ref, g_ref, ovT_ref, wbias_ref, dbias_ref,
                o_ref, bias_ref, st_ref,
                *, n_cmp, n_slc, n_sel):
    f32 = jnp.float32
    bf16 = jnp.bfloat16
    c0 = pl.program_id(2) * NSA_CHUNKS
    c_last = c0 + NSA_CHUNKS - 1
    q = jnp.concatenate([qT_ref[0, 0, j] for j in range(NSA_CHUNKS)], axis=1)
    ncols = q.shape[1]
    chunk_cols = ncols // NSA_CHUNKS
    ncp = kc_ref.shape[2]
    lane = lax.broadcasted_iota(jnp.int32, (1, ncols), 1)
    ccol = c0 + lane // chunk_cols
    tpos = ccol * Q_CHUNK + (lane & (Q_CHUNK - 1))

    s = jnp.dot(kc_ref[0, 0], q, preferred_element_type=f32)
    n_idx = lax.broadcasted_iota(jnp.int32, (ncp, 1), 0)
    vis = (n_idx * CMP_STRIDE + (CMP_BLOCK - 1) <= tpos) & (n_idx < n_cmp)
    s = jnp.where(vis, s, MASK_FILL)
    m = jnp.max(s, axis=0, keepdims=True)
    p = jnp.where(vis, jnp.exp(s - m), 0.0)
    l = jnp.sum(p, axis=0, keepdims=True)
    p = p / jnp.where(l > 0.0, l, 1.0)
    p_hi = p.astype(bf16)
    o_cmp = jnp.dot(vcT_ref[0, 0], p_hi, preferred_element_type=f32)

    p_lo = (p - p_hi.astype(f32)).astype(bf16)
    ov = ovT_ref[...]
    imp4 = jnp.dot(ov, jnp.concatenate([p_hi, p_lo], axis=1), preferred_element_type=f32)
    imp4 = imp4[:, :ncols] + imp4[:, ncols:]
    lane128 = lax.broadcasted_iota(jnp.int32, (1, LANES), 1)
    first_half = lane128 < Q_CHUNK

    def head_sum(j):
        z = imp4[:, j * chunk_cols:j * chunk_cols + LANES] + imp4[:, j * chunk_cols + LANES:(j + 1) * chunk_cols]
        return z + pltpu.roll(z, Q_CHUNK, axis=1)

    imp = jnp.concatenate([jnp.where(first_half, head_sum(2 * p), head_sum(2 * p + 1))
                           for p in range(NSA_CHUNKS // 2)], axis=1)
    tk_lanes = NSA_CHUNKS * Q_CHUNK
    c = c0 + lax.broadcasted_iota(jnp.int32, (1, tk_lanes), 1) // Q_CHUNK

    base = pl.multiple_of(c0 * Q_CHUNK, LANES)
    row16 = lax.broadcasted_iota(jnp.int32, (SLC_STEP_BLOCKS, ncols), 0)
    pad_row = jnp.where(row16 == 0, MASK_FILL, 0.0).astype(bf16)
    zeros_tail = jnp.zeros((LANES - HEAD_DIM - SLC_STEP_BLOCKS, ncols), bf16)
    rhs_w = jnp.concatenate([q, pad_row, zeros_tail], axis=0)
    st = jnp.dot(kw_ref[0, 0, pl.ds(base, WIN_SPAN), :], rhs_w, preferred_element_type=f32) + wbias_ref[...]
    p_w = jnp.exp(st - jnp.max(st, axis=0, keepdims=True))
    pv_w = jnp.dot(vwT_ref[0, 0, :, pl.ds(base, WIN_SPAN)], p_w.astype(bf16), preferred_element_type=f32)
    o_win = pv_w[:HEAD_DIM] / pv_w[HEAD_DIM:HEAD_DIM + 1]

    dbase = pl.multiple_of(c0 * SLC_BLOCK, LANES)
    st = jnp.dot(ks_ref[0, 0, pl.ds(dbase, DIAG_SPAN), :HEAD_DIM], q, preferred_element_type=f32) + dbias_ref[...]
    m_d = jnp.max(st, axis=0, keepdims=True)
    pv_d = jnp.dot(vsT_ref[0, 0, :, pl.ds(dbase, DIAG_SPAN)], jnp.exp(st - m_d).astype(bf16),
                   preferred_element_type=f32)

    jidx = lax.broadcasted_iota(jnp.int32, (n_slc, tk_lanes), 0)
    forced = (jidx == 0) | (jidx == c) | (jidx == c - 1)
    work = jnp.where(forced | (jidx > c), -2.0, imp)
    sel = forced
    for _ in range(n_sel - 3):
        mx = jnp.max(work, axis=0, keepdims=True)
        first = jnp.min(jnp.where(work == mx, jidx, n_slc), axis=0, keepdims=True)
        hit = jidx == first
        sel = sel | hit
        work = jnp.where(hit, -2.0, work)
    early = (jnp.zeros_like(jidx) + c) < n_sel
    sel = (sel | early) & (jidx < c)
    bias = jnp.where(sel, 0.0, MASK_FILL)
    pieces = []
    for p in range(NSA_CHUNKS // 2):
        pair = bias[:, p * LANES:(p + 1) * LANES]
        swapped = pltpu.roll(pair, Q_CHUNK, axis=1)
        even = jnp.where(first_half, pair, swapped)
        odd = jnp.where(first_half, swapped, pair)
        pieces += [even, even, odd, odd]
    bias_ref[...] = jnp.concatenate(pieces, axis=1).astype(bf16)

    def scores(step):
        rows = bias_ref[pl.ds(pl.multiple_of(step * SLC_STEP_BLOCKS, SLC_STEP_BLOCKS), SLC_STEP_BLOCKS), :]
        rhs = jnp.concatenate([q, rows, zeros_tail], axis=0)
        off = pl.multiple_of(step * SLC_STEP, SLC_STEP)
        st = jnp.dot(ks_ref[0, 0, pl.ds(off, SLC_STEP), :], rhs, preferred_element_type=f32)
        st_ref[step & 1] = st
        return jnp.max(st, axis=0, keepdims=True)

    def accumulate(step, mx, m, acc):
        m_new = jnp.maximum(m, mx)
        off = pl.multiple_of(step * SLC_STEP, SLC_STEP)
        pv = jnp.dot(vsT_ref[0, 0, :, pl.ds(off, SLC_STEP)], jnp.exp(st_ref[step & 1] - m_new).astype(bf16),
                     preferred_element_type=f32)
        return m_new, jnp.exp(m - m_new) * acc + pv

    n_steps = jnp.maximum((c_last + SLC_STEP_BLOCKS - 1) // SLC_STEP_BLOCKS, 1)

    def body(i, carry):
        mx, m, acc = carry
        m, acc = accumulate(i, mx, m, acc)
        return scores(i + 1), m, acc

    carry = (scores(0), jnp.full((1, ncols), MAX_FLOOR, f32), jnp.zeros((V_ROWS, ncols), f32))
    mx, m_s, acc = lax.fori_loop(0, n_steps - 1, body, carry)
    m_s, acc = accumulate(n_steps - 1, mx, m_s, acc)
    m_t = jnp.maximum(m_s, m_d)
    tot = jnp.exp(m_s - m_t) * acc + jnp.exp(m_d - m_t) * pv_d
    o_slc = tot[:HEAD_DIM] / tot[HEAD_DIM:HEAD_DIM + 1]

    g = jnp.concatenate([g_ref[0, 0, j] for j in range(NSA_CHUNKS)], axis=1)
    o = g[0:1] * o_cmp + g[1:2] * o_slc + g[2:3] * o_win
    for j in range(NSA_CHUNKS):
        o_ref[0, 0, j] = o[:, j * chunk_cols:(j + 1) * chunk_cols].astype(o_ref.dtype)


def nsa_attention_pallas(q2, kvs, gates2, cmp_pos, cmp_w1, cmp_w2, B, L, interpret=False):
    f32, bf16 = jnp.float32, jnp.bfloat16
    G, H, hd = N_KV, HPG, HEAD_DIM
    C = L // Q_CHUNK
    n16 = L // CMP_STRIDE
    n_cmp = (L - CMP_BLOCK) // CMP_STRIDE + 1
    ncp = -(-n_cmp // LANES) * LANES
    n_slc = L // SLC_BLOCK
    n_sel = min(SLC_TOPK, n_slc)
    k_cmp, v_cmp, k_slc, v_slc, k_win, v_win = kvs

    def compress(kv2, pos, w1, w2):
        x = kv2.reshape(B, n16, CMP_STRIDE, G, hd).transpose(0, 3, 1, 2, 4).reshape(B * G * n16, CMP_STRIDE * hd)
        half = CMP_STRIDE * hd
        wcat = jnp.concatenate([w1[:half], w1[half:]], axis=1)
        ab = pmm(x, wcat, interpret=interpret).reshape(B, G, n16, 2 * CMP_HIDDEN)
        posb = pmm(jnp.pad(pos.reshape(1, -1), ((0, 7), (0, 0))), w1, interpret=interpret)[0]
        hid = ab[:, :, :-1, :CMP_HIDDEN] + ab[:, :, 1:, CMP_HIDDEN:] + posb
        hid = jnp.pad(jax.nn.gelu(hid), ((0, 0), (0, 0), (0, ncp - n_cmp), (0, 0)))
        out = pmm(hid.reshape(B * G * ncp, CMP_HIDDEN), w2, interpret=interpret)
        return out.reshape(B, G, ncp, hd)

    kc = compress(k_cmp, cmp_pos[0], cmp_w1[0], cmp_w2[0]).astype(bf16)
    vcT = compress(v_cmp, cmp_pos[1], cmp_w1[1], cmp_w2[1]).astype(bf16).transpose(0, 1, 3, 2)

    def key_major(a):
        return a.astype(bf16).reshape(B, L, G, hd).transpose(0, 2, 1, 3)

    def dim_major(a):
        return a.astype(bf16).reshape(B, L, G, hd).transpose(0, 2, 3, 1)

    def with_ones_row(vT):
        n = vT.shape[-1]
        extra = jnp.zeros((B, G, V_ROWS - hd, n), bf16).at[:, :, 0, :].set(1.0)
        return jnp.concatenate([vT, extra], axis=2)

    blk_onehot = jax.nn.one_hot((jnp.arange(L) // SLC_BLOCK) % SLC_STEP_BLOCKS, LANES - hd, dtype=bf16)
    ks = jnp.concatenate([key_major(k_slc), jnp.broadcast_to(blk_onehot, (B, G, L, LANES - hd))], axis=-1)
    vsT = with_ones_row(dim_major(v_slc))
    pad_flag = jnp.zeros((L + WINDOW, LANES - hd), bf16).at[:WINDOW, 0].set(1.0)
    kw = jnp.concatenate([jnp.pad(key_major(k_win), ((0, 0), (0, 0), (WINDOW, 0), (0, 0))),
                          jnp.broadcast_to(pad_flag, (B, G, L + WINDOW, LANES - hd))], axis=-1)
    vwT = with_ones_row(jnp.pad(dim_major(v_win), ((0, 0), (0, 0), (0, 0), (WINDOW, 0))))
    qT = q2.astype(bf16).reshape(B, C, Q_CHUNK, G, H, hd)
    qT = qT.transpose(0, 3, 1, 5, 4, 2).reshape(B, G, C, hd, H * Q_CHUNK)
    gT = gates2.astype(f32).reshape(B, C, Q_CHUNK, 3, G, H).transpose(0, 4, 1, 3, 5, 2).reshape(B, G, C, 3, H * Q_CHUNK)
    c_start = jnp.arange(ncp)[None, :] * CMP_STRIDE
    s_start = jnp.arange(n_slc)[:, None] * SLC_BLOCK
    ovT = ((c_start < s_start + SLC_BLOCK) & (c_start + CMP_BLOCK > s_start)
           & (jnp.arange(ncp)[None, :] < n_cmp)).astype(bf16)

    ncols = H * Q_CHUNK
    col = jnp.arange(NSA_CHUNKS * ncols)[None, :]
    tloc = (col // ncols) * Q_CHUNK + col % Q_CHUNK
    dist = WINDOW + tloc - jnp.arange(WIN_SPAN)[:, None]
    wbias = jnp.where((dist >= 0) & (dist < WINDOW), 0.0, MASK_FILL).astype(f32)
    rdiag = jnp.arange(DIAG_SPAN)[:, None]
    dbias = jnp.where((rdiag >= (col // ncols) * SLC_BLOCK) & (rdiag <= tloc), 0.0, MASK_FILL).astype(f32)
    bg = lambda b, g, c: (b, g, 0, 0)
    bgc = lambda b, g, c: (b, g, c, 0, 0)
    oT = pl.pallas_call(
        functools.partial(_nsa_kernel, n_cmp=n_cmp, n_slc=n_slc, n_sel=n_sel),
        grid=(B, G, C // NSA_CHUNKS),
        in_specs=[pl.BlockSpec((1, 1, NSA_CHUNKS, hd, ncols), bgc),
                  pl.BlockSpec((1, 1, ncp, hd), bg),
                  pl.BlockSpec((1, 1, hd, ncp), bg),
                  pl.BlockSpec((1, 1, L, LANES), bg),
                  pl.BlockSpec((1, 1, V_ROWS, L), bg),
                  pl.BlockSpec((1, 1, L + WINDOW, LANES), bg),
                  pl.BlockSpec((1, 1, V_ROWS, L + WINDOW), bg),
                  pl.BlockSpec((1, 1, NSA_CHUNKS, 3, ncols), bgc),
                  pl.BlockSpec((n_slc, ncp), lambda b, g, c: (0, 0)),
                  pl.BlockSpec((WIN_SPAN, NSA_CHUNKS * ncols), lambda b, g, c: (0, 0)),
                  pl.BlockSpec((DIAG_SPAN, NSA_CHUNKS * ncols), lambda b, g, c: (0, 0))],
        out_specs=pl.BlockSpec((1, 1, NSA_CHUNKS, hd, ncols), bgc),
        out_shape=jax.ShapeDtypeStruct((B, G, C, hd, ncols), bf16),
        scratch_shapes=[pltpu.VMEM((n_slc, NSA_CHUNKS * ncols), bf16),
                        pltpu.VMEM((2, SLC_STEP, NSA_CHUNKS * ncols), f32)],
        compiler_params=pltpu.CompilerParams(
            dimension_semantics=("parallel", "parallel", "arbitrary"),
            vmem_limit_bytes=VMEM_LIMIT_BYTES),
        interpret=interpret,
        name="nsa_attention",
    )(qT, kc, vcT, ks, vsT, kw, vwT, gT, ovT, wbias, dbias)
    o = oT.reshape(B, G, C, hd, H, Q_CHUNK).transpose(0, 2, 5, 1, 4, 3)
    return o.reshape(B * L, G * H * hd)


S5_CHUNK = 32


def _s5_state_kernel(x_ref, ws_ref, s_ref):
    s_ref[0] = jnp.dot(x_ref[0], ws_ref[0], preferred_element_type=jnp.float32)


def _s5_scan_kernel(sre_ref, sim_ref, are_ref, aim_ref, hre_ref, him_ref, *, chunks_per_batch):
    a_re = are_ref[...]
    a_im = aim_ref[...]
    n_batch = sre_ref.shape[0] // chunks_per_batch

    def step(c, carry):
        new = []
        for b in range(n_batch):
            h_re, h_im = carry[b]
            row = b * chunks_per_batch + c
            hre_ref[row] = h_re
            him_ref[row] = h_im
            new.append((a_re * h_re - a_im * h_im + sre_ref[row],
                        a_re * h_im + a_im * h_re + sim_ref[row]))
        return tuple(new)

    zero = jnp.zeros(a_re.shape, jnp.float32)
    lax.fori_loop(0, chunks_per_batch, step, tuple((zero, zero) for _ in range(n_batch)))


def _s5_out_kernel(x_ref, hp_ref, m_ref, wo_ref, y_ref):
    y = jnp.dot(x_ref[0], m_ref[0], preferred_element_type=jnp.float32)
    y = y + jnp.dot(hp_ref[0].astype(jnp.bfloat16), wo_ref[0], preferred_element_type=jnp.float32)
    y_ref[0] = y.astype(y_ref.dtype)


def s5_scan_pallas(u2, lam_re, lam_im, log_dt, b_re, b_im, c_re, c_im, B, L, interpret=False):
    f32, bf16 = jnp.float32, jnp.bfloat16
    G, P, H, T = SSM_GROUPS, SSM_STATE, SSM_GROUP, S5_CHUNK
    hi = lax.Precision.HIGHEST
    NC = B * L // T
    dt = jnp.exp(log_dt.astype(f32))[:, None]
    lr, li = lam_re.astype(f32), lam_im.astype(f32)
    mag = jnp.exp(lr * dt)
    ab_re, ab_im = mag * jnp.cos(li * dt), mag * jnp.sin(li * dt)
    den = lr * lr + li * li
    nr = ab_re - 1.0
    cr = (nr * lr + ab_im * li) / den
    cim = (ab_im * lr - nr * li) / den
    br, bim = b_re.astype(f32), b_im.astype(f32)
    bb_re = cr[..., None] * br - cim[..., None] * bim
    bb_im = cr[..., None] * bim + cim[..., None] * br
    k = jnp.arange(T + 1, dtype=f32)[:, None, None]
    pw_mag = jnp.exp(k * (lr * dt))
    pw_re = pw_mag * jnp.cos(k * (li * dt))
    pw_im = pw_mag * jnp.sin(k * (li * dt))
    cre, cimg = c_re.astype(f32), c_im.astype(f32)
    ab_r = pw_re[:T, :, :, None] * bb_re - pw_im[:T, :, :, None] * bb_im
    ab_i = pw_re[:T, :, :, None] * bb_im + pw_im[:T, :, :, None] * bb_re
    kern = (jnp.einsum('gop,tgpi->tgoi', cre, ab_r, precision=hi)
            - jnp.einsum('gop,tgpi->tgoi', cimg, ab_i, precision=hi))
    tt = jnp.arange(T)
    lag = tt[None, :] - tt[:, None]
    m = jnp.where((lag >= 0)[:, :, None, None, None], kern[jnp.clip(lag, 0, T - 1)], 0.0)
    m = m.transpose(2, 0, 4, 1, 3).reshape(G, T * H, T * H).astype(bf16)
    ws_re = ab_r[::-1].transpose(1, 0, 3, 2).reshape(G, T * H, P)
    ws_im = ab_i[::-1].transpose(1, 0, 3, 2).reshape(G, T * H, P)
    ws = jnp.concatenate([ws_re, ws_im], axis=-1).astype(bf16)
    ca_re = cre[None] * pw_re[1:, :, None, :] - cimg[None] * pw_im[1:, :, None, :]
    ca_im = cre[None] * pw_im[1:, :, None, :] + cimg[None] * pw_re[1:, :, None, :]
    wo = jnp.concatenate([ca_re.transpose(1, 3, 0, 2).reshape(G, P, T * H),
                          -ca_im.transpose(1, 3, 0, 2).reshape(G, P, T * H)], axis=1).astype(bf16)

    xg = u2.astype(bf16).reshape(NC, T, G, H).transpose(2, 0, 1, 3).reshape(G, NC, T * H)
    cparams = pltpu.CompilerParams(dimension_semantics=("parallel",), vmem_limit_bytes=VMEM_LIMIT_BYTES)
    s = pl.pallas_call(
        _s5_state_kernel, grid=(G,),
        in_specs=[pl.BlockSpec((1, NC, T * H), lambda g: (g, 0, 0)),
                  pl.BlockSpec((1, T * H, 2 * P), lambda g: (g, 0, 0))],
        out_specs=pl.BlockSpec((1, NC, 2 * P), lambda g: (g, 0, 0)),
        out_shape=jax.ShapeDtypeStruct((G, NC, 2 * P), f32),
        compiler_params=cparams, interpret=interpret, name="s5_chunk_state",
    )(xg, ws)
    sp = s.reshape(G, NC, 2, P).transpose(1, 2, 0, 3).reshape(NC, 2, G // 2, 2 * P)
    a_re = pw_re[T].reshape(G // 2, 2 * P)
    a_im = pw_im[T].reshape(G // 2, 2 * P)
    full = lambda shape: pl.BlockSpec(shape, lambda i: (0,) * len(shape))
    st_shape = (NC, G // 2, 2 * P)
    h_re, h_im = pl.pallas_call(
        functools.partial(_s5_scan_kernel, chunks_per_batch=L // T), grid=(1,),
        in_specs=[full(st_shape), full(st_shape), full((G // 2, 2 * P)), full((G // 2, 2 * P))],
        out_specs=[full(st_shape), full(st_shape)],
        out_shape=[jax.ShapeDtypeStruct(st_shape, f32)] * 2,
        compiler_params=pltpu.CompilerParams(dimension_semantics=("arbitrary",), vmem_limit_bytes=VMEM_LIMIT_BYTES),
        interpret=interpret, name="s5_state_scan",
    )(sp[:, 0], sp[:, 1], a_re, a_im)
    hp = jnp.stack([h_re, h_im], axis=1).reshape(NC, 2, G, P).transpose(2, 0, 1, 3).reshape(G, NC, 2 * P)
    y = pl.pallas_call(
        _s5_out_kernel, grid=(G,),
        in_specs=[pl.BlockSpec((1, NC, T * H), lambda g: (g, 0, 0)),
                  pl.BlockSpec((1, NC, 2 * P), lambda g: (g, 0, 0)),
                  pl.BlockSpec((1, T * H, T * H), lambda g: (g, 0, 0)),
                  pl.BlockSpec((1, 2 * P, T * H), lambda g: (g, 0, 0))],
        out_specs=pl.BlockSpec((1, NC, T * H), lambda g: (g, 0, 0)),
        out_shape=jax.ShapeDtypeStruct((G, NC, T * H), bf16),
        compiler_params=cparams, interpret=interpret, name="s5_chunk_out",
    )(xg, hp, m, wo)
    return y.reshape(G, NC, T, H).transpose(1, 2, 0, 3).reshape(B * L, G * H)


MOE_TILE = 512
MOE_SEG = 32
MOE_EXPERTS_PER_STEP = 8
MOE_SLOTS = TOPK_IN_GROUP * MOE_TILE + N_EXPERTS * MOE_SEG
ROUTE_E1, ROUTE_E2, ROUTE_W1, ROUTE_W2 = 0, 1, 2, 3


def _split_bf16(a):
    hi = a.astype(jnp.bfloat16)
    return hi, (a - hi.astype(jnp.float32)).astype(jnp.bfloat16)


def _moe_route_kernel(x_ref, g_ref, sc_ref, sh_ref, wr_ref, br_ref, h_ref, route_ref, routeT_ref, cnt_ref):
    f32 = jnp.float32
    x = x_ref[...]
    y = x * lax.rsqrt(jnp.mean(x * x, axis=-1, keepdims=True) + EPS)
    h = y * g_ref[...] * (1.0 + sc_ref[0]) + sh_ref[0]
    h_ref[...] = h.astype(h_ref.dtype)
    h_hi, h_lo = _split_bf16(h)
    w_hi, w_lo = _split_bf16(wr_ref[...])
    lg = (jnp.dot(h_hi, w_hi, preferred_element_type=f32) + jnp.dot(h_hi, w_lo, preferred_element_type=f32)
          + jnp.dot(h_lo, w_hi, preferred_element_type=f32) + br_ref[...])
    lane = lax.broadcasted_iota(jnp.int32, lg.shape, 1)
    ninf = -jnp.inf
    gmask = lane < N_GROUPS
    gmax = jnp.max(jnp.where(gmask, lg, ninf), axis=1, keepdims=True)
    gsum = jnp.sum(jnp.where(gmask, jnp.exp(lg - gmax), 0.0), axis=1, keepdims=True)
    g_idx = jnp.min(jnp.where(gmask & (lg == gmax), lane, LANES), axis=1, keepdims=True)
    g_p = 1.0 / gsum
    lo = N_GROUPS + EXP_PER_GROUP * g_idx
    emask = (lane >= lo) & (lane < lo + EXP_PER_GROUP)
    el = jnp.where(emask, lg, ninf)
    emax = jnp.max(el, axis=1, keepdims=True)
    i1 = jnp.min(jnp.where(el == emax, lane, LANES), axis=1, keepdims=True)
    el2 = jnp.where(lane == i1, ninf, el)
    emax2 = jnp.max(el2, axis=1, keepdims=True)
    i2 = jnp.min(jnp.where(el2 == emax2, lane, LANES), axis=1, keepdims=True)
    r = jnp.exp(emax2 - emax)
    w1 = g_p / (1.0 + r)
    w2 = g_p * r / (1.0 + r)
    e1 = (i1 - N_GROUPS).astype(f32)
    e2 = (i2 - N_GROUPS).astype(f32)
    route = jnp.where(lane == ROUTE_E1, e1, jnp.where(lane == ROUTE_E2, e2,
                      jnp.where(lane == ROUTE_W1, w1, jnp.where(lane == ROUTE_W2, w2, 0.0))))
    route_ref[...] = route
    routeT_ref[...] = route.T[:8, :]
    cnt = jnp.sum(jnp.where((lane == i1) | (lane == i2), 1.0, 0.0), axis=0, keepdims=True)
    cnt_ref[0] = jnp.broadcast_to(cnt, (8, LANES))


def _moe_ffn_kernel(seg_start_ref, seg_nblk_ref, h_ref, route_ref, routeT_ref, lrow_ref, lcol_ref, wgu_ref, wd_ref,
                    x_ref, gate_ref, fg_ref, o_ref, xs_ref, yw_ref, ws_ref, *, final_norm):
    f32, bf16 = jnp.float32, jnp.bfloat16
    t = pl.program_id(0)
    step = pl.program_id(1)
    tm = h_ref.shape[0]

    @pl.when(step == 0)
    def _dispatch():
        rT = routeT_ref[...]
        e1r, e2r = rT[ROUTE_E1:ROUTE_E1 + 1], rT[ROUTE_E2:ROUTE_E2 + 1]
        w1r, w2r = rT[ROUTE_W1:ROUTE_W1 + 1], rT[ROUTE_W2:ROUTE_W2 + 1]
        eio = lax.broadcasted_iota(jnp.int32, (N_EXPERTS, tm), 0).astype(f32)
        oh1 = eio == e1r
        oh2 = eio == e2r
        oh = jnp.where(oh1 | oh2, 1.0, 0.0).astype(bf16)
        before = (lax.broadcasted_iota(jnp.int32, (tm, tm), 0)
                  < lax.broadcasted_iota(jnp.int32, (tm, tm), 1))
        rank = jnp.dot(oh, jnp.where(before, 1.0, 0.0).astype(bf16), preferred_element_type=f32)
        pos = lcol_ref[0][:, :1] + rank
        d1 = jnp.sum(jnp.where(oh1, pos, 0.0), axis=0, keepdims=True)
        d2 = jnp.sum(jnp.where(oh2, pos, 0.0), axis=0, keepdims=True)
        slot = lax.broadcasted_iota(jnp.int32, (MOE_SLOTS, tm), 0).astype(f32)
        m1 = slot == d1
        m2 = slot == d2
        perm = jnp.where(m1 | m2, 1.0, 0.0).astype(bf16)
        xs_ref[...] = jnp.dot(perm, h_ref[...], preferred_element_type=f32).astype(bf16)
        ws_ref[...] = jnp.sum(jnp.where(m1, w1r, 0.0) + jnp.where(m2, w2r, 0.0), axis=1, keepdims=True)
        yw_ref[...] = jnp.zeros(yw_ref.shape, yw_ref.dtype)

    rows = 2 * MOE_SEG

    for j in range(MOE_EXPERTS_PER_STEP):
        e = step * MOE_EXPERTS_PER_STEP + j
        start = seg_start_ref[t * N_EXPERTS + e]
        nblk = seg_nblk_ref[t * N_EXPERTS + e]

        def block(i, _, start=start, j=j):
            r0 = pl.multiple_of(start + i * rows, MOE_SEG)
            xb = xs_ref[pl.ds(r0, rows), :]
            gu = jnp.dot(xb, wgu_ref[j], preferred_element_type=f32)
            a = jax.nn.silu(gu[:, :D_EXPERT]) * gu[:, D_EXPERT:]
            y = jnp.dot(a.astype(bf16), wd_ref[j], preferred_element_type=f32)
            yw_ref[pl.ds(r0, rows), :] = (y * ws_ref[pl.ds(r0, rows), :]).astype(bf16)
            return 0

        lax.fori_loop(0, (nblk + 1) // 2, block, 0)

    @pl.when(step == N_EXPERTS // MOE_EXPERTS_PER_STEP - 1)
    def _combine():
        r = route_ref[...]
        e1c, e2c = r[:, ROUTE_E1:ROUTE_E1 + 1], r[:, ROUTE_E2:ROUTE_E2 + 1]
        lane = lax.broadcasted_iota(jnp.int32, (tm, LANES), 1).astype(f32)
        oh1 = lane == e1c
        oh2 = lane == e2c
        oh = jnp.where(oh1 | oh2, 1.0, 0.0).astype(bf16)
        after = (lax.broadcasted_iota(jnp.int32, (tm, tm), 1)
                 < lax.broadcasted_iota(jnp.int32, (tm, tm), 0))
        rank = jnp.dot(jnp.where(after, 1.0, 0.0).astype(bf16), oh, preferred_element_type=f32)
        pos = lrow_ref[0][:1, :] + rank
        d1 = jnp.sum(jnp.where(oh1, pos, 0.0), axis=1, keepdims=True)
        d2 = jnp.sum(jnp.where(oh2, pos, 0.0), axis=1, keepdims=True)
        slot = lax.broadcasted_iota(jnp.int32, (tm, MOE_SLOTS), 1).astype(f32)
        comb = jnp.where((slot == d1) | (slot == d2), 1.0, 0.0).astype(bf16)
        moe = jnp.dot(comb, yw_ref[...], preferred_element_type=f32)
        xn = x_ref[...] + gate_ref[0] * moe
        if final_norm:
            xn = xn * lax.rsqrt(jnp.mean(xn * xn, axis=-1, keepdims=True) + EPS) * fg_ref[...]
        o_ref[...] = xn


def moe_layer_pallas(x2, norm_g, sc2, sh2, g2, w_group, b_group, w_expert, b_expert, w_gate, w_up, w_down,
                     final_g, final_norm, B, L, interpret=False):
    f32, bf16 = jnp.float32, jnp.bfloat16
    N, D = x2.shape
    tm = MOE_TILE
    n_tiles = N // tm
    tpb = L // tm
    wr = jnp.zeros((D, LANES), f32).at[:, :N_GROUPS].set(w_group).at[:, N_GROUPS:N_GROUPS + N_EXPERTS].set(w_expert)
    br = jnp.zeros((1, LANES), f32).at[0, :N_GROUPS].set(b_group).at[0, N_GROUPS:N_GROUPS + N_EXPERTS].set(b_expert)
    row = lambda v: v.reshape(1, D).astype(f32)
    per_batch = lambda v: v.reshape(B, 1, D).astype(f32)
    cp = pltpu.CompilerParams(dimension_semantics=("parallel",), vmem_limit_bytes=VMEM_LIMIT_BYTES)
    h, route, routeT, cnt = pl.pallas_call(
        _moe_route_kernel, grid=(n_tiles,),
        in_specs=[pl.BlockSpec((tm, D), lambda i: (i, 0)),
                  pl.BlockSpec((1, D), lambda i: (0, 0)),
                  pl.BlockSpec((1, 1, D), lambda i: (i // tpb, 0, 0)),
                  pl.BlockSpec((1, 1, D), lambda i: (i // tpb, 0, 0)),
                  pl.BlockSpec((D, LANES), lambda i: (0, 0)),
                  pl.BlockSpec((1, LANES), lambda i: (0, 0))],
        out_specs=[pl.BlockSpec((tm, D), lambda i: (i, 0)),
                   pl.BlockSpec((tm, LANES), lambda i: (i, 0)),
                   pl.BlockSpec((8, tm), lambda i: (0, i)),
                   pl.BlockSpec((1, 8, LANES), lambda i: (i, 0, 0))],
        out_shape=[jax.ShapeDtypeStruct((N, D), bf16), jax.ShapeDtypeStruct((N, LANES), f32),
                   jax.ShapeDtypeStruct((8, N), f32), jax.ShapeDtypeStruct((n_tiles, 8, LANES), f32)],
        compiler_params=cp, interpret=interpret, name="moe_route",
    )(x2, row(norm_g), per_batch(sc2), per_batch(sh2), wr, br)
    counts = cnt[:, 0, N_GROUPS:N_GROUPS + N_EXPERTS].astype(jnp.int32)
    nblk = (counts + MOE_SEG - 1) // MOE_SEG
    starts = (jnp.cumsum(nblk, axis=1) - nblk) * MOE_SEG
    starts_f = starts.astype(f32)
    lrow = jnp.zeros((n_tiles, 8, LANES), f32).at[:, :, :N_EXPERTS].set(starts_f[:, None, :])
    lcol = jnp.broadcast_to(starts_f[:, :, None], (n_tiles, N_EXPERTS, LANES))
    wgu = jnp.concatenate([w_gate, w_up], axis=-1).astype(bf16)
    wd = w_down.astype(bf16)
    tile = lambda t, e, *_: (t, 0)
    out = pl.pallas_call(
        functools.partial(_moe_ffn_kernel, final_norm=final_norm),
        grid_spec=pltpu.PrefetchScalarGridSpec(
            num_scalar_prefetch=2, grid=(n_tiles, N_EXPERTS // MOE_EXPERTS_PER_STEP),
            in_specs=[pl.BlockSpec((tm, D), tile),
                      pl.BlockSpec((tm, LANES), tile),
                      pl.BlockSpec((8, tm), lambda t, e, *_: (0, t)),
                      pl.BlockSpec((1, 8, LANES), lambda t, e, *_: (t, 0, 0)),
                      pl.BlockSpec((1, N_EXPERTS, LANES), lambda t, e, *_: (t, 0, 0)),
                      pl.BlockSpec((MOE_EXPERTS_PER_STEP, D, 2 * D_EXPERT), lambda t, e, *_: (e, 0, 0)),
                      pl.BlockSpec((MOE_EXPERTS_PER_STEP, D_EXPERT, D), lambda t, e, *_: (e, 0, 0)),
                      pl.BlockSpec((tm, D), tile),
                      pl.BlockSpec((1, 1, D), lambda t, e, *_: (t // tpb, 0, 0)),
                      pl.BlockSpec((1, D), lambda t, e, *_: (0, 0))],
            out_specs=pl.BlockSpec((tm, D), tile),
            scratch_shapes=[pltpu.VMEM((MOE_SLOTS, D), bf16), pltpu.VMEM((MOE_SLOTS, D), bf16),
                            pltpu.VMEM((MOE_SLOTS, 1), f32)]),
        out_shape=jax.ShapeDtypeStruct((N, D), f32),
        compiler_params=pltpu.CompilerParams(dimension_semantics=("parallel", "arbitrary"),
                                             vmem_limit_bytes=VMEM_LIMIT_BYTES),
        interpret=interpret, name="moe_ffn",
    )(starts.reshape(-1), nblk.reshape(-1), h, route, routeT, lrow, lcol, wgu, wd, x2, per_batch(g2), row(final_g))
    return out


ROW_TILE = 256
GATE_LANES = 128


def _in_proj_kernel(x_ref, g_ref, sc_ref, sh_ref, wq_ref, wkv_ref, wg_ref, wu_ref, wm_ref,
                    q_ref, kv_ref, ng_ref, u_ref, mg_ref):
    f32 = jnp.float32
    x = x_ref[...]
    y = x * lax.rsqrt(jnp.mean(x * x, axis=-1, keepdims=True) + EPS)
    h = (y * g_ref[...] * (1.0 + sc_ref[0]) + sh_ref[0]).astype(jnp.bfloat16)
    q = jnp.dot(h, wq_ref[...], preferred_element_type=f32)
    q_ref[...] = (q * HEAD_DIM ** -0.5).astype(q_ref.dtype)
    kv_ref[...] = jnp.dot(h, wkv_ref[...], preferred_element_type=f32).astype(kv_ref.dtype)
    ng_ref[...] = jax.nn.sigmoid(jnp.dot(h, wg_ref[...], preferred_element_type=f32))
    u_ref[...] = jnp.dot(h, wu_ref[...], preferred_element_type=f32)
    mg_ref[...] = jax.nn.sigmoid(jnp.dot(h, wm_ref[...], preferred_element_type=f32)).astype(mg_ref.dtype)


def in_proj_pallas(x2, norm_g, sc1, sh1, w_in, B, L, interpret=False):
    f32, bf16 = jnp.float32, jnp.bfloat16
    N, D = x2.shape
    tm = ROW_TILE
    tpb = L // tm
    o1 = ATTN_WIDTH
    o2 = o1 + 6 * KV_WIDTH
    o3 = o2 + 3 * N_HEADS
    o4 = o3 + SSM_WIDTH
    wb = w_in.astype(bf16)
    wq, wkv, wu, wm = wb[:, :o1], wb[:, o1:o2], wb[:, o3:o4], wb[:, o4:]
    wg = jnp.pad(wb[:, o2:o3], ((0, 0), (0, GATE_LANES - 3 * N_HEADS)))
    const = lambda shape: pl.BlockSpec(shape, lambda i: (0,) * len(shape))
    rows = lambda w: pl.BlockSpec((tm, w), lambda i: (i, 0))
    per_batch = pl.BlockSpec((1, 1, D), lambda i: (i // tpb, 0, 0))
    return pl.pallas_call(
        _in_proj_kernel, grid=(N // tm,),
        in_specs=[rows(D), const((1, D)), per_batch, per_batch,
                  const(wq.shape), const(wkv.shape), const(wg.shape), const(wu.shape), const(wm.shape)],
        out_specs=[rows(o1), rows(o2 - o1), rows(GATE_LANES), rows(SSM_WIDTH), rows(2 * D)],
        out_shape=[jax.ShapeDtypeStruct((N, o1), bf16), jax.ShapeDtypeStruct((N, o2 - o1), bf16),
                   jax.ShapeDtypeStruct((N, GATE_LANES), f32), jax.ShapeDtypeStruct((N, SSM_WIDTH), f32),
                   jax.ShapeDtypeStruct((N, 2 * D), bf16)],
        compiler_params=pltpu.CompilerParams(dimension_semantics=("parallel",), vmem_limit_bytes=VMEM_LIMIT_BYTES),
        interpret=interpret, name="in_proj",
    )(x2, norm_g.reshape(1, D).astype(f32), sc1.reshape(B, 1, D).astype(f32), sh1.reshape(B, 1, D).astype(f32),
      wq, wkv, wg, wu, wm)


def _merge_kernel(attn_ref, y_ref, u_ref, d_ref, mg_ref, x_ref, g1_ref, wglu_ref, wso_ref, wao_ref, wout_ref, o_ref):
    f32, bf16 = jnp.float32, jnp.bfloat16
    D = x_ref.shape[1]
    y = jax.nn.gelu(y_ref[...].astype(f32) + d_ref[...] * u_ref[...])
    ssm = y * jax.nn.sigmoid(jnp.dot(y.astype(bf16), wglu_ref[...], preferred_element_type=f32))
    a = jnp.dot(attn_ref[...], wao_ref[...], preferred_element_type=f32)
    s = jnp.dot(ssm.astype(bf16), wso_ref[...], preferred_element_type=f32)
    mg = mg_ref[...].astype(f32)
    merged = mg[:, :D] * a + mg[:, D:] * s
    out = jnp.dot(merged.astype(bf16), wout_ref[...], preferred_element_type=f32)
    o_ref[...] = x_ref[...] + g1_ref[0] * out


def merge_pallas(attn, y, u, d_skip, mg, x2, g1, w_glu, ssm_w_o, nsa_w_o, w_out, B, L, interpret=False):
    bf16 = jnp.bfloat16
    N, D = x2.shape
    tm = ROW_TILE
    tpb = L // tm
    const = lambda shape: pl.BlockSpec(shape, lambda i: (0,) * len(shape))
    rows = lambda w: pl.BlockSpec((tm, w), lambda i: (i, 0))
    return pl.pallas_call(
        _merge_kernel, grid=(N // tm,),
        in_specs=[rows(ATTN_WIDTH), rows(SSM_WIDTH), rows(SSM_WIDTH), const((1, SSM_WIDTH)), rows(2 * D), rows(D),
                  pl.BlockSpec((1, 1, D), lambda i: (i // tpb, 0, 0)),
                  const(w_glu.shape), const(ssm_w_o.shape), const(nsa_w_o.shape), const(w_out.shape)],
        out_specs=rows(D),
        out_shape=jax.ShapeDtypeStruct((N, D), jnp.float32),
        compiler_params=pltpu.CompilerParams(dimension_semantics=("parallel",), vmem_limit_bytes=VMEM_LIMIT_BYTES),
        interpret=interpret, name="mixer_merge",
    )(attn, y, u, d_skip.reshape(1, SSM_WIDTH).astype(jnp.float32), mg, x2, g1.reshape(B, 1, D).astype(jnp.float32),
      w_glu.astype(bf16), ssm_w_o.astype(bf16), nsa_w_o.astype(bf16), w_out.astype(bf16))


def kernel(x, c, ada_w, ada_b, norm1_g, w_in, ssm_lam_re, ssm_lam_im, ssm_log_dt, ssm_b_re, ssm_b_im, ssm_c_re, ssm_c_im, ssm_d, ssm_w_glu, ssm_w_o, nsa_cmp_pos, nsa_cmp_w1, nsa_cmp_w2, nsa_w_o, w_out, norm2_g, moe_w_group, moe_b_group, moe_w_expert, moe_b_expert, moe_w_gate, moe_w_up, moe_w_down, final_g):
    B, L, D = x.shape
    N = B * L
    x2 = x.reshape(N, D)
    cs = jnp.pad(jax.nn.silu(c), ((0, 8 - B), (0, 0)))
    for l in range(DEPTH):
        mod = pmm(cs, ada_w[l])[:B] + ada_b[l]
        sh1, sc1, g1, sh2, sc2, g2 = jnp.split(mod, 6, axis=-1)
        q, kv, ng, u, mg = in_proj_pallas(x2, norm1_g[l], sc1, sh1, w_in[l], B, L)
        kvs = [kv[:, i * KV_WIDTH:(i + 1) * KV_WIDTH] for i in range(6)]
        attn = nsa_attention_pallas(q, kvs, ng[:, :3 * N_HEADS], nsa_cmp_pos[l], nsa_cmp_w1[l], nsa_cmp_w2[l], B, L)
        y = s5_scan_pallas(u, ssm_lam_re[l], ssm_lam_im[l], ssm_log_dt[l], ssm_b_re[l], ssm_b_im[l],
                           ssm_c_re[l], ssm_c_im[l], B, L)
        x2 = merge_pallas(attn, y, u, ssm_d[l], mg, x2, g1, ssm_w_glu[l], ssm_w_o[l], nsa_w_o[l], w_out[l], B, L)
        x2 = moe_layer_pallas(x2, norm2_g[l], sc2, sh2, g2, moe_w_group[l], moe_b_group[l],
                              moe_w_expert[l], moe_b_expert[l], moe_w_gate[l], moe_w_up[l], moe_w_down[l],
                              final_g, l == DEPTH - 1, B, L)
    return x2.reshape(B, L, D)
```

```python
import functools

import jax
import jax.numpy as jnp
from jax import lax
from jax.experimental import pallas as pl
from jax.experimental.pallas import tpu as pltpu

D_MODEL = 1024
DEPTH = 2
SSM_GROUP = 16
SSM_STATE = 64
SSM_WIDTH = D_MODEL // 2
SSM_GROUPS = SSM_WIDTH // SSM_GROUP
HEAD_DIM = 64
N_HEADS = D_MODEL // HEAD_DIM
N_KV = N_HEADS // 4
HPG = N_HEADS // N_KV
ATTN_WIDTH = N_HEADS * HEAD_DIM
KV_WIDTH = N_KV * HEAD_DIM
CMP_BLOCK = 32
CMP_STRIDE = 16
CMP_HIDDEN = 4 * HEAD_DIM
SLC_BLOCK = 64
SLC_TOPK = 16
WINDOW = 512
Q_CHUNK = 64
N_GROUPS = 4
EXP_PER_GROUP = 8
N_EXPERTS = N_GROUPS * EXP_PER_GROUP
TOPK_IN_GROUP = 2
D_EXPERT = D_MODEL // 4
EPS = 1e-6

VMEM_LIMIT_BYTES = 48 * 1024 * 1024


def _mm_kernel(a_ref, b_ref, o_ref):
    a = a_ref[...].astype(jnp.bfloat16)
    b = b_ref[...].astype(jnp.bfloat16)
    o_ref[...] = jnp.dot(a, b, preferred_element_type=jnp.float32).astype(o_ref.dtype)


def pmm(a, b, tm=512, tn=512, out_dtype=jnp.float32, interpret=False):
    M, K = a.shape
    _, N = b.shape
    tm = min(tm, M)
    tn = min(tn, N)
    assert M % tm == 0 and N % tn == 0, (M, N, tm, tn)
    return pl.pallas_call(
        _mm_kernel,
        interpret=interpret,
        grid=(M // tm, N // tn),
        in_specs=[pl.BlockSpec((tm, K), lambda i, j: (i, 0)),
                  pl.BlockSpec((K, tn), lambda i, j: (0, j))],
        out_specs=pl.BlockSpec((tm, tn), lambda i, j: (i, j)),
        out_shape=jax.ShapeDtypeStruct((M, N), out_dtype),
        compiler_params=pltpu.CompilerParams(
            dimension_semantics=("parallel", "parallel"),
            vmem_limit_bytes=VMEM_LIMIT_BYTES),
    )(a, b)


MASK_FILL = -1e30
MAX_FLOOR = -1e29
SLC_STEP = 1024
SLC_STEP_BLOCKS = SLC_STEP // SLC_BLOCK
LANES = 128
V_ROWS = HEAD_DIM + 16
NSA_CHUNKS = 4
WIN_SPAN = WINDOW + NSA_CHUNKS * Q_CHUNK
DIAG_SPAN = NSA_CHUNKS * SLC_BLOCK


def _nsa_kernel(qT_ref, kc_ref, vcT_ref, ks_ref, vsT_ref, kw_ref, vwT_ref, g_ref, ovT_ref, wbias_ref, dbias_ref,
                o_ref, bias_ref, st_ref,
                *, n_cmp, n_slc, n_sel):
    f32 = jnp.float32
    bf16 = jnp.bfloat16
    c0 = pl.program_id(2) * NSA_CHUNKS
    c_last = c0 + NSA_CHUNKS - 1
    q = jnp.concatenate([qT_ref[0, 0, j] for j in range(NSA_CHUNKS)], axis=1)
    ncols = q.shape[1]
    chunk_cols = ncols // NSA_CHUNKS
    ncp = kc_ref.shape[2]
    lane = lax.broadcasted_iota(jnp.int32, (1, ncols), 1)
    ccol = c0 + lane // chunk_cols
    tpos = ccol * Q_CHUNK + (lane & (Q_CHUNK - 1))

    s = jnp.dot(kc_ref[0, 0], q, preferred_element_type=f32)
    n_idx = lax.broadcasted_iota(jnp.int32, (ncp, 1), 0)
    vis = (n_idx * CMP_STRIDE + (CMP_BLOCK - 1) <= tpos) & (n_idx < n_cmp)
    s = jnp.where(vis, s, MASK_FILL)
    m = jnp.max(s, axis=0, keepdims=True)
    p = jnp.where(vis, jnp.exp(s - m), 0.0)
    l = jnp.sum(p, axis=0, keepdims=True)
    p = p / jnp.where(l > 0.0, l, 1.0)
    p_hi = p.astype(bf16)
    o_cmp = jnp.dot(vcT_ref[0, 0], p_hi, preferred_element_type=f32)

    p_lo = (p - p_hi.astype(f32)).astype(bf16)
    ov = ovT_ref[...]
    imp4 = jnp.dot(ov, jnp.concatenate([p_hi, p_lo], axis=1), preferred_element_type=f32)
    imp4 = imp4[:, :ncols] + imp4[:, ncols:]
    lane128 = lax.broadcasted_iota(jnp.int32, (1, LANES), 1)
    first_half = lane128 < Q_CHUNK

    def head_sum(j):
        z = imp4[:, j * chunk_cols:j * chunk_cols + LANES] + imp4[:, j * chunk_cols + LANES:(j + 1) * chunk_cols]
        return z + pltpu.roll(z, Q_CHUNK, axis=1)

    imp = jnp.concatenate([jnp.where(first_half, head_sum(2 * p), head_sum(2 * p + 1))
                           for p in range(NSA_CHUNKS // 2)], axis=1)
    tk_lanes = NSA_CHUNKS * Q_CHUNK
    c = c0 + lax.broadcasted_iota(jnp.int32, (1, tk_lanes), 1) // Q_CHUNK

    base = pl.multiple_of(c0 * Q_CHUNK, LANES)
    row16 = lax.broadcasted_iota(jnp.int32, (SLC_STEP_BLOCKS, ncols), 0)
    pad_row = jnp.where(row16 == 0, MASK_FILL, 0.0).astype(bf16)
    zeros_tail = jnp.zeros((LANES - HEAD_DIM - SLC_STEP_BLOCKS, ncols), bf16)
    rhs_w = jnp.concatenate([q, pad_row, zeros_tail], axis=0)
    st = jnp.dot(kw_ref[0, 0, pl.ds(base, WIN_SPAN), :], rhs_w, preferred_element_type=f32) + wbias_ref[...]
    p_w = jnp.exp(st - jnp.max(st, axis=0, keepdims=True))
    pv_w = jnp.dot(vwT_ref[0, 0, :, pl.ds(base, WIN_SPAN)], p_w.astype(bf16), preferred_element_type=f32)
    o_win = pv_w[:HEAD_DIM] / pv_w[HEAD_DIM:HEAD_DIM + 1]

    dbase = pl.multiple_of(c0 * SLC_BLOCK, LANES)
    st = jnp.dot(ks_ref[0, 0, pl.ds(dbase, DIAG_SPAN), :HEAD_DIM], q, preferred_element_type=f32) + dbias_ref[...]
    m_d = jnp.max(st, axis=0, keepdims=True)
    pv_d = jnp.dot(vsT_ref[0, 0, :, pl.ds(dbase, DIAG_SPAN)], jnp.exp(st - m_d).astype(bf16),
                   preferred_element_type=f32)

    jidx = lax.broadcasted_iota(jnp.int32, (n_slc, tk_lanes), 0)
    forced = (jidx == 0) | (jidx == c) | (jidx == c - 1)
    work = jnp.where(forced | (jidx > c), -2.0, imp)
    sel = forced
    for _ in range(n_sel - 3):
        mx = jnp.max(work, axis=0, keepdims=True)
        first = jnp.min(jnp.where(work == mx, jidx, n_slc), axis=0, keepdims=True)
        hit = jidx == first
        sel = sel | hit
        work = jnp.where(hit, -2.0, work)
    early = (jnp.zeros_like(jidx) + c) < n_sel
    sel = (sel | early) & (jidx < c)
    bias = jnp.where(sel, 0.0, MASK_FILL)
    pieces = []
    for p in range(NSA_CHUNKS // 2):
        pair = bias[:, p * LANES:(p + 1) * LANES]
        swapped = pltpu.roll(pair, Q_CHUNK, axis=1)
        even = jnp.where(first_half, pair, swapped)
        odd = jnp.where(first_half, swapped, pair)
        pieces += [even, even, odd, odd]
    bias_ref[...] = jnp.concatenate(pieces, axis=1).astype(bf16)

    def scores(step):
        rows = bias_ref[pl.ds(pl.multiple_of(step * SLC_STEP_BLOCKS, SLC_STEP_BLOCKS), SLC_STEP_BLOCKS), :]
        rhs = jnp.concatenate([q, rows, zeros_tail], axis=0)
        off = pl.multiple_of(step * SLC_STEP, SLC_STEP)
        st = jnp.dot(ks_ref[0, 0, pl.ds(off, SLC_STEP), :], rhs, preferred_element_type=f32)
        st_ref[step & 1] = st
        return jnp.max(st, axis=0, keepdims=True)

    def accumulate(step, mx, m, acc):
        m_new = jnp.maximum(m, mx)
        off = pl.multiple_of(step * SLC_STEP, SLC_STEP)
        pv = jnp.dot(vsT_ref[0, 0, :, pl.ds(off, SLC_STEP)], jnp.exp(st_ref[step & 1] - m_new).astype(bf16),
                     preferred_element_type=f32)
        return m_new, jnp.exp(m - m_new) * acc + pv

    n_steps = jnp.maximum((c_last + SLC_STEP_BLOCKS - 1) // SLC_STEP_BLOCKS, 1)

    def body(i, carry):
        mx, m, acc = carry
        m, acc = accumulate(i, mx, m, acc)
        return scores(i + 1), m, acc

    carry = (scores(0), jnp.full((1, ncols), MAX_FLOOR, f32), jnp.zeros((V_ROWS, ncols), f32))
    mx, m_s, acc = lax.fori_loop(0, n_steps - 1, body, carry)
    m_s, acc = accumulate(n_steps - 1, mx, m_s, acc)
    m_t = jnp.maximum(m_s, m_d)
    tot = jnp.exp(m_s - m_t) * acc + jnp.exp(m_d - m_t) * pv_d
    o_slc = tot[:HEAD_DIM] / tot[HEAD_DIM:HEAD_DIM + 1]

    g = jnp.concatenate([g_ref[0, 0, j] for j in range(NSA_CHUNKS)], axis=1)
    o = g[0:1] * o_cmp + g[1:2] * o_slc + g[2:3] * o_win
    for j in range(NSA_CHUNKS):
        o_ref[0, 0, j] = o[:, j * chunk_cols:(j + 1) * chunk_cols].astype(o_ref.dtype)


def nsa_attention_pallas(q2, kvs, gates2, cmp_pos, cmp_w1, cmp_w2, B, L, interpret=False):
    f32, bf16 = jnp.float32, jnp.bfloat16
    G, H, hd = N_KV, HPG, HEAD_DIM
    C = L // Q_CHUNK
    n16 = L // CMP_STRIDE
    n_cmp = (L - CMP_BLOCK) // CMP_STRIDE + 1
    ncp = -(-n_cmp // LANES) * LANES
    n_slc = L // SLC_BLOCK
    n_sel = min(SLC_TOPK, n_slc)
    k_cmp, v_cmp, k_slc, v_slc, k_win, v_win = kvs

    def compress(kv2, pos, w1, w2):
        x = kv2.reshape(B, n16, CMP_STRIDE, G, hd).transpose(0, 3, 1, 2, 4).reshape(B * G * n16, CMP_STRIDE * hd)
        half = CMP_STRIDE * hd
        wcat = jnp.concatenate([w1[:half], w1[half:]], axis=1)
        ab = pmm(x, wcat, interpret=interpret).reshape(B, G, n16, 2 * CMP_HIDDEN)
        posb = pmm(jnp.pad(pos.reshape(1, -1), ((0, 7), (0, 0))), w1, interpret=interpret)[0]
        hid = ab[:, :, :-1, :CMP_HIDDEN] + ab[:, :, 1:, CMP_HIDDEN:] + posb
        hid = jnp.pad(jax.nn.gelu(hid), ((0, 0), (0, 0), (0, ncp - n_cmp), (0, 0)))
        out = pmm(hid.reshape(B * G * ncp, CMP_HIDDEN), w2, interpret=interpret)
        return out.reshape(B, G, ncp, hd)

    kc = compress(k_cmp, cmp_pos[0], cmp_w1[0], cmp_w2[0]).astype(bf16)
    vcT = compress(v_cmp, cmp_pos[1], cmp_w1[1], cmp_w2[1]).astype(bf16).transpose(0, 1, 3, 2)

    def key_major(a):
        return a.astype(bf16).reshape(B, L, G, hd).transpose(0, 2, 1, 3)

    def dim_major(a):
        return a.astype(bf16).reshape(B, L, G, hd).transpose(0, 2, 3, 1)

    def with_ones_row(vT):
        n = vT.shape[-1]
        extra = jnp.zeros((B, G, V_ROWS - hd, n), bf16).at[:, :, 0, :].set(1.0)
        return jnp.concatenate([vT, extra], axis=2)

    blk_onehot = jax.nn.one_hot((jnp.arange(L) // SLC_BLOCK) % SLC_STEP_BLOCKS, LANES - hd, dtype=bf16)
    ks = jnp.concatenate([key_major(k_slc), jnp.broadcast_to(blk_onehot, (B, G, L, LANES - hd))], axis=-1)
    vsT = with_ones_row(dim_major(v_slc))
    pad_flag = jnp.zeros((L + WINDOW, LANES - hd), bf16).at[:WINDOW, 0].set(1.0)
    kw = jnp.concatenate([jnp.pad(key_major(k_win), ((0, 0), (0, 0), (WINDOW, 0), (0, 0))),
                          jnp.broadcast_to(pad_flag, (B, G, L + WINDOW, LANES - hd))], axis=-1)
    vwT = with_ones_row(jnp.pad(dim_major(v_win), ((0, 0), (0, 0), (0, 0), (WINDOW, 0))))
    qT = q2.astype(bf16).reshape(B, C, Q_CHUNK, G, H, hd)
    qT = qT.transpose(0, 3, 1, 5, 4, 2).reshape(B, G, C, hd, H * Q_CHUNK)
    gT = gates2.astype(f32).reshape(B, C, Q_CHUNK, 3, G, H).transpose(0, 4, 1, 3, 5, 2).reshape(B, G, C, 3, H * Q_CHUNK)
    c_start = jnp.arange(ncp)[None, :] * CMP_STRIDE
    s_start = jnp.arange(n_slc)[:, None] * SLC_BLOCK
    ovT = ((c_start < s_start + SLC_BLOCK) & (c_start + CMP_BLOCK > s_start)
           & (jnp.arange(ncp)[None, :] < n_cmp)).astype(bf16)

    ncols = H * Q_CHUNK
    col = jnp.arange(NSA_CHUNKS * ncols)[None, :]
    tloc = (col // ncols) * Q_CHUNK + col % Q_CHUNK
    dist = WINDOW + tloc - jnp.arange(WIN_SPAN)[:, None]
    wbias = jnp.where((dist >= 0) & (dist < WINDOW), 0.0, MASK_FILL).astype(f32)
    rdiag = jnp.arange(DIAG_SPAN)[:, None]
    dbias = jnp.where((rdiag >= (col // ncols) * SLC_BLOCK) & (rdiag <= tloc), 0.0, MASK_FILL).astype(f32)
    bg = lambda b, g, c: (b, g, 0, 0)
    bgc = lambda b, g, c: (b, g, c, 0, 0)
    oT = pl.pallas_call(
        functools.partial(_nsa_kernel, n_cmp=n_cmp, n_slc=n_slc, n_sel=n_sel),
        grid=(B, G, C // NSA_CHUNKS),
        in_specs=[pl.BlockSpec((1, 1, NSA_CHUNKS, hd, ncols), bgc),
                  pl.BlockSpec((1, 1, ncp, hd), bg),
                  pl.BlockSpec((1, 1, hd, ncp), bg),
                  pl.BlockSpec((1, 1, L, LANES), bg),
                  pl.BlockSpec((1, 1, V_ROWS, L), bg),
                  pl.BlockSpec((1, 1, L + WINDOW, LANES), bg),
                  pl.BlockSpec((1, 1, V_ROWS, L + WINDOW), bg),
                  pl.BlockSpec((1, 1, NSA_CHUNKS, 3, ncols), bgc),
                  pl.BlockSpec((n_slc, ncp), lambda b, g, c: (0, 0)),
                  pl.BlockSpec((WIN_SPAN, NSA_CHUNKS * ncols), lambda b, g, c: (0, 0)),
                  pl.BlockSpec((DIAG_SPAN, NSA_CHUNKS * ncols), lambda b, g, c: (0, 0))],
        out_specs=pl.BlockSpec((1, 1, NSA_CHUNKS, hd, ncols), bgc),
        out_shape=jax.ShapeDtypeStruct((B, G, C, hd, ncols), bf16),
        scratch_shapes=[pltpu.VMEM((n_slc, NSA_CHUNKS * ncols), bf16),
                        pltpu.VMEM((2, SLC_STEP, NSA_CHUNKS * ncols), f32)],
        compiler_params=pltpu.CompilerParams(
            dimension_semantics=("parallel", "parallel", "arbitrary"),
            vmem_limit_bytes=VMEM_LIMIT_BYTES),
        interpret=interpret,
        name="nsa_attention",
    )(qT, kc, vcT, ks, vsT, kw, vwT, gT, ovT, wbias, dbias)
    o = oT.reshape(B, G, C, hd, H, Q_CHUNK).transpose(0, 2, 5, 1, 4, 3)
    return o.reshape(B * L, G * H * hd)


S5_CHUNK = 32


def _s5_state_kernel(x_ref, ws_ref, s_ref):
    s_ref[0] = jnp.dot(x_ref[0], ws_ref[0], preferred_element_type=jnp.float32)


def _s5_scan_kernel(sre_ref, sim_ref, are_ref, aim_ref, hre_ref, him_ref, *, chunks_per_batch):
    a_re = are_ref[...]
    a_im = aim_ref[...]
    n_batch = sre_ref.shape[0] // chunks_per_batch

    def step(c, carry):
        new = []
        for b in range(n_batch):
            h_re, h_im = carry[b]
            row = b * chunks_per_batch + c
            hre_ref[row] = h_re
            him_ref[row] = h_im
            new.append((a_re * h_re - a_im * h_im + sre_ref[row],
                        a_re * h_im + a_im * h_re + sim_ref[row]))
        return tuple(new)

    zero = jnp.zeros(a_re.shape, jnp.float32)
    lax.fori_loop(0, chunks_per_batch, step, tuple((zero, zero) for _ in range(n_batch)))


def _s5_out_kernel(x_ref, hp_ref, m_ref, wo_ref, y_ref):
    y = jnp.dot(x_ref[0], m_ref[0], preferred_element_type=jnp.float32)
    y = y + jnp.dot(hp_ref[0].astype(jnp.bfloat16), wo_ref[0], preferred_element_type=jnp.float32)
    y_ref[0] = y.astype(y_ref.dtype)


def s5_scan_pallas(u2, lam_re, lam_im, log_dt, b_re, b_im, c_re, c_im, B, L, interpret=False):
    f32, bf16 = jnp.float32, jnp.bfloat16
    G, P, H, T = SSM_GROUPS, SSM_STATE, SSM_GROUP, S5_CHUNK
    hi = lax.Precision.HIGHEST
    NC = B * L // T
    dt = jnp.exp(log_dt.astype(f32))[:, None]
    lr, li = lam_re.astype(f32), lam_im.astype(f32)
    mag = jnp.exp(lr * dt)
    ab_re, ab_im = mag * jnp.cos(li * dt), mag * jnp.sin(li * dt)
    den = lr * lr + li * li
    nr = ab_re - 1.0
    cr = (nr * lr + ab_im * li) / den
    cim = (ab_im * lr - nr * li) / den
    br, bim = b_re.astype(f32), b_im.astype(f32)
    bb_re = cr[..., None] * br - cim[..., None] * bim
    bb_im = cr[..., None] * bim + cim[..., None] * br
    k = jnp.arange(T + 1, dtype=f32)[:, None, None]
    pw_mag = jnp.exp(k * (lr * dt))
    pw_re = pw_mag * jnp.cos(k * (li * dt))
    pw_im = pw_mag * jnp.sin(k * (li * dt))
    cre, cimg = c_re.astype(f32), c_im.astype(f32)
    ab_r = pw_re[:T, :, :, None] * bb_re - pw_im[:T, :, :, None] * bb_im
    ab_i = pw_re[:T, :, :, None] * bb_im + pw_im[:T, :, :, None] * bb_re
    kern = (jnp.einsum('gop,tgpi->tgoi', cre, ab_r, precision=hi)
            - jnp.einsum('gop,tgpi->tgoi', cimg, ab_i, precision=hi))
    tt = jnp.arange(T)
    lag = tt[None, :] - tt[:, None]
    m = jnp.where((lag >= 0)[:, :, None, None, None], kern[jnp.clip(lag, 0, T - 1)], 0.0)
    m = m.transpose(2, 0, 4, 1, 3).reshape(G, T * H, T * H).astype(bf16)
    ws_re = ab_r[::-1].transpose(1, 0, 3, 2).reshape(G, T * H, P)
    ws_im = ab_i[::-1].transpose(1, 0, 3, 2).reshape(G, T * H, P)
    ws = jnp.concatenate([ws_re, ws_im], axis=-1).astype(bf16)
    ca_re = cre[None] * pw_re[1:, :, None, :] - cimg[None] * pw_im[1:, :, None, :]
    ca_im = cre[None] * pw_im[1:, :, None, :] + cimg[None] * pw_re[1:, :, None, :]
    wo = jnp.concatenate([ca_re.transpose(1, 3, 0, 2).reshape(G, P, T * H),
                          -ca_im.transpose(1, 3, 0, 2).reshape(G, P, T * H)], axis=1).astype(bf16)

    xg = u2.astype(bf16).reshape(NC, T, G, H).transpose(2, 0, 1, 3).reshape(G, NC, T * H)
    cparams = pltpu.CompilerParams(dimension_semantics=("parallel",), vmem_limit_bytes=VMEM_LIMIT_BYTES)
    s = pl.pallas_call(
        _s5_state_kernel, grid=(G,),
        in_specs=[pl.BlockSpec((1, NC, T * H), lambda g: (g, 0, 0)),
                  pl.BlockSpec((1, T * H, 2 * P), lambda g: (g, 0, 0))],
        out_specs=pl.BlockSpec((1, NC, 2 * P), lambda g: (g, 0, 0)),
        out_shape=jax.ShapeDtypeStruct((G, NC, 2 * P), f32),
        compiler_params=cparams, interpret=interpret, name="s5_chunk_state",
    )(xg, ws)
    sp = s.reshape(G, NC, 2, P).transpose(1, 2, 0, 3).reshape(NC, 2, G // 2, 2 * P)
    a_re = pw_re[T].reshape(G // 2, 2 * P)
    a_im = pw_im[T].reshape(G // 2, 2 * P)
    full = lambda shape: pl.BlockSpec(shape, lambda i: (0,) * len(shape))
    st_shape = (NC, G // 2, 2 * P)
    h_re, h_im = pl.pallas_call(
        functools.partial(_s5_scan_kernel, chunks_per_batch=L // T), grid=(1,),
        in_specs=[full(st_shape), full(st_shape), full((G // 2, 2 * P)), full((G // 2, 2 * P))],
        out_specs=[full(st_shape), full(st_shape)],
        out_shape=[jax.ShapeDtypeStruct(st_shape, f32)] * 2,
        compiler_params=pltpu.CompilerParams(dimension_semantics=("arbitrary",), vmem_limit_bytes=VMEM_LIMIT_BYTES),
        interpret=interpret, name="s5_state_scan",
    )(sp[:, 0], sp[:, 1], a_re, a_im)
    hp = jnp.stack([h_re, h_im], axis=1).reshape(NC, 2, G, P).transpose(2, 0, 1, 3).reshape(G, NC, 2 * P)
    y = pl.pallas_call(
        _s5_out_kernel, grid=(G,),
        in_specs=[pl.BlockSpec((1, NC, T * H), lambda g: (g, 0, 0)),
                  pl.BlockSpec((1, NC, 2 * P), lambda g: (g, 0, 0)),
                  pl.BlockSpec((1, T * H, T * H), lambda g: (g, 0, 0)),
                  pl.BlockSpec((1, 2 * P, T * H), lambda g: (g, 0, 0))],
        out_specs=pl.BlockSpec((1, NC, T * H), lambda g: (g, 0, 0)),
        out_shape=jax.ShapeDtypeStruct((G, NC, T * H), bf16),
        compiler_params=cparams, interpret=interpret, name="s5_chunk_out",
    )(xg, hp, m, wo)
    return y.reshape(G, NC, T, H).transpose(1, 2, 0, 3).reshape(B * L, G * H)


MOE_TILE = 512
MOE_SEG = 32
MOE_EXPERTS_PER_STEP = 8
MOE_SLOTS = TOPK_IN_GROUP * MOE_TILE + N_EXPERTS * MOE_SEG
ROUTE_E1, ROUTE_E2, ROUTE_W1, ROUTE_W2 = 0, 1, 2, 3


def _split_bf16(a):
    hi = a.astype(jnp.bfloat16)
    return hi, (a - hi.astype(jnp.float32)).astype(jnp.bfloat16)


def _moe_route_kernel(x_ref, g_ref, sc_ref, sh_ref, wr_ref, br_ref, h_ref, route_ref, routeT_ref, cnt_ref):
    f32 = jnp.float32
    x = x_ref[...]
    y = x * lax.rsqrt(jnp.mean(x * x, axis=-1, keepdims=True) + EPS)
    h = y * g_ref[...] * (1.0 + sc_ref[0]) + sh_ref[0]
    h_ref[...] = h.astype(h_ref.dtype)
    h_hi, h_lo = _split_bf16(h)
    w_hi, w_lo = _split_bf16(wr_ref[...])
    lg = (jnp.dot(h_hi, w_hi, preferred_element_type=f32) + jnp.dot(h_hi, w_lo, preferred_element_type=f32)
          + jnp.dot(h_lo, w_hi, preferred_element_type=f32) + br_ref[...])
    lane = lax.broadcasted_iota(jnp.int32, lg.shape, 1)
    ninf = -jnp.inf
    gmask = lane < N_GROUPS
    gmax = jnp.max(jnp.where(gmask, lg, ninf), axis=1, keepdims=True)
    gsum = jnp.sum(jnp.where(gmask, jnp.exp(lg - gmax), 0.0), axis=1, keepdims=True)
    g_idx = jnp.min(jnp.where(gmask & (lg == gmax), lane, LANES), axis=1, keepdims=True)
    g_p = 1.0 / gsum
    lo = N_GROUPS + EXP_PER_GROUP * g_idx
    emask = (lane >= lo) & (lane < lo + EXP_PER_GROUP)
    el = jnp.where(emask, lg, ninf)
    emax = jnp.max(el, axis=1, keepdims=True)
    i1 = jnp.min(jnp.where(el == emax, lane, LANES), axis=1, keepdims=True)
    el2 = jnp.where(lane == i1, ninf, el)
    emax2 = jnp.max(el2, axis=1, keepdims=True)
    i2 = jnp.min(jnp.where(el2 == emax2, lane, LANES), axis=1, keepdims=True)
    r = jnp.exp(emax2 - emax)
    w1 = g_p / (1.0 + r)
    w2 = g_p * r / (1.0 + r)
    e1 = (i1 - N_GROUPS).astype(f32)
    e2 = (i2 - N_GROUPS).astype(f32)
    route = jnp.where(lane == ROUTE_E1, e1, jnp.where(lane == ROUTE_E2, e2,
                      jnp.where(lane == ROUTE_W1, w1, jnp.where(lane == ROUTE_W2, w2, 0.0))))
    route_ref[...] = route
    routeT_ref[...] = route.T[:8, :]
    cnt = jnp.sum(jnp.where((lane == i1) | (lane == i2), 1.0, 0.0), axis=0, keepdims=True)
    cnt_ref[0] = jnp.broadcast_to(cnt, (8, LANES))


def _moe_ffn_kernel(seg_start_ref, seg_nblk_ref, h_ref, route_ref, routeT_ref, lrow_ref, lcol_ref, wgu_ref, wd_ref,
                    x_ref, gate_ref, fg_ref, o_ref, xs_ref, yw_ref, ws_ref, *, final_norm):
    f32, bf16 = jnp.float32, jnp.bfloat16
    t = pl.program_id(0)
    step = pl.program_id(1)
    tm = h_ref.shape[0]

    @pl.when(step == 0)
    def _dispatch():
        rT = routeT_ref[...]
        e1r, e2r = rT[ROUTE_E1:ROUTE_E1 + 1], rT[ROUTE_E2:ROUTE_E2 + 1]
        w1r, w2r = rT[ROUTE_W1:ROUTE_W1 + 1], rT[ROUTE_W2:ROUTE_W2 + 1]
        eio = lax.broadcasted_iota(jnp.int32, (N_EXPERTS, tm), 0).astype(f32)
        oh1 = eio == e1r
        oh2 = eio == e2r
        oh = jnp.where(oh1 | oh2, 1.0, 0.0).astype(bf16)
        before = (lax.broadcasted_iota(jnp.int32, (tm, tm), 0)
                  < lax.broadcasted_iota(jnp.int32, (tm, tm), 1))
        rank = jnp.dot(oh, jnp.where(before, 1.0, 0.0).astype(bf16), preferred_element_type=f32)
        pos = lcol_ref[0][:, :1] + rank
        d1 = jnp.sum(jnp.where(oh1, pos, 0.0), axis=0, keepdims=True)
        d2 = jnp.sum(jnp.where(oh2, pos, 0.0), axis=0, keepdims=True)
        slot = lax.broadcasted_iota(jnp.int32, (MOE_SLOTS, tm), 0).astype(f32)
        m1 = slot == d1
        m2 = slot == d2
        perm = jnp.where(m1 | m2, 1.0, 0.0).astype(bf16)
        xs_ref[...] = jnp.dot(perm, h_ref[...], preferred_element_type=f32).astype(bf16)
        ws_ref[...] = jnp.sum(jnp.where(m1, w1r, 0.0) + jnp.where(m2, w2r, 0.0), axis=1, keepdims=True)
        yw_ref[...] = jnp.zeros(yw_ref.shape, yw_ref.dtype)

    rows = 2 * MOE_SEG

    for j in range(MOE_EXPERTS_PER_STEP):
        e = step * MOE_EXPERTS_PER_STEP + j
        start = seg_start_ref[t * N_EXPERTS + e]
        nblk = seg_nblk_ref[t * N_EXPERTS + e]

        def block(i, _, start=start, j=j):
            r0 = pl.multiple_of(start + i * rows, MOE_SEG)
            xb = xs_ref[pl.ds(r0, rows), :]
            gu = jnp.dot(xb, wgu_ref[j], preferred_element_type=f32)
            a = jax.nn.silu(gu[:, :D_EXPERT]) * gu[:, D_EXPERT:]
            y = jnp.dot(a.astype(bf16), wd_ref[j], preferred_element_type=f32)
            yw_ref[pl.ds(r0, rows), :] = (y * ws_ref[pl.ds(r0, rows), :]).astype(bf16)
            return 0

        lax.fori_loop(0, (nblk + 1) // 2, block, 0)

    @pl.when(step == N_EXPERTS // MOE_EXPERTS_PER_STEP - 1)
    def _combine():
        r = route_ref[...]
        e1c, e2c = r[:, ROUTE_E1:ROUTE_E1 + 1], r[:, ROUTE_E2:ROUTE_E2 + 1]
        lane = lax.broadcasted_iota(jnp.int32, (tm, LANES), 1).astype(f32)
        oh1 = lane == e1c
        oh2 = lane == e2c
        oh = jnp.where(oh1 | oh2, 1.0, 0.0).astype(bf16)
        after = (lax.broadcasted_iota(jnp.int32, (tm, tm), 1)
                 < lax.broadcasted_iota(jnp.int32, (tm, tm), 0))
        rank = jnp.dot(jnp.where(after, 1.0, 0.0).astype(bf16), oh, preferred_element_type=f32)
        pos = lrow_ref[0][:1, :] + rank
        d1 = jnp.sum(jnp.where(oh1, pos, 0.0), axis=1, keepdims=True)
        d2 = jnp.sum(jnp.where(oh2, pos, 0.0), axis=1, keepdims=True)
        slot = lax.broadcasted_iota(jnp.int32, (tm, MOE_SLOTS), 1).astype(f32)
        comb = jnp.where((slot == d1) | (slot == d2), 1.0, 0.0).astype(bf16)
        moe = jnp.dot(comb, yw_ref[...], preferred_element_type=f32)
        xn = x_ref[...] + gate_ref[0] * moe
        if final_norm:
            xn = xn * lax.rsqrt(jnp.mean(xn * xn, axis=-1, keepdims=True) + EPS) * fg_ref[...]
        o_ref[...] = xn


def moe_layer_pallas(x2, norm_g, sc2, sh2, g2, w_group, b_group, w_expert, b_expert, w_gate, w_up, w_down,
                     final_g, final_norm, B, L, interpret=False):
    f32, bf16 = jnp.float32, jnp.bfloat16
    N, D = x2.shape
    tm = MOE_TILE
    n_tiles = N // tm
    tpb = L // tm
    wr = jnp.zeros((D, LANES), f32).at[:, :N_GROUPS].set(w_group).at[:, N_GROUPS:N_GROUPS + N_EXPERTS].set(w_expert)
    br = jnp.zeros((1, LANES), f32).at[0, :N_GROUPS].set(b_group).at[0, N_GROUPS:N_GROUPS + N_EXPERTS].set(b_expert)
    row = lambda v: v.reshape(1, D).astype(f32)
    per_batch = lambda v: v.reshape(B, 1, D).astype(f32)
    cp = pltpu.CompilerParams(dimension_semantics=("parallel",), vmem_limit_bytes=VMEM_LIMIT_BYTES)
    h, route, routeT, cnt = pl.pallas_call(
        _moe_route_kernel, grid=(n_tiles,),
        in_specs=[pl.BlockSpec((tm, D), lambda i: (i, 0)),
                  pl.BlockSpec((1, D), lambda i: (0, 0)),
                  pl.BlockSpec((1, 1, D), lambda i: (i // tpb, 0, 0)),
                  pl.BlockSpec((1, 1, D), lambda i: (i // tpb, 0, 0)),
                  pl.BlockSpec((D, LANES), lambda i: (0, 0)),
                  pl.BlockSpec((1, LANES), lambda i: (0, 0))],
        out_specs=[pl.BlockSpec((tm, D), lambda i: (i, 0)),
                   pl.BlockSpec((tm, LANES), lambda i: (i, 0)),
                   pl.BlockSpec((8, tm), lambda i: (0, i)),
                   pl.BlockSpec((1, 8, LANES), lambda i: (i, 0, 0))],
        out_shape=[jax.ShapeDtypeStruct((N, D), bf16), jax.ShapeDtypeStruct((N, LANES), f32),
                   jax.ShapeDtypeStruct((8, N), f32), jax.ShapeDtypeStruct((n_tiles, 8, LANES), f32)],
        compiler_params=cp, interpret=interpret, name="moe_route",
    )(x2, row(norm_g), per_batch(sc2), per_batch(sh2), wr, br)
    counts = cnt[:, 0, N_GROUPS:N_GROUPS + N_EXPERTS].astype(jnp.int32)
    nblk = (counts + MOE_SEG - 1) // MOE_SEG
    starts = (jnp.cumsum(nblk, axis=1) - nblk) * MOE_SEG
    starts_f = starts.astype(f32)
    lrow = jnp.zeros((n_tiles, 8, LANES), f32).at[:, :, :N_EXPERTS].set(starts_f[:, None, :])
    lcol = jnp.broadcast_to(starts_f[:, :, None], (n_tiles, N_EXPERTS, LANES))
    wgu = jnp.concatenate([w_gate, w_up], axis=-1).astype(bf16)
    wd = w_down.astype(bf16)
    tile = lambda t, e, *_: (t, 0)
    out = pl.pallas_call(
        functools.partial(_moe_ffn_kernel, final_norm=final_norm),
        grid_spec=pltpu.PrefetchScalarGridSpec(
            num_scalar_prefetch=2, grid=(n_tiles, N_EXPERTS // MOE_EXPERTS_PER_STEP),
            in_specs=[pl.BlockSpec((tm, D), tile),
                      pl.BlockSpec((tm, LANES), tile),
                      pl.BlockSpec((8, tm), lambda t, e, *_: (0, t)),
                      pl.BlockSpec((1, 8, LANES), lambda t, e, *_: (t, 0, 0)),
                      pl.BlockSpec((1, N_EXPERTS, LANES), lambda t, e, *_: (t, 0, 0)),
                      pl.BlockSpec((MOE_EXPERTS_PER_STEP, D, 2 * D_EXPERT), lambda t, e, *_: (e, 0, 0)),
                      pl.BlockSpec((MOE_EXPERTS_PER_STEP, D_EXPERT, D), lambda t, e, *_: (e, 0, 0)),
                      pl.BlockSpec((tm, D), tile),
                      pl.BlockSpec((1, 1, D), lambda t, e, *_: (t // tpb, 0, 0)),
                      pl.BlockSpec((1, D), lambda t, e, *_: (0, 0))],
            out_specs=pl.BlockSpec((tm, D), tile),
            scratch_shapes=[pltpu.VMEM((MOE_SLOTS, D), bf16), pltpu.VMEM((MOE_SLOTS, D), bf16),
                            pltpu.VMEM((MOE_SLOTS, 1), f32)]),
        out_shape=jax.ShapeDtypeStruct((N, D), f32),
        compiler_params=pltpu.CompilerParams(dimension_semantics=("parallel", "arbitrary"),
                                             vmem_limit_bytes=VMEM_LIMIT_BYTES),
        interpret=interpret, name="moe_ffn",
    )(starts.reshape(-1), nblk.reshape(-1), h, route, routeT, lrow, lcol, wgu, wd, x2, per_batch(g2), row(final_g))
    return out


ROW_TILE = 256
GATE_LANES = 128


def _in_proj_kernel(x_ref, g_ref, sc_ref, sh_ref, wq_ref, wkv_ref, wg_ref, wu_ref, wm_ref,
                    q_ref, kv_ref, ng_ref, u_ref, mg_ref):
    f32 = jnp.float32
    x = x_ref[...]
    y = x * lax.rsqrt(jnp.mean(x * x, axis=-1, keepdims=True) + EPS)
    h = (y * g_ref[...] * (1.0 + sc_ref[0]) + sh_ref[0]).astype(jnp.bfloat16)
    q = jnp.dot(h, wq_ref[...], preferred_element_type=f32)
    q_ref[...] = (q * HEAD_DIM ** -0.5).astype(q_ref.dtype)
    kv_ref[...] = jnp.dot(h, wkv_ref[...], preferred_element_type=f32).astype(kv_ref.dtype)
    ng_ref[...] = jax.nn.sigmoid(jnp.dot(h, wg_ref[...], preferred_element_type=f32))
    u_ref[...] = jnp.dot(h, wu_ref[...], preferred_element_type=f32)
    mg_ref[...] = jax.nn.sigmoid(jnp.dot(h, wm_ref[...], preferred_element_type=f32)).astype(mg_ref.dtype)


def in_proj_pallas(x2, norm_g, sc1, sh1, w_in, B, L, interpret=False):
    f32, bf16 = jnp.float32, jnp.bfloat16
    N, D = x2.shape
    tm = ROW_TILE
    tpb = L // tm
    o1 = ATTN_WIDTH
    o2 = o1 + 6 * KV_WIDTH
    o3 = o2 + 3 * N_HEADS
    o4 = o3 + SSM_WIDTH
    wb = w_in.astype(bf16)
    wq, wkv, wu, wm = wb[:, :o1], wb[:, o1:o2], wb[:, o3:o4], wb[:, o4:]
    wg = jnp.pad(wb[:, o2:o3], ((0, 0), (0, GATE_LANES - 3 * N_HEADS)))
    const = lambda shape: pl.BlockSpec(shape, lambda i: (0,) * len(shape))
    rows = lambda w: pl.BlockSpec((tm, w), lambda i: (i, 0))
    per_batch = pl.BlockSpec((1, 1, D), lambda i: (i // tpb, 0, 0))
    return pl.pallas_call(
        _in_proj_kernel, grid=(N // tm,),
        in_specs=[rows(D), const((1, D)), per_batch, per_batch,
                  const(wq.shape), const(wkv.shape), const(wg.shape), const(wu.shape), const(wm.shape)],
        out_specs=[rows(o1), rows(o2 - o1), rows(GATE_LANES), rows(SSM_WIDTH), rows(2 * D)],
        out_shape=[jax.ShapeDtypeStruct((N, o1), bf16), jax.ShapeDtypeStruct((N, o2 - o1), bf16),
                   jax.ShapeDtypeStruct((N, GATE_LANES), f32), jax.ShapeDtypeStruct((N, SSM_WIDTH), f32),
                   jax.ShapeDtypeStruct((N, 2 * D), bf16)],
        compiler_params=pltpu.CompilerParams(dimension_semantics=("parallel",), vmem_limit_bytes=VMEM_LIMIT_BYTES),
        interpret=interpret, name="in_proj",
    )(x2, norm_g.reshape(1, D).astype(f32), sc1.reshape(B, 1, D).astype(f32), sh1.reshape(B, 1, D).astype(f32),
      wq, wkv, wg, wu, wm)


def _merge_kernel(attn_ref, y_ref, u_ref, d_ref, mg_ref, x_ref, g1_ref, wglu_ref, wso_ref, wao_ref, wout_ref, o_ref):
    f32, bf16 = jnp.float32, jnp.bfloat16
    D = x_ref.shape[1]
    y = jax.nn.gelu(y_ref[...].astype(f32) + d_ref[...] * u_ref[...])
    ssm = y * jax.nn.sigmoid(jnp.dot(y.astype(bf16), wglu_ref[...], preferred_element_type=f32))
    a = jnp.dot(attn_ref[...], wao_ref[...], preferred_element_type=f32)
    s = jnp.dot(ssm.astype(bf16), wso_ref[...], preferred_element_type=f32)
    mg = mg_ref[...].astype(f32)
    merged = mg[:, :D] * a + mg[:, D:] * s
    out = jnp.dot(merged.astype(bf16), wout_ref[...], preferred_element_type=f32)
    o_ref[...] = x_ref[...] + g1_ref[0] * out


def merge_pallas(attn, y, u, d_skip, mg, x2, g1, w_glu, ssm_w_o, nsa_w_o, w_out, B, L, interpret=False):
    bf16 = jnp.bfloat16
    N, D = x2.shape
    tm = ROW_TILE
    tpb = L // tm
    const = lambda shape: pl.BlockSpec(shape, lambda i: (0,) * len(shape))
    rows = lambda w: pl.BlockSpec((tm, w), lambda i: (i, 0))
    return pl.pallas_call(
        _merge_kernel, grid=(N // tm,),
        in_specs=[rows(ATTN_WIDTH), rows(SSM_WIDTH), rows(SSM_WIDTH), const((1, SSM_WIDTH)), rows(2 * D), rows(D),
                  pl.BlockSpec((1, 1, D), lambda i: (i // tpb, 0, 0)),
                  const(w_glu.shape), const(ssm_w_o.shape), const(nsa_w_o.shape), const(w_out.shape)],
        out_specs=rows(D),
        out_shape=jax.ShapeDtypeStruct((N, D), jnp.float32),
        compiler_params=pltpu.CompilerParams(dimension_semantics=("parallel",), vmem_limit_bytes=VMEM_LIMIT_BYTES),
        interpret=interpret, name="mixer_merge",
    )(attn, y, u, d_skip.reshape(1, SSM_WIDTH).astype(jnp.float32), mg, x2, g1.reshape(B, 1, D).astype(jnp.float32),
      w_glu.astype(bf16), ssm_w_o.astype(bf16), nsa_w_o.astype(bf16), w_out.astype(bf16))


def kernel(x, c, ada_w, ada_b, norm1_g, w_in, ssm_lam_re, ssm_lam_im, ssm_log_dt, ssm_b_re, ssm_b_im, ssm_c_re, ssm_c_im, ssm_d, ssm_w_glu, ssm_w_o, nsa_cmp_pos, nsa_cmp_w1, nsa_cmp_w2, nsa_w_o, w_out, norm2_g, moe_w_group, moe_b_group, moe_w_expert, moe_b_expert, moe_w_gate, moe_w_up, moe_w_down, final_g):
    B, L, D = x.shape
    N = B * L
    x2 = x.reshape(N, D)
    cs = jnp.pad(jax.nn.silu(c), ((0, 8 - B), (0, 0)))
    for l in range(DEPTH):
        mod = pmm(cs, ada_w[l])[:B] + ada_b[l]
        sh1, sc1, g1, sh2, sc2, g2 = jnp.split(mod, 6, axis=-1)
        q, kv, ng, u, mg = in_proj_pallas(x2, norm1_g[l], sc1, sh1, w_in[l], B, L)
        kvs = [kv[:, i * KV_WIDTH:(i + 1) * KV_WIDTH] for i in range(6)]
        attn = nsa_attention_pallas(q, kvs, ng[:, :3 * N_HEADS], nsa_cmp_pos[l], nsa_cmp_w1[l], nsa_cmp_w2[l], B, L)
        y = s5_scan_pallas(u, ssm_lam_re[l], ssm_lam_im[l], ssm_log_dt[l], ssm_b_re[l], ssm_b_im[l],
                           ssm_c_re[l], ssm_c_im[l], B, L)
        x2 = merge_pallas(attn, y, u, ssm_d[l], mg, x2, g1, ssm_w_glu[l], ssm_w_o[l], nsa_w_o[l], w_out[l], B, L)
        x2 = moe_layer_pallas(x2, norm2_g[l], sc2, sh2, g2, moe_w_group[l], moe_b_group[l],
                              moe_w_expert[l], moe_b_expert[l], moe_w_gate[l], moe_w_up[l], moe_w_down[l],
                              final_g, l == DEPTH - 1, B, L)
    return x2.reshape(B, L, D)
```
